```python
import math
import jax, jax.numpy as jnp
from jax import lax
import numpy as np

D_MODEL = 1024
BATCH = 4
SEQ = 4096
DEPTH = 1

NSA_HEADS = 8
NSA_KV_GROUPS = 2
NSA_HPG = NSA_HEADS // NSA_KV_GROUPS
NSA_HEAD_DIM = 64
NSA_WIDTH = NSA_HEADS * NSA_HEAD_DIM
NSA_KV_WIDTH = NSA_KV_GROUPS * NSA_HEAD_DIM
CMP_BLOCK = 32
CMP_STRIDE = 16
SLC_BLOCK = 64
SLC_TOPN = 16
WINDOW = 512
Q_BLOCK = 128
SLC_Q_BLOCK = 64
CONV_CH = 512
CONV_WIDTH = 31
PEER_HEADS = 8
PEER_NKEYS = 128
PEER_EXPERTS = PEER_NKEYS * PEER_NKEYS
PEER_QDIM = 256
PEER_HALF = PEER_QDIM // 2
PEER_TOPK = 16
PEER_CHUNK = 128

EPS = 1e-6
NEG = -1e30
FORCED = 1e9

IN_SPLIT_SIZES = (NSA_WIDTH,
                  NSA_KV_WIDTH, NSA_KV_WIDTH,
                  NSA_KV_WIDTH, NSA_KV_WIDTH,
                  NSA_KV_WIDTH, NSA_KV_WIDTH,
                  3 * NSA_HEADS,
                  2 * CONV_CH,
                  2 * D_MODEL)
IN_COLS = sum(IN_SPLIT_SIZES)

kernel_name = "hybrid_nsa_conformer_peer_block"


def rmsnorm(x, g):
    xf = x.astype(jnp.float32)
    y = xf * lax.rsqrt(jnp.mean(xf * xf, axis=-1, keepdims=True) + EPS)
    return (y * g.astype(jnp.float32)).astype(x.dtype)


def alibi_slopes(n):
    return jnp.asarray(np.array([2.0 ** (-8.0 * (h + 1) / n) for h in range(n)], np.float32))


def masked_softmax(s, mask):
    s = jnp.where(mask, s.astype(jnp.float32), NEG)
    return jnp.where(mask, jax.nn.softmax(s, axis=-1), 0.0)


def nsa_compressed(q, k, v, pe_k, pe_v, w_k1, w_k2, w_v1, w_v2, slopes_gh):
    B, S = q.shape[0], q.shape[1]
    n_cmp = (S - CMP_BLOCK) // CMP_STRIDE + 1
    idx = np.arange(n_cmp)[:, None] * CMP_STRIDE + np.arange(CMP_BLOCK)[None, :]

    def compress(t, pe, w1, w2):
        blk = t[:, idx] + pe[None, None, :, None, :]
        blk = blk.transpose(0, 1, 3, 2, 4).reshape(B, n_cmp, NSA_KV_GROUPS, CMP_BLOCK * NSA_HEAD_DIM)
        return jax.nn.gelu(blk @ w1, approximate=False) @ w2

    kc = compress(k, pe_k, w_k1, w_k2)
    vc = compress(v, pe_v, w_v1, w_v2)
    t = np.arange(S)
    end = idx[:, -1]
    mask = end[None, :] <= t[:, None]
    dist = (t[:, None] - end[None, :]).astype(np.float32)
    s = jnp.einsum('bsghd,bngd->bghsn', q, kc).astype(jnp.float32) / math.sqrt(NSA_HEAD_DIM)
    s = s - slopes_gh[:, :, None, None] * dist
    p = masked_softmax(s, mask)
    o = jnp.einsum('bghsn,bngd->bsghd', p.astype(vc.dtype), vc)
    return o, p, idx


def nsa_selected(q, k, v, p_cmp, cmp_idx, slopes_gh):
    B, S = q.shape[0], q.shape[1]
    n_slc = S // SLC_BLOCK
    n_sel = min(SLC_TOPN, n_slc)
    starts = cmp_idx[:, 0]
    ends = cmp_idx[:, -1]
    j = np.arange(n_slc)
    overlap = ((starts[:, None] <= (j[None, :] + 1) * SLC_BLOCK - 1) &
               (ends[:, None] >= j[None, :] * SLC_BLOCK)).astype(np.float32)
    imp = jnp.einsum('bghsn,nj->bgsj', p_cmp, overlap)
    t = np.arange(S)
    t_blk = t // SLC_BLOCK
    future = j[None, :] > t_blk[:, None]
    forced = (j[None, :] == 0) | (j[None, :] == t_blk[:, None]) | (j[None, :] == t_blk[:, None] - 1)
    imp = jnp.where(forced, FORCED, jnp.where(future, NEG, imp))
    _, sel = lax.top_k(imp, n_sel)

    kblk = k.reshape(B, n_slc, SLC_BLOCK, NSA_KV_GROUPS, NSA_HEAD_DIM).transpose(0, 3, 1, 2, 4)
    vblk = v.reshape(B, n_slc, SLC_BLOCK, NSA_KV_GROUPS, NSA_HEAD_DIM).transpose(0, 3, 1, 2, 4)
    nqb = S // SLC_Q_BLOCK
    qb_all = q.transpose(0, 2, 1, 3, 4).reshape(B, NSA_KV_GROUPS, nqb, SLC_Q_BLOCK, NSA_HPG, NSA_HEAD_DIM)
    qb_all = jnp.moveaxis(qb_all, 2, 0)
    sel_all = jnp.moveaxis(sel.reshape(B, NSA_KV_GROUPS, nqb, SLC_Q_BLOCK, n_sel), 2, 0)
    q0_all = jnp.arange(nqb, dtype=jnp.int32) * SLC_Q_BLOCK
    bi = jnp.arange(B)[:, None, None, None]
    gi = jnp.arange(NSA_KV_GROUPS)[None, :, None, None]
    n_keys = n_sel * SLC_BLOCK

    def body(xs):
        qb, sb, q0 = xs
        kg = kblk[bi, gi, sb].reshape(B, NSA_KV_GROUPS, SLC_Q_BLOCK, n_keys, NSA_HEAD_DIM)
        vg = vblk[bi, gi, sb].reshape(B, NSA_KV_GROUPS, SLC_Q_BLOCK, n_keys, NSA_HEAD_DIM)
        kpos = (sb[..., None] * SLC_BLOCK + jnp.arange(SLC_BLOCK)).reshape(B, NSA_KV_GROUPS, SLC_Q_BLOCK, n_keys)
        tq = q0 + jnp.arange(SLC_Q_BLOCK)
        diff = tq[None, None, :, None] - kpos
        s = jnp.einsum('bgqhd,bgqkd->bgqhk', qb, kg).astype(jnp.float32) / math.sqrt(NSA_HEAD_DIM)
        s = s - slopes_gh[None, :, None, :, None] * diff[:, :, :, None, :].astype(jnp.float32)
        p = masked_softmax(s, (diff >= 0)[:, :, :, None, :])
        return jnp.einsum('bgqhk,bgqkd->bgqhd', p.astype(vg.dtype), vg)

    o = lax.map(body, (qb_all, sel_all, q0_all))
    return o.transpose(1, 0, 3, 2, 4, 5).reshape(B, S, NSA_KV_GROUPS, NSA_HPG, NSA_HEAD_DIM)


def nsa_window(q, k, v, slopes_gh):
    B, S = q.shape[0], q.shape[1]
    nqb = S // Q_BLOCK
    span = WINDOW + Q_BLOCK
    kp = jnp.pad(k, ((0, 0), (WINDOW, 0), (0, 0), (0, 0)))
    vp = jnp.pad(v, ((0, 0), (WINDOW, 0), (0, 0), (0, 0)))
    idx = np.arange(nqb)[:, None] * Q_BLOCK + np.arange(span)[None, :]
    kw = kp[:, idx]
    vw = vp[:, idx]
    kpos = idx - WINDOW
    tq = np.arange(nqb)[:, None] * Q_BLOCK + np.arange(Q_BLOCK)[None, :]
    diff = tq[:, :, None] - kpos[:, None, :]
    mask = (diff >= 0) & (diff < WINDOW) & (kpos >= 0)[:, None, :]
    qw = q.reshape(B, nqb, Q_BLOCK, NSA_KV_GROUPS, NSA_HPG, NSA_HEAD_DIM)
    s = jnp.einsum('bcqghd,bckgd->bcghqk', qw, kw).astype(jnp.float32) / math.sqrt(NSA_HEAD_DIM)
    s = s - slopes_gh[None, None, :, :, None, None] * diff.astype(np.float32)[None, :, None, None, :, :]
    p = masked_softmax(s, mask[None, :, None, None, :, :])
    o = jnp.einsum('bcghqk,bckgd->bcqghd', p.astype(vw.dtype), vw)
    return o.reshape(B, S, NSA_KV_GROUPS, NSA_HPG, NSA_HEAD_DIM)


def conformer_conv(glu_in, w_dw, b_dw, g_ln, b_ln):
    a, b = jnp.split(glu_in, 2, axis=-1)
    u = a * jax.nn.sigmoid(b)
    u = lax.conv_general_dilated(u, w_dw, window_strides=(1,), padding=[(CONV_WIDTH - 1, 0)],
                                 dimension_numbers=('NWC', 'WIO', 'NWC'),
                                 feature_group_count=CONV_CH) + b_dw
    uf = u.astype(jnp.float32)
    mu = jnp.mean(uf, axis=-1, keepdims=True)
    var = jnp.mean(jnp.square(uf - mu), axis=-1, keepdims=True)
    un = ((uf - mu) * lax.rsqrt(var + EPS) * g_ln.astype(jnp.float32) + b_ln.astype(jnp.float32)).astype(u.dtype)
    return jax.nn.silu(un)


def peer_ffn(xn, w_q, sub_keys, u_tab, v_tab):
    B, S, D = xn.shape
    q = (xn @ w_q).reshape(B, S, PEER_HEADS, 2, PEER_HALF)
    s1 = jnp.einsum('bshd,hkd->bshk', q[..., 0, :], sub_keys[:, 0]).astype(jnp.float32)
    s2 = jnp.einsum('bshd,hkd->bshk', q[..., 1, :], sub_keys[:, 1]).astype(jnp.float32)
    v1, i1 = lax.top_k(s1, PEER_TOPK)
    v2, i2 = lax.top_k(s2, PEER_TOPK)
    cand = (v1[..., :, None] + v2[..., None, :]).reshape(B, S, PEER_HEADS, PEER_TOPK * PEER_TOPK)
    cs, ci = lax.top_k(cand, PEER_TOPK)
    e = (jnp.take_along_axis(i1, ci // PEER_TOPK, axis=-1) * PEER_NKEYS +
         jnp.take_along_axis(i2, ci % PEER_TOPK, axis=-1))
    g = jax.nn.softmax(cs, axis=-1)
    T = B * S
    nch = T // PEER_CHUNK
    xs = (xn.reshape(nch, PEER_CHUNK, D),
          e.reshape(nch, PEER_CHUNK, PEER_HEADS, PEER_TOPK),
          g.reshape(nch, PEER_CHUNK, PEER_HEADS, PEER_TOPK))

    def body(args):
        xc, ec, gc = args
        u = u_tab[ec]
        h = jnp.einsum('cd,chkd->chk', xc, u)
        a = (jax.nn.gelu(h.astype(jnp.float32), approximate=False) * gc).astype(xc.dtype)
        return jnp.einsum('chk,chkd->cd', a, v_tab[ec])

    return lax.map(body, xs).reshape(B, S, D)


def setup_inputs(seed: int = 0) -> dict:
    key = jax.random.key(seed)
    ks = jax.random.split(key, 24)
    L = DEPTH
    f = jnp.float32

    def nrm(k, shape, scale):
        return jax.random.normal(k, shape, f) * scale

    return {
        "x": jax.random.normal(ks[0], (BATCH, SEQ, D_MODEL), f),
        "g_mix": 1.0 + nrm(ks[1], (L, D_MODEL), 0.01),
        "w_in": nrm(ks[2], (L, D_MODEL, IN_COLS), D_MODEL ** -0.5),
        "pe_cmp_k": nrm(ks[3], (L, CMP_BLOCK, NSA_HEAD_DIM), 0.02),
        "pe_cmp_v": nrm(ks[4], (L, CMP_BLOCK, NSA_HEAD_DIM), 0.02),
        "w_cmp_k1": nrm(ks[5], (L, CMP_BLOCK * NSA_HEAD_DIM, NSA_HEAD_DIM), (CMP_BLOCK * NSA_HEAD_DIM) ** -0.5),
        "w_cmp_k2": nrm(ks[6], (L, NSA_HEAD_DIM, NSA_HEAD_DIM), NSA_HEAD_DIM ** -0.5),
        "w_cmp_v1": nrm(ks[7], (L, CMP_BLOCK * NSA_HEAD_DIM, NSA_HEAD_DIM), (CMP_BLOCK * NSA_HEAD_DIM) ** -0.5),
        "w_cmp_v2": nrm(ks[8], (L, NSA_HEAD_DIM, NSA_HEAD_DIM), NSA_HEAD_DIM ** -0.5),
        "w_nsa_out": nrm(ks[9], (L, NSA_WIDTH, D_MODEL), NSA_WIDTH ** -0.5),
        "w_dw": nrm(ks[10], (L, CONV_WIDTH, 1, CONV_CH), CONV_WIDTH ** -0.5),
        "b_dw": nrm(ks[11], (L, CONV_CH), 0.01),
        "g_conv_ln": 1.0 + nrm(ks[12], (L, CONV_CH), 0.01),
        "b_conv_ln": nrm(ks[13], (L, CONV_CH), 0.01),
        "w_conv_out": nrm(ks[14], (L, CONV_CH, D_MODEL), CONV_CH ** -0.5),
        "w_o": nrm(ks[15], (L, D_MODEL, D_MODEL), D_MODEL ** -0.5),
        "g_ffn": 1.0 + nrm(ks[16], (L, D_MODEL), 0.01),
        "w_peer_q": nrm(ks[17], (L, D_MODEL, PEER_HEADS * PEER_QDIM), D_MODEL ** -0.5),
        "peer_sub_keys": nrm(ks[18], (L, PEER_HEADS, 2, PEER_NKEYS, PEER_HALF), PEER_HALF ** -0.5),
        "peer_u": nrm(ks[19], (L, PEER_EXPERTS, D_MODEL), D_MODEL ** -0.5),
        "peer_v": nrm(ks[20], (L, PEER_EXPERTS, D_MODEL), (PEER_HEADS * PEER_TOPK) ** -0.5),
        "g_final": 1.0 + nrm(ks[21], (D_MODEL,), 0.01),
    }


def reference(x, g_mix, w_in, pe_cmp_k, pe_cmp_v, w_cmp_k1, w_cmp_k2, w_cmp_v1, w_cmp_v2,
              w_nsa_out, w_dw, b_dw, g_conv_ln, b_conv_ln, w_conv_out, w_o,
              g_ffn, w_peer_q, peer_sub_keys, peer_u, peer_v, g_final):
    B, S, D = x.shape
    slopes_gh = alibi_slopes(NSA_HEADS).reshape(NSA_KV_GROUPS, NSA_HPG)
    split_pts = np.cumsum(IN_SPLIT_SIZES)[:-1].tolist()
    kv_shape = (B, S, NSA_KV_GROUPS, NSA_HEAD_DIM)
    for l in range(DEPTH):
        xn = rmsnorm(x, g_mix[l])
        proj = xn @ w_in[l]
        (q, kc, vc, ksl, vsl, kwn, vwn, nsa_g, glu_in, merge_g) = jnp.split(proj, split_pts, axis=-1)
        q = q.reshape(B, S, NSA_KV_GROUPS, NSA_HPG, NSA_HEAD_DIM)
        o_cmp, p_cmp, cmp_idx = nsa_compressed(q, kc.reshape(kv_shape), vc.reshape(kv_shape),
                                               pe_cmp_k[l], pe_cmp_v[l], w_cmp_k1[l], w_cmp_k2[l],
                                               w_cmp_v1[l], w_cmp_v2[l], slopes_gh)
        o_slc = nsa_selected(q, ksl.reshape(kv_shape), vsl.reshape(kv_shape), p_cmp, cmp_idx, slopes_gh)
        o_win = nsa_window(q, kwn.reshape(kv_shape), vwn.reshape(kv_shape), slopes_gh)
        gts = jax.nn.sigmoid(nsa_g).reshape(B, S, 3, NSA_KV_GROUPS, NSA_HPG, 1)
        o_nsa = gts[:, :, 0] * o_cmp + gts[:, :, 1] * o_slc + gts[:, :, 2] * o_win
        y_a = o_nsa.reshape(B, S, NSA_WIDTH) @ w_nsa_out[l]
        y_b = conformer_conv(glu_in, w_dw[l], b_dw[l], g_conv_ln[l], b_conv_ln[l]) @ w_conv_out[l]
        g_a, g_b = jnp.split(jax.nn.sigmoid(merge_g), 2, axis=-1)
        x = x + (g_a * y_a + g_b * y_b) @ w_o[l]
        x = x + peer_ffn(rmsnorm(x, g_ffn[l]), w_peer_q[l], peer_sub_keys[l], peer_u[l], peer_v[l])
    return rmsnorm(x, g_final)
```

```python
import functools
import math

import jax
import jax.numpy as jnp
from jax import lax
from jax.experimental import pallas as pl
from jax.experimental.pallas import tpu as pltpu

F32 = jnp.float32
BF16 = jnp.bfloat16

NSA_HEADS = 8
NSA_KV_GROUPS = 2
NSA_HPG = NSA_HEADS // NSA_KV_GROUPS
NSA_HEAD_DIM = 64
NSA_WIDTH = NSA_HEADS * NSA_HEAD_DIM
NSA_KV_WIDTH = NSA_KV_GROUPS * NSA_HEAD_DIM
CMP_BLOCK = 32
CMP_STRIDE = 16
SLC_BLOCK = 64
SLC_TOPN = 16
WINDOW = 512
CONV_CH = 512
CONV_WIDTH = 31
PEER_HEADS = 8
PEER_NKEYS = 128
PEER_QDIM = 256
PEER_HALF = PEER_QDIM // 2
PEER_TOPK = 16
EPS = 1e-6
NEG = -1e30
FORCED = 1e9
SLOPES = tuple(2.0 ** (-8.0 * (h + 1) / NSA_HEADS) for h in range(NSA_HEADS))

LANE = 128
GATE_COLS = 3 * NSA_HEADS
GATE_PAD = LANE
QKV_COLS = NSA_WIDTH + 6 * NSA_KV_WIDTH
VMEM_LIMIT = 56 * 1024 * 1024

_NT = (((1,), (1,)), ((), ()))


def _cparams(*sem):
    return pltpu.CompilerParams(dimension_semantics=sem, vmem_limit_bytes=VMEM_LIMIT)


def _sigmoid(x):
    return 1.0 / (1.0 + jnp.exp(-x))


def _gelu(x):
    return 0.5 * x * (1.0 + lax.erf(x * (1.0 / math.sqrt(2.0))))


def _norm_proj_kernel(x_ref, g_ref, w_ref, *out_refs, splits):
    x = x_ref[...]
    xn = x * lax.rsqrt(jnp.mean(x * x, axis=-1, keepdims=True) + EPS) * g_ref[...]
    xb = xn.astype(BF16)
    off = 0
    for o_ref, n in zip(out_refs, splits):
        o_ref[...] = jnp.dot(xb, w_ref[:, off:off + n], preferred_element_type=F32)
        off += n


def _norm_proj(x2, g, w_bf16, splits, tm):
    t, d = x2.shape
    n = w_bf16.shape[1]
    return pl.pallas_call(
        functools.partial(_norm_proj_kernel, splits=splits),
        grid=(t // tm,),
        in_specs=[pl.BlockSpec((tm, d), lambda i: (i, 0)),
                  pl.BlockSpec((1, d), lambda i: (0, 0)),
                  pl.BlockSpec((d, n), lambda i: (0, 0))],
        out_specs=[pl.BlockSpec((tm, s), lambda i: (i, 0)) for s in splits],
        out_shape=[jax.ShapeDtypeStruct((t, s), F32) for s in splits],
        compiler_params=_cparams("parallel"),
        name="norm_in_proj",
    )(x2, g.reshape(1, d), w_bf16)


def _compress_kernel(k_ref, v_ref, pek_ref, pev_ref, wk1_ref, wk2_ref, wv1_ref, wv2_ref,
                     kc_ref, vc_ref, *, n_blk):
    hp = lax.Precision.HIGHEST
    dh = NSA_HEAD_DIM
    per = CMP_BLOCK // CMP_STRIDE
    assert per == 2

    def one(src_ref, pe_ref, w1_ref, w2_ref, dst_ref):
        pe8 = jnp.broadcast_to(pe_ref[...], (8, CMP_BLOCK * dh))
        const = jnp.dot(pe8, w1_ref[...], precision=hp, preferred_element_type=F32)[0:1, :]
        acc = [[jnp.zeros((n_blk, dh), F32) for _ in range(per)] for _ in range(NSA_KV_GROUPS)]
        for lo in range(CMP_STRIDE):
            rows = src_ref[pl.ds(lo, n_blk, stride=CMP_STRIDE), :]
            for g in range(NSA_KV_GROUPS):
                rg = rows[:, g * dh:(g + 1) * dh]
                for hi in range(per):
                    l = hi * CMP_STRIDE + lo
                    acc[g][hi] = acc[g][hi] + jnp.dot(rg, w1_ref[l * dh:(l + 1) * dh, :], precision=hp,
                                                     preferred_element_type=F32)
        for g in range(NSA_KV_GROUPS):
            pre = acc[g][0] + pltpu.roll(acc[g][1], n_blk - 1, 0) + const
            out = jnp.dot(_gelu(pre), w2_ref[...], precision=hp, preferred_element_type=F32)
            rid = lax.broadcasted_iota(jnp.int32, (n_blk, dh), 0)
            dst_ref[0, g] = jnp.where(rid < n_blk - 1, out, 0.0)

    one(k_ref, pek_ref, wk1_ref, wk2_ref, kc_ref)
    one(v_ref, pev_ref, wv1_ref, wv2_ref, vc_ref)


def _compress(qkv, b, s, pe_k, pe_v, wk1, wk2, wv1, wv2):
    n_blk = s // CMP_STRIDE
    dh = NSA_HEAD_DIM
    kcol = NSA_WIDTH // LANE
    full = lambda shape: pl.BlockSpec(shape, lambda i: tuple(0 for _ in shape))
    out_spec = pl.BlockSpec((1, NSA_KV_GROUPS, n_blk, dh), lambda i: (i, 0, 0, 0))
    out_shape = jax.ShapeDtypeStruct((b, NSA_KV_GROUPS, n_blk, dh), F32)
    return pl.pallas_call(
        functools.partial(_compress_kernel, n_blk=n_blk),
        grid=(b,),
        in_specs=[pl.BlockSpec((s, LANE), lambda i: (i, kcol)),
                  pl.BlockSpec((s, LANE), lambda i: (i, kcol + 1)),
                  full((1, CMP_BLOCK * dh)), full((1, CMP_BLOCK * dh)),
                  full((CMP_BLOCK * dh, dh)), full((dh, dh)),
                  full((CMP_BLOCK * dh, dh)), full((dh, dh))],
        out_specs=[out_spec, out_spec],
        out_shape=[out_shape, out_shape],
        compiler_params=_cparams("parallel"),
        name="nsa_compress",
    )(qkv, qkv, pe_k.reshape(1, -1), pe_v.reshape(1, -1), wk1, wk2, wv1, wv2)


def _cmp_attn_kernel(q_ref, kc_ref, vc_ref, o_ref, sel_ref, *, tq, n_blk, n_slc):
    hp = lax.Precision.HIGHEST
    dh = NSA_HEAD_DIM
    q0 = pl.program_id(1) * tq
    t = q0 + lax.broadcasted_iota(jnp.int32, (tq, n_blk), 0)
    end = lax.broadcasted_iota(jnp.int32, (tq, n_blk), 1) * CMP_STRIDE + (CMP_BLOCK - 1)
    dist = t - end
    mask = dist >= 0
    distf = dist.astype(F32)
    nn = lax.broadcasted_iota(jnp.int32, (n_blk, n_slc), 0) * CMP_STRIDE
    jj = lax.broadcasted_iota(jnp.int32, (n_blk, n_slc), 1) * SLC_BLOCK
    overlap = ((nn <= jj + SLC_BLOCK - 1) & (nn + CMP_BLOCK - 1 >= jj)).astype(F32)
    tj = q0 + lax.broadcasted_iota(jnp.int32, (tq, n_slc), 0)
    jb = lax.broadcasted_iota(jnp.int32, (tq, n_slc), 1)
    t_blk = tj // SLC_BLOCK
    future = jb > t_blk
    forced = (jb == 0) | (jb == t_blk) | (jb == t_blk - 1)
    n_sel = min(SLC_TOPN, n_slc)

    for g in range(NSA_KV_GROUPS):
        kc = kc_ref[0, g].astype(BF16)
        vc = vc_ref[0, g].astype(BF16)
        psum = jnp.zeros((tq, n_blk), F32)
        for h in range(NSA_HPG):
            hh = g * NSA_HPG + h
            qh = (q_ref[:, hh * dh:(hh + 1) * dh] * (1.0 / math.sqrt(dh))).astype(BF16)
            s = lax.dot_general(qh, kc, _NT, preferred_element_type=F32)
            s = jnp.where(mask, s - SLOPES[hh] * distf, NEG)
            m = jnp.max(s, axis=-1, keepdims=True)
            e = jnp.where(mask, jnp.exp(s - m), 0.0)
            l = jnp.sum(e, axis=-1, keepdims=True)
            p = e / jnp.where(l > 0.0, l, 1.0)
            psum = psum + p
            o_ref[:, hh * dh:(hh + 1) * dh] = jnp.dot(p.astype(BF16), vc, preferred_element_type=F32)
        imp = jnp.dot(psum, overlap, precision=hp, preferred_element_type=F32)
        imp = jnp.where(forced, FORCED, jnp.where(future, NEG, imp))
        cnt = jnp.zeros((tq, n_slc), F32)
        for j2 in range(n_slc):
            col = imp[:, j2:j2 + 1]
            ge = jnp.where(col >= imp, 1.0, 0.0)
            gt = jnp.where(col > imp, 1.0, 0.0)
            cnt = cnt + jnp.where(jb > j2, ge, gt)
        sel_ref[g] = (cnt < float(n_sel)).astype(F32)


def _cmp_attn(qkv, kcmp, vcmp, b, s, tq):
    t = b * s
    nq = s // tq
    n_blk = kcmp.shape[2]
    n_slc = s // SLC_BLOCK
    cmp_spec = pl.BlockSpec((1, NSA_KV_GROUPS, n_blk, NSA_HEAD_DIM), lambda bi, i: (bi, 0, 0, 0))
    return pl.pallas_call(
        functools.partial(_cmp_attn_kernel, tq=tq, n_blk=n_blk, n_slc=n_slc),
        grid=(b, nq),
        in_specs=[pl.BlockSpec((tq, NSA_WIDTH), lambda bi, i: (bi * nq + i, 0)), cmp_spec, cmp_spec],
        out_specs=[pl.BlockSpec((tq, NSA_WIDTH), lambda bi, i: (bi * nq + i, 0)),
                   pl.BlockSpec((NSA_KV_GROUPS, tq, n_slc), lambda bi, i: (0, bi * nq + i, 0))],
        out_shape=[jax.ShapeDtypeStruct((t, NSA_WIDTH), F32),
                   jax.ShapeDtypeStruct((NSA_KV_GROUPS, t, n_slc), F32)],
        compiler_params=_cparams("parallel", "parallel"),
        name="nsa_cmp_attn_select",
    )(qkv, kcmp, vcmp)


def _flash_kernel(*refs, mode, tq, tk):
    if mode == "slc":
        q_ref, k_ref, v_ref, sel_ref, o_ref, m_scr, l_scr, acc_scr = refs
    else:
        q_ref, k_ref, v_ref, o_ref, m_scr, l_scr, acc_scr = refs
        sel_ref = None
    dh = NSA_HEAD_DIM
    q0 = pl.program_id(1) * tq
    m_scr[...] = jnp.full(m_scr.shape, NEG, F32)
    l_scr[...] = jnp.zeros(l_scr.shape, F32)
    acc_scr[...] = jnp.zeros(acc_scr.shape, F32)
    lo = 0 if mode == "slc" else jnp.maximum(q0 - (WINDOW - 1), 0) // tk
    hi = (q0 + tq - 1) // tk + 1
    row = lax.broadcasted_iota(jnp.int32, (tq, tk), 0)
    col = lax.broadcasted_iota(jnp.int32, (tq, tk), 1)
    rc = row - col
    n_slc = None if sel_ref is None else sel_ref.shape[-1]

    def body(kt, carry):
        k0 = pl.multiple_of(kt * tk, tk)
        diff = rc + (q0 - k0)
        dfl = diff.astype(F32)
        base = diff >= 0
        if mode == "win":
            base = base & (diff < WINDOW)
        for g in range(NSA_KV_GROUPS):
            if mode == "slc":
                jr = lax.broadcasted_iota(jnp.int32, (n_slc, tk), 0)
                jc = lax.broadcasted_iota(jnp.int32, (n_slc, tk), 1) // SLC_BLOCK + k0 // SLC_BLOCK
                expand = (jr == jc).astype(BF16)
                selx = jnp.dot(sel_ref[g].astype(BF16), expand, preferred_element_type=F32)
                valid = base & (selx > 0.5)
            else:
                valid = base
            kg = k_ref[pl.ds(k0, tk), g * dh:(g + 1) * dh].astype(BF16)
            vg = v_ref[pl.ds(k0, tk), g * dh:(g + 1) * dh].astype(BF16)
            for h in range(NSA_HPG):
                hh = g * NSA_HPG + h
                qh = (q_ref[:, hh * dh:(hh + 1) * dh] * (1.0 / math.sqrt(dh))).astype(BF16)
                s = lax.dot_general(qh, kg, _NT, preferred_element_type=F32)
                s = jnp.where(valid, s - SLOPES[hh] * dfl, NEG)
                m_old = m_scr[hh]
                m_new = jnp.maximum(m_old, jnp.max(s, axis=-1, keepdims=True))
                p = jnp.where(valid, jnp.exp(s - m_new), 0.0)
                alpha = jnp.exp(m_old - m_new)
                l_scr[hh] = alpha * l_scr[hh] + jnp.sum(p, axis=-1, keepdims=True)
                acc_scr[hh] = alpha * acc_scr[hh] + jnp.dot(p.astype(BF16), vg, preferred_element_type=F32)
                m_scr[hh] = m_new
        return carry

    lax.fori_loop(lo, hi, body, 0)
    for hh in range(NSA_HEADS):
        o_ref[:, hh * dh:(hh + 1) * dh] = acc_scr[hh] / l_scr[hh]


def _flash(qkv, sel, b, s, mode, tq, tk):
    t = b * s
    nq = s // tq
    kcol = {"slc": NSA_WIDTH // LANE + 2, "win": NSA_WIDTH // LANE + 4}[mode]
    in_specs = [pl.BlockSpec((tq, NSA_WIDTH), lambda bi, i: (bi * nq + i, 0)),
                pl.BlockSpec((s, LANE), lambda bi, i: (bi, kcol)),
                pl.BlockSpec((s, LANE), lambda bi, i: (bi, kcol + 1))]
    args = [qkv, qkv, qkv]
    if mode == "slc":
        in_specs.append(pl.BlockSpec((NSA_KV_GROUPS, tq, sel.shape[-1]), lambda bi, i: (0, bi * nq + i, 0)))
        args.append(sel)
    return pl.pallas_call(
        functools.partial(_flash_kernel, mode=mode, tq=tq, tk=tk),
        grid=(b, nq),
        in_specs=in_specs,
        out_specs=pl.BlockSpec((tq, NSA_WIDTH), lambda bi, i: (bi * nq + i, 0)),
        out_shape=jax.ShapeDtypeStruct((t, NSA_WIDTH), F32),
        scratch_shapes=[pltpu.VMEM((NSA_HEADS, tq, 1), F32),
                        pltpu.VMEM((NSA_HEADS, tq, 1), F32),
                        pltpu.VMEM((NSA_HEADS, tq, NSA_HEAD_DIM), F32)],
        compiler_params=_cparams("parallel", "parallel"),
        name="nsa_flash_" + mode,
    )(*args)


HALO = 32


def _mix_kernel(ocmp_ref, oslc_ref, owin_ref, gates_ref, glu_ref, halo_ref, merge_ref, x_ref,
                wexp_ref, wnsa_ref, wdw_ref, bdw_ref, gln_ref, bln_ref, wconv_ref, wo_ref,
                o_ref, uext_scr, *, tm, tiles_per_seq):
    hp = lax.Precision.HIGHEST
    i = pl.program_id(0)
    gts = _sigmoid(gates_ref[...])
    gexp = jnp.dot(gts, wexp_ref[...], precision=hp, preferred_element_type=F32)
    w = NSA_WIDTH
    o_nsa = gexp[:, :w] * ocmp_ref[...] + gexp[:, w:2 * w] * oslc_ref[...] + gexp[:, 2 * w:] * owin_ref[...]
    y_a = jnp.dot(o_nsa.astype(BF16), wnsa_ref[...], preferred_element_type=F32)

    c = CONV_CH
    gl = glu_ref[...]
    u = gl[:, :c] * _sigmoid(gl[:, c:])
    hl = halo_ref[...]
    uh = hl[:, :c] * _sigmoid(hl[:, c:])
    uh = jnp.where(i % tiles_per_seq == 0, 0.0, uh)
    uext_scr[0:HALO, :] = uh
    uext_scr[HALO:HALO + tm, :] = u
    acc = jnp.zeros((tm, c), F32)
    for k in range(CONV_WIDTH):
        acc = acc + uext_scr[pl.ds(HALO - (CONV_WIDTH - 1) + k, tm), :] * wdw_ref[k:k + 1, :]
    cv = acc + bdw_ref[...]
    mu = jnp.mean(cv, axis=-1, keepdims=True)
    var = jnp.mean(jnp.square(cv - mu), axis=-1, keepdims=True)
    un = (cv - mu) * lax.rsqrt(var + EPS) * gln_ref[...] + bln_ref[...]
    act = un * _sigmoid(un)
    y_b = jnp.dot(act.astype(BF16), wconv_ref[...], preferred_element_type=F32)

    d = x_ref.shape[-1]
    mg = merge_ref[...]
    z = _sigmoid(mg[:, :d]) * y_a + _sigmoid(mg[:, d:]) * y_b
    o_ref[...] = x_ref[...] + jnp.dot(z.astype(BF16), wo_ref[...], preferred_element_type=F32)


def _mix(ocmp, oslc, owin, gates, glu, merge, x2, wexp, wnsa, wdw, bdw, gln, bln, wconv, wo, s, tm):
    t, d = x2.shape
    row = lambda n: pl.BlockSpec((tm, n), lambda i: (i, 0))
    full = lambda a: pl.BlockSpec(a.shape, lambda i: tuple(0 for _ in a.shape))
    halo_spec = pl.BlockSpec((HALO, glu.shape[1]), lambda i: (jnp.maximum(i * (tm // HALO) - 1, 0), 0))
    weights = [wexp, wnsa, wdw, bdw, gln, bln, wconv, wo]
    return pl.pallas_call(
        functools.partial(_mix_kernel, tm=tm, tiles_per_seq=s // tm),
        grid=(t // tm,),
        in_specs=[row(NSA_WIDTH), row(NSA_WIDTH), row(NSA_WIDTH), row(GATE_PAD), row(glu.shape[1]),
                  halo_spec, row(merge.shape[1]), row(d)] + [full(a) for a in weights],
        out_specs=row(d),
        out_shape=jax.ShapeDtypeStruct((t, d), F32),
        scratch_shapes=[pltpu.VMEM((HALO + tm, CONV_CH), F32)],
        compiler_params=_cparams("parallel"),
        name="mixer_merge",
    )(ocmp, oslc, owin, gates, glu, glu, merge, x2, *weights)


def _extract_topk(cur_ref, rank_ref, val_ref, n_rows, tm, k_top):
    rid = lax.broadcasted_iota(jnp.int32, (n_rows, tm), 0)

    def body(k, carry):
        cur = cur_ref[...]
        v = jnp.max(cur, axis=0, keepdims=True)
        idx = jnp.min(jnp.where(cur == v, rid, n_rows), axis=0, keepdims=True)
        hit = rid == idx
        rank_ref[...] = jnp.where(hit, k.astype(F32), rank_ref[...])
        cur_ref[...] = jnp.where(hit, -jnp.inf, cur)
        val_ref[pl.ds(k, 1), :] = v
        return carry

    lax.fori_loop(0, k_top, body, 0)


def _peer_stats_kernel(x_ref, g_ref, wq_ref, keys_ref, xn_ref, e1_ref, n1_ref, e2_ref, r2_ref,
                       cur_scr, r1_scr, r2_scr, v1_scr, v2_scr, cand_scr, csel_scr, cval_scr, *, tm):
    nk = PEER_NKEYS
    kt = PEER_TOPK
    x = x_ref[...]
    xn = (x * lax.rsqrt(jnp.mean(x * x, axis=-1, keepdims=True) + EPS) * g_ref[...]).astype(BF16)
    xn_ref[...] = xn
    big = float(nk)

    def head(h, carry):
        qp = jnp.dot(xn, wq_ref[h], preferred_element_type=F32)
        s1 = lax.dot_general(keys_ref[h, 0], qp[:, :PEER_HALF].astype(BF16), _NT,
                             preferred_element_type=F32)
        s2 = lax.dot_general(keys_ref[h, 1], qp[:, PEER_HALF:].astype(BF16), _NT,
                             preferred_element_type=F32)
        cur_scr[...] = s1
        r1_scr[...] = jnp.full((nk, tm), big, F32)
        _extract_topk(cur_scr, r1_scr, v1_scr, nk, tm, kt)
        cur_scr[...] = s2
        r2_scr[...] = jnp.full((nk, tm), big, F32)
        _extract_topk(cur_scr, r2_scr, v2_scr, nk, tm, kt)
        v1 = v1_scr[...]
        v2 = v2_scr[...]
        for a in range(kt):
            cand_scr[a * kt:(a + 1) * kt, :] = v1[a:a + 1, :] + v2
        cand = cand_scr[...]
        csel_scr[...] = jnp.full((kt * kt, tm), big, F32)
        _extract_topk(cand_scr, csel_scr, cval_scr, kt * kt, tm, kt)
        selected = csel_scr[...] < big
        top = v1[0:1, :] + v2[0:1, :]
        z = jnp.sum(jnp.where(selected, jnp.exp(cand - top), 0.0), axis=0, keepdims=True)
        r1 = r1_scr[...]
        n1 = jnp.zeros((nk, tm), F32)
        for a in range(kt):
            n_a = jnp.sum(selected[a * kt:(a + 1) * kt, :].astype(F32), axis=0, keepdims=True)
            n1 = jnp.where(r1 == float(a), n_a, n1)
        e1_ref[h] = jnp.exp(s1 - v1[0:1, :])
        n1_ref[h] = n1
        e2_ref[h] = jnp.exp(s2 - v2[0:1, :]) / z
        r2_ref[h] = r2_scr[...]
        return carry

    lax.fori_loop(0, PEER_HEADS, head, 0)


def _peer_stats(x1, g, wq_h, keys_bf16, tm):
    t, d = x1.shape
    nk, kt = PEER_NKEYS, PEER_TOPK
    stat_spec = pl.BlockSpec((PEER_HEADS, nk, tm), lambda i: (0, 0, i))
    stat_shape = jax.ShapeDtypeStruct((PEER_HEADS, nk, t), F32)
    full = lambda a: pl.BlockSpec(a.shape, lambda i: tuple(0 for _ in a.shape))
    return pl.pallas_call(
        functools.partial(_peer_stats_kernel, tm=tm),
        grid=(t // tm,),
        in_specs=[pl.BlockSpec((tm, d), lambda i: (i, 0)), pl.BlockSpec((1, d), lambda i: (0, 0)),
                  full(wq_h), full(keys_bf16)],
        out_specs=[pl.BlockSpec((tm, d), lambda i: (i, 0))] + [stat_spec] * 4,
        out_shape=[jax.ShapeDtypeStruct((t, d), BF16)] + [stat_shape] * 4,
        scratch_shapes=[pltpu.VMEM((nk, tm), F32), pltpu.VMEM((nk, tm), F32), pltpu.VMEM((nk, tm), F32),
                        pltpu.VMEM((kt, tm), F32), pltpu.VMEM((kt, tm), F32),
                        pltpu.VMEM((kt * kt, tm), F32), pltpu.VMEM((kt * kt, tm), F32),
                        pltpu.VMEM((kt, tm), F32)],
        compiler_params=_cparams("parallel"),
        name="peer_stats",
    )(x1, g.reshape(1, d), wq_h, keys_bf16)


def _peer_dense_kernel(xn_ref, u_ref, vt_ref, e1_ref, n1_ref, e2_ref, r2_ref, x1_ref, gf_ref,
                       o_ref, acc_scr, *, tm, eb):
    nk = PEER_NKEYS
    j = pl.program_id(1)

    @pl.when(j == 0)
    def _():
        acc_scr[...] = jnp.zeros(acc_scr.shape, F32)

    ht = lax.dot_general(u_ref[...], xn_ref[...], _NT, preferred_element_type=F32)
    act = _gelu(ht)
    parts = []
    for c in range(eb // nk):
        i1 = j * (eb // nk) + c
        gate = jnp.zeros((nk, tm), F32)
        for h in range(PEER_HEADS):
            n1 = n1_ref[h, pl.ds(i1, 1), :]
            e1 = e1_ref[h, pl.ds(i1, 1), :]
            gate = gate + jnp.where(r2_ref[h] < n1, e2_ref[h], 0.0) * e1
        parts.append((act[c * nk:(c + 1) * nk, :] * gate).astype(BF16))
    at = jnp.concatenate(parts, axis=0)
    acc_scr[...] += jnp.dot(vt_ref[...], at, preferred_element_type=F32)

    @pl.when(j == pl.num_programs(1) - 1)
    def _():
        y = x1_ref[...] + acc_scr[...].T
        o_ref[...] = y * lax.rsqrt(jnp.mean(y * y, axis=-1, keepdims=True) + EPS) * gf_ref[...]


def _peer_dense(xn, u_bf16, vt_bf16, e1, n1, e2, r2, x1, g_final, tm, eb):
    t, d = x1.shape
    n_exp = u_bf16.shape[0]
    stat_spec = pl.BlockSpec((PEER_HEADS, PEER_NKEYS, tm), lambda i, j: (0, 0, i))
    return pl.pallas_call(
        functools.partial(_peer_dense_kernel, tm=tm, eb=eb),
        grid=(t // tm, n_exp // eb),
        in_specs=[pl.BlockSpec((tm, d), lambda i, j: (i, 0)),
                  pl.BlockSpec((eb, d), lambda i, j: (j, 0)),
                  pl.BlockSpec((d, eb), lambda i, j: (0, j)),
                  stat_spec, stat_spec, stat_spec, stat_spec,
                  pl.BlockSpec((tm, d), lambda i, j: (i, 0)),
                  pl.BlockSpec((1, d), lambda i, j: (0, 0))],
        out_specs=pl.BlockSpec((tm, d), lambda i, j: (i, 0)),
        out_shape=jax.ShapeDtypeStruct((t, d), F32),
        scratch_shapes=[pltpu.VMEM((d, tm), F32)],
        compiler_params=_cparams("parallel", "arbitrary"),
        name="peer_dense",
    )(xn, u_bf16, vt_bf16, e1, n1, e2, r2, x1, g_final.reshape(1, d))


def _gate_expand_matrix():
    r = jnp.arange(GATE_PAD)[:, None]
    c = jnp.arange(3 * NSA_WIDTH)[None, :]
    return ((r < GATE_COLS) & (r == (c // NSA_WIDTH) * NSA_HEADS + (c % NSA_WIDTH) // NSA_HEAD_DIM)).astype(F32)


def kernel(x, g_mix, w_in, pe_cmp_k, pe_cmp_v, w_cmp_k1, w_cmp_k2, w_cmp_v1, w_cmp_v2, w_nsa_out, w_dw, b_dw,
           g_conv_ln, b_conv_ln, w_conv_out, w_o, g_ffn, w_peer_q, peer_sub_keys, peer_u, peer_v, g_final):
    b, s, d = x.shape
    t = b * s
    depth = w_in.shape[0]
    assert depth == 1, "the fused final norm assumes a single layer"
    assert s % 512 == 0 and d % LANE == 0
    x2 = x.reshape(t, d)
    wexp = _gate_expand_matrix()
    l = 0
    gate_end = QKV_COLS + GATE_COLS
    w_pad = jnp.concatenate([w_in[l][:, :gate_end], jnp.zeros((d, GATE_PAD - GATE_COLS), F32),
                             w_in[l][:, gate_end:]], axis=1).astype(BF16)
    splits = (QKV_COLS, GATE_PAD, 2 * CONV_CH, 2 * d)
    qkv, gates, glu, merge = _norm_proj(x2, g_mix[l], w_pad, splits, tm=256)

    kcmp, vcmp = _compress(qkv, b, s, pe_cmp_k[l], pe_cmp_v[l], w_cmp_k1[l], w_cmp_k2[l],
                           w_cmp_v1[l], w_cmp_v2[l])
    o_cmp, sel = _cmp_attn(qkv, kcmp, vcmp, b, s, tq=256)
    o_slc = _flash(qkv, sel, b, s, "slc", tq=256, tk=256)
    o_win = _flash(qkv, None, b, s, "win", tq=256, tk=256)

    x1 = _mix(o_cmp, o_slc, o_win, gates, glu, merge, x2, wexp,
              w_nsa_out[l].astype(BF16), w_dw[l].reshape(CONV_WIDTH, CONV_CH), b_dw[l].reshape(1, -1),
              g_conv_ln[l].reshape(1, -1), b_conv_ln[l].reshape(1, -1),
              w_conv_out[l].astype(BF16), w_o[l].astype(BF16), s, tm=256)

    wq_h = w_peer_q[l].reshape(d, PEER_HEADS, PEER_QDIM).transpose(1, 0, 2).astype(BF16)
    xn, e1, n1, e2, r2 = _peer_stats(x1, g_ffn[l], wq_h, peer_sub_keys[l].astype(BF16), tm=256)
    out = _peer_dense(xn, peer_u[l].astype(BF16), peer_v[l].T.astype(BF16), e1, n1, e2, r2, x1, g_final,
                      tm=512, eb=512)
    return out.reshape(b, s, d)
```

```python
import functools
import math

import jax
import jax.numpy as jnp
from jax import lax
from jax.experimental import pallas as pl
from jax.experimental.pallas import tpu as pltpu

F32 = jnp.float32
BF16 = jnp.bfloat16

NSA_HEADS = 8
NSA_KV_GROUPS = 2
NSA_HPG = NSA_HEADS // NSA_KV_GROUPS
NSA_HEAD_DIM = 64
NSA_WIDTH = NSA_HEADS * NSA_HEAD_DIM
NSA_KV_WIDTH = NSA_KV_GROUPS * NSA_HEAD_DIM
CMP_BLOCK = 32
CMP_STRIDE = 16
SLC_BLOCK = 64
SLC_TOPN = 16
WINDOW = 512
CONV_CH = 512
CONV_WIDTH = 31
PEER_HEADS = 8
PEER_NKEYS = 128
PEER_QDIM = 256
PEER_HALF = PEER_QDIM // 2
PEER_TOPK = 16
EPS = 1e-6
NEG = -1e30
FORCED = 1e9
SLOPES = tuple(2.0 ** (-8.0 * (h + 1) / NSA_HEADS) for h in range(NSA_HEADS))

LANE = 128
GATE_COLS = 3 * NSA_HEADS
GATE_PAD = LANE
QKV_COLS = NSA_WIDTH + 6 * NSA_KV_WIDTH
VMEM_LIMIT = 56 * 1024 * 1024

_NT = (((1,), (1,)), ((), ()))
_TN = (((0,), (0,)), ((), ()))


def _cparams(*sem):
    return pltpu.CompilerParams(dimension_semantics=sem, vmem_limit_bytes=VMEM_LIMIT)


def _sigmoid(x):
    return 1.0 / (1.0 + jnp.exp(-x))


def _gelu(x):
    return 0.5 * x * (1.0 + lax.erf(x * (1.0 / math.sqrt(2.0))))


def _norm_proj_kernel(x_ref, g_ref, w_ref, *out_refs, splits):
    x = x_ref[...]
    xn = x * lax.rsqrt(jnp.mean(x * x, axis=-1, keepdims=True) + EPS) * g_ref[...]
    xb = xn.astype(BF16)
    off = 0
    for o_ref, n in zip(out_refs, splits):
        o_ref[...] = jnp.dot(xb, w_ref[:, off:off + n], preferred_element_type=F32)
        off += n


def _norm_proj(x2, g, w_bf16, splits, tm):
    t, d = x2.shape
    n = w_bf16.shape[1]
    return pl.pallas_call(
        functools.partial(_norm_proj_kernel, splits=splits),
        grid=(t // tm,),
        in_specs=[pl.BlockSpec((tm, d), lambda i: (i, 0)),
                  pl.BlockSpec((1, d), lambda i: (0, 0)),
                  pl.BlockSpec((d, n), lambda i: (0, 0))],
        out_specs=[pl.BlockSpec((tm, s), lambda i: (i, 0)) for s in splits],
        out_shape=[jax.ShapeDtypeStruct((t, s), F32) for s in splits],
        compiler_params=_cparams("parallel"),
        name="norm_in_proj",
    )(x2, g.reshape(1, d), w_bf16)


def _compress_kernel(k_ref, v_ref, pek_ref, pev_ref, wk1_ref, wk2_ref, wv1_ref, wv2_ref,
                     kc_ref, vc_ref, *, n_blk):
    hp = lax.Precision.HIGHEST
    dh = NSA_HEAD_DIM
    per = CMP_BLOCK // CMP_STRIDE
    assert per == 2

    def one(src_ref, pe_ref, w1_ref, w2_ref, dst_ref):
        pe8 = jnp.broadcast_to(pe_ref[...], (8, CMP_BLOCK * dh))
        const = jnp.dot(pe8, w1_ref[...], precision=hp, preferred_element_type=F32)[0:1, :]
        acc = [[jnp.zeros((n_blk, dh), F32) for _ in range(per)] for _ in range(NSA_KV_GROUPS)]
        for lo in range(CMP_STRIDE):
            rows = src_ref[pl.ds(lo, n_blk, stride=CMP_STRIDE), :]
            for g in range(NSA_KV_GROUPS):
                rg = rows[:, g * dh:(g + 1) * dh]
                for hi in range(per):
                    l = hi * CMP_STRIDE + lo
                    acc[g][hi] = acc[g][hi] + jnp.dot(rg, w1_ref[l * dh:(l + 1) * dh, :], precision=hp,
                                                     preferred_element_type=F32)
        for g in range(NSA_KV_GROUPS):
            pre = acc[g][0] + pltpu.roll(acc[g][1], n_blk - 1, 0) + const
            out = jnp.dot(_gelu(pre), w2_ref[...], precision=hp, preferred_element_type=F32)
            rid = lax.broadcasted_iota(jnp.int32, (n_blk, dh), 0)
            dst_ref[0, g] = jnp.where(rid < n_blk - 1, out, 0.0)

    one(k_ref, pek_ref, wk1_ref, wk2_ref, kc_ref)
    one(v_ref, pev_ref, wv1_ref, wv2_ref, vc_ref)


def _compress(qkv, b, s, pe_k, pe_v, wk1, wk2, wv1, wv2):
    n_blk = s // CMP_STRIDE
    dh = NSA_HEAD_DIM
    kcol = NSA_WIDTH // LANE
    full = lambda shape: pl.BlockSpec(shape, lambda i: tuple(0 for _ in shape))
    out_spec = pl.BlockSpec((1, NSA_KV_GROUPS, n_blk, dh), lambda i: (i, 0, 0, 0))
    out_shape = jax.ShapeDtypeStruct((b, NSA_KV_GROUPS, n_blk, dh), F32)
    return pl.pallas_call(
        functools.partial(_compress_kernel, n_blk=n_blk),
        grid=(b,),
        in_specs=[pl.BlockSpec((s, LANE), lambda i: (i, kcol)),
                  pl.BlockSpec((s, LANE), lambda i: (i, kcol + 1)),
                  full((1, CMP_BLOCK * dh)), full((1, CMP_BLOCK * dh)),
                  full((CMP_BLOCK * dh, dh)), full((dh, dh)),
                  full((CMP_BLOCK * dh, dh)), full((dh, dh))],
        out_specs=[out_spec, out_spec],
        out_shape=[out_shape, out_shape],
        compiler_params=_cparams("parallel"),
        name="nsa_compress",
    )(qkv, qkv, pe_k.reshape(1, -1), pe_v.reshape(1, -1), wk1, wk2, wv1, wv2)


def _cmp_attn_kernel(q_ref, kc_ref, vc_ref, o_ref, sel_ref, *, tq, n_blk, n_slc):
    hp = lax.Precision.HIGHEST
    dh = NSA_HEAD_DIM
    q0 = pl.program_id(1) * tq
    t = q0 + lax.broadcasted_iota(jnp.int32, (tq, n_blk), 0)
    end = lax.broadcasted_iota(jnp.int32, (tq, n_blk), 1) * CMP_STRIDE + (CMP_BLOCK - 1)
    dist = t - end
    mask = dist >= 0
    distf = dist.astype(F32)
    nn = lax.broadcasted_iota(jnp.int32, (n_blk, n_slc), 0) * CMP_STRIDE
    jj = lax.broadcasted_iota(jnp.int32, (n_blk, n_slc), 1) * SLC_BLOCK
    overlap = ((nn <= jj + SLC_BLOCK - 1) & (nn + CMP_BLOCK - 1 >= jj)).astype(F32)
    tj = q0 + lax.broadcasted_iota(jnp.int32, (tq, n_slc), 0)
    jb = lax.broadcasted_iota(jnp.int32, (tq, n_slc), 1)
    t_blk = tj // SLC_BLOCK
    future = jb > t_blk
    forced = (jb == 0) | (jb == t_blk) | (jb == t_blk - 1)
    n_sel = min(SLC_TOPN, n_slc)

    for g in range(NSA_KV_GROUPS):
        kc = kc_ref[0, g].astype(BF16)
        vc = vc_ref[0, g].astype(BF16)
        psum = jnp.zeros((tq, n_blk), F32)
        for h in range(NSA_HPG):
            hh = g * NSA_HPG + h
            qh = (q_ref[:, hh * dh:(hh + 1) * dh] * (1.0 / math.sqrt(dh))).astype(BF16)
            s = lax.dot_general(qh, kc, _NT, preferred_element_type=F32)
            s = jnp.where(mask, s - SLOPES[hh] * distf, NEG)
            m = jnp.max(s, axis=-1, keepdims=True)
            e = jnp.where(mask, jnp.exp(s - m), 0.0)
            l = jnp.sum(e, axis=-1, keepdims=True)
            p = e / jnp.where(l > 0.0, l, 1.0)
            psum = psum + p
            o_ref[:, hh * dh:(hh + 1) * dh] = jnp.dot(p.astype(BF16), vc, preferred_element_type=F32)
        imp = jnp.dot(psum, overlap, precision=hp, preferred_element_type=F32)
        imp = jnp.where(forced, FORCED, jnp.where(future, NEG, imp))
        cnt = jnp.zeros((tq, n_slc), F32)
        for j2 in range(n_slc):
            col = imp[:, j2:j2 + 1]
            ge = jnp.where(col >= imp, 1.0, 0.0)
            gt = jnp.where(col > imp, 1.0, 0.0)
            cnt = cnt + jnp.where(jb > j2, ge, gt)
        sel_ref[g] = (cnt < float(n_sel)).astype(F32)


def _cmp_attn(qkv, kcmp, vcmp, b, s, tq):
    t = b * s
    nq = s // tq
    n_blk = kcmp.shape[2]
    n_slc = s // SLC_BLOCK
    cmp_spec = pl.BlockSpec((1, NSA_KV_GROUPS, n_blk, NSA_HEAD_DIM), lambda bi, i: (bi, 0, 0, 0))
    return pl.pallas_call(
        functools.partial(_cmp_attn_kernel, tq=tq, n_blk=n_blk, n_slc=n_slc),
        grid=(b, nq),
        in_specs=[pl.BlockSpec((tq, NSA_WIDTH), lambda bi, i: (bi * nq + i, 0)), cmp_spec, cmp_spec],
        out_specs=[pl.BlockSpec((tq, NSA_WIDTH), lambda bi, i: (bi * nq + i, 0)),
                   pl.BlockSpec((NSA_KV_GROUPS, tq, n_slc), lambda bi, i: (0, bi * nq + i, 0))],
        out_shape=[jax.ShapeDtypeStruct((t, NSA_WIDTH), F32),
                   jax.ShapeDtypeStruct((NSA_KV_GROUPS, t, n_slc), F32)],
        compiler_params=_cparams("parallel", "parallel"),
        name="nsa_cmp_attn_select",
    )(qkv, kcmp, vcmp)


ATT_TILE = 256
AUG_SLC = 2 * LANE
AUG_WIN = LANE


def _attn_prep_kernel(ks_ref, kw_ref, kas_ref, kaw_ref, *, ts):
    dh = NSA_HEAD_DIM
    s0 = pl.program_id(1) * ts
    pos = s0 + lax.broadcasted_iota(jnp.int32, (ts, dh), 0)
    lane = lax.broadcasted_iota(jnp.int32, (ts, dh), 1)
    onehot = jnp.where(lane == pos // SLC_BLOCK, 1.0, 0.0)
    off = (pos % ATT_TILE).astype(F32)
    cols = jnp.where(lane == 0, 1.0, jnp.where(lane == 1, off, 0.0))
    zeros = jnp.zeros((ts, dh), F32)
    for g in range(NSA_KV_GROUPS):
        ks = ks_ref[:, g * dh:(g + 1) * dh]
        kas_ref[0, g] = jnp.concatenate([ks, onehot, cols, zeros], axis=1).astype(BF16)
        kw = kw_ref[:, g * dh:(g + 1) * dh]
        kaw_ref[0, g] = jnp.concatenate([kw, cols], axis=1).astype(BF16)


def _attn_prep(qkv, b, s, ts):
    kcol = NSA_WIDTH // LANE + 2
    return pl.pallas_call(
        functools.partial(_attn_prep_kernel, ts=ts),
        grid=(b, s // ts),
        in_specs=[pl.BlockSpec((ts, LANE), lambda bi, i: (bi * (s // ts) + i, kcol)),
                  pl.BlockSpec((ts, LANE), lambda bi, i: (bi * (s // ts) + i, kcol + 2))],
        out_specs=[pl.BlockSpec((1, NSA_KV_GROUPS, ts, AUG_SLC), lambda bi, i: (bi, 0, i, 0)),
                   pl.BlockSpec((1, NSA_KV_GROUPS, ts, AUG_WIN), lambda bi, i: (bi, 0, i, 0))],
        out_shape=[jax.ShapeDtypeStruct((b, NSA_KV_GROUPS, s, AUG_SLC), BF16),
                   jax.ShapeDtypeStruct((b, NSA_KV_GROUPS, s, AUG_WIN), BF16)],
        compiler_params=_cparams("parallel", "parallel"),
        name="nsa_key_augment",
    )(qkv, qkv)


def _flash_kernel(*refs, mode, tq):
    if mode == "slc":
        q_ref, ka_ref, v_ref, sel_ref, o_ref, qa_scr, m_scr, l_scr, acc_scr = refs
    else:
        q_ref, ka_ref, v_ref, o_ref, qa_scr, m_scr, l_scr, acc_scr = refs
        sel_ref = None
    dh = NSA_HEAD_DIM
    tk = tq
    rows = NSA_HPG * tq
    i = pl.program_id(1)
    q0 = i * tq
    qid = lax.broadcasted_iota(jnp.int32, (tk, rows), 1) % tq
    kid = lax.broadcasted_iota(jnp.int32, (tk, rows), 0)
    head_of_row = lax.broadcasted_iota(jnp.int32, (1, rows), 1) // tq
    rq = lax.broadcasted_iota(jnp.int32, (tq, dh), 0).astype(F32)
    lq = lax.broadcasted_iota(jnp.int32, (tq, dh), 1)

    for g in range(NSA_KV_GROUPS):
        slopes = [SLOPES[g * NSA_HPG + h] for h in range(NSA_HPG)]
        slope_row = jnp.full((1, rows), slopes[-1], F32)
        for h in range(NSA_HPG - 2, -1, -1):
            slope_row = jnp.where(head_of_row == h, slopes[h], slope_row)
        blocks = []
        for h in range(NSA_HPG):
            hh = g * NSA_HPG + h
            qh = q_ref[:, hh * dh:(hh + 1) * dh] * (1.0 / math.sqrt(dh))
            cols = jnp.where(lq == 0, -slopes[h] * rq, jnp.where(lq == 1, slopes[h], 0.0))
            if mode == "slc":
                n_slc = sel_ref.shape[-1]
                notsel = (sel_ref[g] - 1.0) * (-NEG)
                if n_slc < dh:
                    notsel = jnp.concatenate([notsel, jnp.zeros((tq, dh - n_slc), F32)], axis=1)
                parts = [qh, notsel, cols, jnp.zeros((tq, dh), F32)]
            else:
                parts = [qh, cols]
            blocks.append(jnp.concatenate(parts, axis=1))
        qa_scr[...] = jnp.concatenate(blocks, axis=0).astype(BF16)
        m_scr[...] = jnp.full(m_scr.shape, NEG, F32)
        l_scr[...] = jnp.zeros(l_scr.shape, F32)
        acc_scr[g * dh:(g + 1) * dh, :] = jnp.zeros((dh, rows), F32)

        def tiles(specs, g=g, slope_row=slope_row):
            sts, shifts, vgs = [], [], []
            for kt, mask in specs:
                k0 = pl.multiple_of(kt * tk, tk)
                ka = ka_ref[0, g, pl.ds(k0, tk), :]
                vgs.append(v_ref[pl.ds(k0, tk), g * dh:(g + 1) * dh].astype(BF16))
                st = lax.dot_general(ka, qa_scr[...], _NT, preferred_element_type=F32)
                sts.append(st if mask is None else jnp.where(mask, st, NEG))
                shifts.append(slope_row * (k0 - q0).astype(F32))
            m_old = m_scr[...]
            m_new = m_old
            for st, shift in zip(sts, shifts):
                m_new = jnp.maximum(m_new, jnp.max(st, axis=0, keepdims=True) + shift)
            alpha = jnp.exp(m_old - m_new)
            l_new = alpha * l_scr[...]
            acc = alpha * acc_scr[g * dh:(g + 1) * dh, :]
            for st, shift, vg in zip(sts, shifts, vgs):
                p = jnp.exp(st - (m_new - shift))
                l_new = l_new + jnp.sum(p, axis=0, keepdims=True)
                acc = acc + lax.dot_general(vg, p.astype(BF16), _TN, preferred_element_type=F32)
            l_scr[...] = l_new
            acc_scr[g * dh:(g + 1) * dh, :] = acc
            m_scr[...] = m_new

        diag = kid <= qid
        if mode == "slc":
            def body(k2, carry, tiles=tiles):
                tiles([(2 * k2, None), (2 * k2 + 1, None)])
                return carry
            lax.fori_loop(0, i // 2, body, 0)

            @pl.when(i % 2 == 1)
            def _(tiles=tiles):
                tiles([(i - 1, None), (i, diag)])

            @pl.when(i % 2 == 0)
            def _(tiles=tiles):
                tiles([(i, diag)])
        else:
            @pl.when(i == 0)
            def _(tiles=tiles):
                tiles([(i, diag)])

            @pl.when(i == 1)
            def _(tiles=tiles):
                tiles([(i, diag), (i - 1, None)])

            @pl.when(i >= 2)
            def _(tiles=tiles):
                tiles([(i, diag), (i - 1, None), (i - 2, kid > qid)])

        acc_scr[g * dh:(g + 1) * dh, :] = acc_scr[g * dh:(g + 1) * dh, :] / l_scr[...]

    out = acc_scr[...].T
    for g in range(NSA_KV_GROUPS):
        for h in range(NSA_HPG):
            hh = g * NSA_HPG + h
            o_ref[:, hh * dh:(hh + 1) * dh] = out[h * tq:(h + 1) * tq, g * dh:(g + 1) * dh]


def _flash(qkv, kaug, sel, b, s, mode):
    tq = ATT_TILE
    assert WINDOW == 2 * tq and s % tq == 0
    t = b * s
    nq = s // tq
    aug = kaug.shape[-1]
    vcol = NSA_WIDTH // LANE + {"slc": 3, "win": 5}[mode]
    in_specs = [pl.BlockSpec((tq, NSA_WIDTH), lambda bi, i: (bi * nq + i, 0)),
                pl.BlockSpec((1, NSA_KV_GROUPS, s, aug), lambda bi, i: (bi, 0, 0, 0)),
                pl.BlockSpec((s, LANE), lambda bi, i: (bi, vcol))]
    args = [qkv, kaug, qkv]
    if mode == "slc":
        assert sel.shape[-1] <= NSA_HEAD_DIM
        in_specs.append(pl.BlockSpec((NSA_KV_GROUPS, tq, sel.shape[-1]), lambda bi, i: (0, bi * nq + i, 0)))
        args.append(sel)
    rows = NSA_HPG * tq
    return pl.pallas_call(
        functools.partial(_flash_kernel, mode=mode, tq=tq),
        grid=(b, nq),
        in_specs=in_specs,
        out_specs=pl.BlockSpec((tq, NSA_WIDTH), lambda bi, i: (bi * nq + i, 0)),
        out_shape=jax.ShapeDtypeStruct((t, NSA_WIDTH), F32),
        scratch_shapes=[pltpu.VMEM((rows, aug), BF16),
                        pltpu.VMEM((1, rows), F32),
                        pltpu.VMEM((1, rows), F32),
                        pltpu.VMEM((NSA_KV_GROUPS * NSA_HEAD_DIM, rows), F32)],
        compiler_params=_cparams("parallel", "parallel"),
        name="nsa_flash_" + mode,
    )(*args)


HALO = 32


def _mix_kernel(ocmp_ref, oslc_ref, owin_ref, gates_ref, glu_ref, halo_ref, merge_ref, x_ref,
                wexp_ref, wnsa_ref, wdw_ref, bdw_ref, gln_ref, bln_ref, wconv_ref, wo_ref,
                o_ref, uext_scr, *, tm, tiles_per_seq):
    hp = lax.Precision.HIGHEST
    i = pl.program_id(0)
    gts = _sigmoid(gates_ref[...])
    gexp = jnp.dot(gts, wexp_ref[...], precision=hp, preferred_element_type=F32)
    w = NSA_WIDTH
    o_nsa = gexp[:, :w] * ocmp_ref[...] + gexp[:, w:2 * w] * oslc_ref[...] + gexp[:, 2 * w:] * owin_ref[...]
    y_a = jnp.dot(o_nsa.astype(BF16), wnsa_ref[...], preferred_element_type=F32)

    c = CONV_CH
    gl = glu_ref[...]
    u = gl[:, :c] * _sigmoid(gl[:, c:])
    hl = halo_ref[...]
    uh = hl[:, :c] * _sigmoid(hl[:, c:])
    uh = jnp.where(i % tiles_per_seq == 0, 0.0, uh)
    uext_scr[0:HALO, :] = uh
    uext_scr[HALO:HALO + tm, :] = u
    acc = jnp.zeros((tm, c), F32)
    for k in range(CONV_WIDTH):
        acc = acc + uext_scr[pl.ds(HALO - (CONV_WIDTH - 1) + k, tm), :] * wdw_ref[k:k + 1, :]
    cv = acc + bdw_ref[...]
    mu = jnp.mean(cv, axis=-1, keepdims=True)
    var = jnp.mean(jnp.square(cv - mu), axis=-1, keepdims=True)
    un = (cv - mu) * lax.rsqrt(var + EPS) * gln_ref[...] + bln_ref[...]
    act = un * _sigmoid(un)
    y_b = jnp.dot(act.astype(BF16), wconv_ref[...], preferred_element_type=F32)

    d = x_ref.shape[-1]
    mg = merge_ref[...]
    z = _sigmoid(mg[:, :d]) * y_a + _sigmoid(mg[:, d:]) * y_b
    o_ref[...] = x_ref[...] + jnp.dot(z.astype(BF16), wo_ref[...], preferred_element_type=F32)


def _mix(ocmp, oslc, owin, gates, glu, merge, x2, wexp, wnsa, wdw, bdw, gln, bln, wconv, wo, s, tm):
    t, d = x2.shape
    row = lambda n: pl.BlockSpec((tm, n), lambda i: (i, 0))
    full = lambda a: pl.BlockSpec(a.shape, lambda i: tuple(0 for _ in a.shape))
    halo_spec = pl.BlockSpec((HALO, glu.shape[1]), lambda i: (jnp.maximum(i * (tm // HALO) - 1, 0), 0))
    weights = [wexp, wnsa, wdw, bdw, gln, bln, wconv, wo]
    return pl.pallas_call(
        functools.partial(_mix_kernel, tm=tm, tiles_per_seq=s // tm),
        grid=(t // tm,),
        in_specs=[row(NSA_WIDTH), row(NSA_WIDTH), row(NSA_WIDTH), row(GATE_PAD), row(glu.shape[1]),
                  halo_spec, row(merge.shape[1]), row(d)] + [full(a) for a in weights],
        out_specs=row(d),
        out_shape=jax.ShapeDtypeStruct((t, d), F32),
        scratch_shapes=[pltpu.VMEM((HALO + tm, CONV_CH), F32)],
        compiler_params=_cparams("parallel"),
        name="mixer_merge",
    )(ocmp, oslc, owin, gates, glu, glu, merge, x2, *weights)


def _extract_topk(cur_ref, rank_ref, val_ref, n_rows, tm, k_top):
    rid = lax.broadcasted_iota(jnp.int32, (n_rows, tm), 0)

    def body(k, carry):
        cur = cur_ref[...]
        v = jnp.max(cur, axis=0, keepdims=True)
        idx = jnp.min(jnp.where(cur == v, rid, n_rows), axis=0, keepdims=True)
        hit = rid == idx
        rank_ref[...] = jnp.where(hit, k.astype(F32), rank_ref[...])
        cur_ref[...] = jnp.where(hit, -jnp.inf, cur)
        val_ref[pl.ds(k, 1), :] = v
        return carry

    lax.fori_loop(0, k_top, body, 0)


def _peer_stats_kernel(x_ref, g_ref, wq_ref, keys_ref, xn_ref, e1_ref, n1_ref, e2_ref, r2_ref,
                       cur_scr, r1_scr, r2_scr, v1_scr, v2_scr, cand_scr, csel_scr, cval_scr, *, tm):
    nk = PEER_NKEYS
    kt = PEER_TOPK
    x = x_ref[...]
    xn = (x * lax.rsqrt(jnp.mean(x * x, axis=-1, keepdims=True) + EPS) * g_ref[...]).astype(BF16)
    xn_ref[...] = xn
    big = float(nk)
    n_cand = cand_scr.shape[0]

    def head(h, carry):
        qp = jnp.dot(xn, wq_ref[h], preferred_element_type=F32)
        s1 = lax.dot_general(keys_ref[h, 0], qp[:, :PEER_HALF].astype(BF16), _NT,
                             preferred_element_type=F32)
        s2 = lax.dot_general(keys_ref[h, 1], qp[:, PEER_HALF:].astype(BF16), _NT,
                             preferred_element_type=F32)
        cur_scr[...] = s1
        r1_scr[...] = jnp.full((nk, tm), big, F32)
        _extract_topk(cur_scr, r1_scr, v1_scr, nk, tm, kt)
        cur_scr[...] = s2
        r2_scr[...] = jnp.full((nk, tm), big, F32)
        _extract_topk(cur_scr, r2_scr, v2_scr, nk, tm, kt)
        v1 = v1_scr[...]
        v2 = v2_scr[...]
        pieces = [v1[a:a + 1, :] + v2[0:kt // (a + 1), :] for a in range(kt)]
        pad = n_cand - sum(kt // (a + 1) for a in range(kt))
        pieces.append(jnp.full((pad, tm), -jnp.inf, F32))
        cand = jnp.concatenate(pieces, axis=0)
        cand_scr[...] = cand
        csel_scr[...] = jnp.full((n_cand, tm), big, F32)
        _extract_topk(cand_scr, csel_scr, cval_scr, n_cand, tm, kt)
        sel_f = jnp.where(csel_scr[...] < big, 1.0, 0.0)
        top = v1[0:1, :] + v2[0:1, :]
        z = jnp.sum(sel_f * jnp.exp(cand - top), axis=0, keepdims=True)
        r1 = r1_scr[...]
        n1 = jnp.zeros((nk, tm), F32)
        off = 0
        for a in range(kt):
            cnt = kt // (a + 1)
            n_a = jnp.sum(sel_f[off:off + cnt, :], axis=0, keepdims=True)
            off += cnt
            n1 = jnp.where(r1 == float(a), n_a, n1)
        e1_ref[h] = jnp.exp(s1 - v1[0:1, :])
        n1_ref[h] = n1
        e2_ref[h] = jnp.exp(s2 - v2[0:1, :]) / z
        r2_ref[h] = r2_scr[...]
        return carry

    lax.fori_loop(0, PEER_HEADS, head, 0)


def _peer_stats(x1, g, wq_h, keys_bf16, tm):
    t, d = x1.shape
    nk, kt = PEER_NKEYS, PEER_TOPK
    stat_spec = pl.BlockSpec((PEER_HEADS, nk, tm), lambda i: (0, 0, i))
    stat_shape = jax.ShapeDtypeStruct((PEER_HEADS, nk, t), F32)
    full = lambda a: pl.BlockSpec(a.shape, lambda i: tuple(0 for _ in a.shape))
    n_cells = sum(kt // (a + 1) for a in range(kt))
    n_cand = -(-n_cells // 8) * 8
    return pl.pallas_call(
        functools.partial(_peer_stats_kernel, tm=tm),
        grid=(t // tm,),
        in_specs=[pl.BlockSpec((tm, d), lambda i: (i, 0)), pl.BlockSpec((1, d), lambda i: (0, 0)),
                  full(wq_h), full(keys_bf16)],
        out_specs=[pl.BlockSpec((tm, d), lambda i: (i, 0))] + [stat_spec] * 4,
        out_shape=[jax.ShapeDtypeStruct((t, d), BF16)] + [stat_shape] * 4,
        scratch_shapes=[pltpu.VMEM((nk, tm), F32), pltpu.VMEM((nk, tm), F32), pltpu.VMEM((nk, tm), F32),
                        pltpu.VMEM((kt, tm), F32), pltpu.VMEM((kt, tm), F32),
                        pltpu.VMEM((n_cand, tm), F32), pltpu.VMEM((n_cand, tm), F32),
                        pltpu.VMEM((kt, tm), F32)],
        compiler_params=_cparams("parallel"),
        name="peer_stats",
    )(x1, g.reshape(1, d), wq_h, keys_bf16)


PEER_KEY_GROUP = 4


def _peer_dense_kernel(xn_ref, u_ref, vt_ref, e1_ref, n1_ref, e2_ref, r2_ref, o_ref, *, tm, eb, tc):
    nk = PEER_NKEYS
    j = pl.program_id(1)

    @pl.when(j == 0)
    def _():
        o_ref[...] = jnp.zeros(o_ref.shape, F32)

    u = u_ref[...]
    n_keys = eb // nk
    n1_rows = [[n1_ref[h, pl.ds(j * n_keys + c, 1), :].astype(BF16) for c in range(n_keys)]
               for h in range(PEER_HEADS)]
    e1_rows = [[e1_ref[h, pl.ds(j * n_keys + c, 1), :].astype(BF16) for c in range(n_keys)]
               for h in range(PEER_HEADS)]
    for ch in range(tm // tc):
        ht = lax.dot_general(u, xn_ref[ch * tc:(ch + 1) * tc, :], _NT, preferred_element_type=F32)
        act = _gelu(ht).astype(BF16)
        cols = []
        for lt in range(tc // LANE):
            lanes = slice(ch * tc + lt * LANE, ch * tc + (lt + 1) * LANE)
            gates = []
            for c0 in range(0, n_keys, PEER_KEY_GROUP):
                grp = [jnp.zeros((nk, LANE), BF16) for _ in range(PEER_KEY_GROUP)]
                for h in range(PEER_HEADS):
                    r2 = r2_ref[h, :, lanes].astype(BF16)
                    e2 = e2_ref[h, :, lanes].astype(BF16)
                    for c in range(PEER_KEY_GROUP):
                        n1 = n1_rows[h][c0 + c][:, lanes]
                        e1 = e1_rows[h][c0 + c][:, lanes]
                        grp[c] = grp[c] + jnp.where(r2 < n1, e2, jnp.zeros_like(e2)) * e1
                gates.extend(grp)
            cols.append(jnp.concatenate(gates, axis=0))
        gate = cols[0] if len(cols) == 1 else jnp.concatenate(cols, axis=1)
        at = act * gate
        o_ref[:, ch * tc:(ch + 1) * tc] += jnp.dot(vt_ref[...], at, preferred_element_type=F32)


def _peer_dense(xn, u_bf16, vt_bf16, e1, n1, e2, r2, tm, eb, tc):
    t, d = xn.shape
    n_exp = u_bf16.shape[0]
    stat_spec = pl.BlockSpec((PEER_HEADS, PEER_NKEYS, tm), lambda i, j: (0, 0, i))
    return pl.pallas_call(
        functools.partial(_peer_dense_kernel, tm=tm, eb=eb, tc=tc),
        grid=(t // tm, n_exp // eb),
        in_specs=[pl.BlockSpec((tm, d), lambda i, j: (i, 0)),
                  pl.BlockSpec((eb, d), lambda i, j: (j, 0)),
                  pl.BlockSpec((d, eb), lambda i, j: (0, j)),
                  stat_spec, stat_spec, stat_spec, stat_spec],
        out_specs=pl.BlockSpec((d, tm), lambda i, j: (0, i)),
        out_shape=jax.ShapeDtypeStruct((d, t), F32),
        compiler_params=_cparams("parallel", "arbitrary"),
        name="peer_dense",
    )(xn, u_bf16, vt_bf16, e1, n1, e2, r2)


def _final_kernel(x1_ref, yt_ref, g_ref, o_ref):
    y = x1_ref[...] + yt_ref[...].T
    o_ref[...] = y * lax.rsqrt(jnp.mean(y * y, axis=-1, keepdims=True) + EPS) * g_ref[...]


def _final_norm(x1, yt, g_final, tm):
    t, d = x1.shape
    return pl.pallas_call(
        _final_kernel,
        grid=(t // tm,),
        in_specs=[pl.BlockSpec((tm, d), lambda i: (i, 0)), pl.BlockSpec((d, tm), lambda i: (0, i)),
                  pl.BlockSpec((1, d), lambda i: (0, 0))],
        out_specs=pl.BlockSpec((tm, d), lambda i: (i, 0)),
        out_shape=jax.ShapeDtypeStruct((t, d), F32),
        compiler_params=_cparams("parallel"),
        name="peer_residual_final_norm",
    )(x1, yt, g_final.reshape(1, d))


def _gate_expand_matrix():
    r = jnp.arange(GATE_PAD)[:, None]
    c = jnp.arange(3 * NSA_WIDTH)[None, :]
    return ((r < GATE_COLS) & (r == (c // NSA_WIDTH) * NSA_HEADS + (c % NSA_WIDTH) // NSA_HEAD_DIM)).astype(F32)


def kernel(x, g_mix, w_in, pe_cmp_k, pe_cmp_v, w_cmp_k1, w_cmp_k2, w_cmp_v1, w_cmp_v2, w_nsa_out, w_dw, b_dw,
           g_conv_ln, b_conv_ln, w_conv_out, w_o, g_ffn, w_peer_q, peer_sub_keys, peer_u, peer_v, g_final):
    b, s, d = x.shape
    t = b * s
    depth = w_in.shape[0]
    assert depth == 1, "the fused final norm assumes a single layer"
    assert s % 512 == 0 and d % LANE == 0
    x2 = x.reshape(t, d)
    wexp = _gate_expand_matrix()
    l = 0
    gate_end = QKV_COLS + GATE_COLS
    w_pad = jnp.concatenate([w_in[l][:, :gate_end], jnp.zeros((d, GATE_PAD - GATE_COLS), F32),
                             w_in[l][:, gate_end:]], axis=1).astype(BF16)
    splits = (QKV_COLS, GATE_PAD, 2 * CONV_CH, 2 * d)
    qkv, gates, glu, merge = _norm_proj(x2, g_mix[l], w_pad, splits, tm=256)

    kcmp, vcmp = _compress(qkv, b, s, pe_cmp_k[l], pe_cmp_v[l], w_cmp_k1[l], w_cmp_k2[l],
                           w_cmp_v1[l], w_cmp_v2[l])
    o_cmp, sel = _cmp_attn(qkv, kcmp, vcmp, b, s, tq=256)
    kaug_slc, kaug_win = _attn_prep(qkv, b, s, ts=512)
    o_slc = _flash(qkv, kaug_slc, sel, b, s, "slc")
    o_win = _flash(qkv, kaug_win, None, b, s, "win")

    x1 = _mix(o_cmp, o_slc, o_win, gates, glu, merge, x2, wexp,
              w_nsa_out[l].astype(BF16), w_dw[l].reshape(CONV_WIDTH, CONV_CH), b_dw[l].reshape(1, -1),
              g_conv_ln[l].reshape(1, -1), b_conv_ln[l].reshape(1, -1),
              w_conv_out[l].astype(BF16), w_o[l].astype(BF16), s, tm=256)

    wq_h = w_peer_q[l].reshape(d, PEER_HEADS, PEER_QDIM).transpose(1, 0, 2).astype(BF16)
    xn, e1, n1, e2, r2 = _peer_stats(x1, g_ffn[l], wq_h, peer_sub_keys[l].astype(BF16), tm=256)
    yt = _peer_dense(xn, peer_u[l].astype(BF16), peer_v[l].T.astype(BF16), e1, n1, e2, r2,
                     tm=512, eb=1024, tc=256)
    out = _final_norm(x1, yt, g_final, tm=512)
    return out.reshape(b, s, d)
```

```python
import functools
import math

import jax
import jax.numpy as jnp
from jax import lax
from jax.experimental import pallas as pl
from jax.experimental.pallas import tpu as pltpu

F32 = jnp.float32
BF16 = jnp.bfloat16

NSA_HEADS = 8
NSA_KV_GROUPS = 2
NSA_HPG = NSA_HEADS // NSA_KV_GROUPS
NSA_HEAD_DIM = 64
NSA_WIDTH = NSA_HEADS * NSA_HEAD_DIM
NSA_KV_WIDTH = NSA_KV_GROUPS * NSA_HEAD_DIM
CMP_BLOCK = 32
CMP_STRIDE = 16
SLC_BLOCK = 64
SLC_TOPN = 16
WINDOW = 512
CONV_CH = 512
CONV_WIDTH = 31
PEER_HEADS = 8
PEER_NKEYS = 128
PEER_QDIM = 256
PEER_HALF = PEER_QDIM // 2
PEER_TOPK = 16
EPS = 1e-6
NEG = -1e30
FORCED = 1e9
SLOPES = tuple(2.0 ** (-8.0 * (h + 1) / NSA_HEADS) for h in range(NSA_HEADS))

LANE = 128
GATE_COLS = 3 * NSA_HEADS
GATE_PAD = LANE
QKV_COLS = NSA_WIDTH + 6 * NSA_KV_WIDTH
VMEM_LIMIT = 56 * 1024 * 1024

_NT = (((1,), (1,)), ((), ()))
_TN = (((0,), (0,)), ((), ()))


def _cparams(*sem):
    return pltpu.CompilerParams(dimension_semantics=sem, vmem_limit_bytes=VMEM_LIMIT)


def _sigmoid(x):
    return 1.0 / (1.0 + jnp.exp(-x))


def _gelu(x):
    return 0.5 * x * (1.0 + lax.erf(x * (1.0 / math.sqrt(2.0))))


def _norm_proj_kernel(x_ref, g_ref, w_ref, *out_refs, splits):
    x = x_ref[...]
    xn = x * lax.rsqrt(jnp.mean(x * x, axis=-1, keepdims=True) + EPS) * g_ref[...]
    xb = xn.astype(BF16)
    off = 0
    for o_ref, n in zip(out_refs, splits):
        o_ref[...] = jnp.dot(xb, w_ref[:, off:off + n], preferred_element_type=F32)
        off += n


def _norm_proj(x2, g, w_bf16, splits, tm):
    t, d = x2.shape
    n = w_bf16.shape[1]
    return pl.pallas_call(
        functools.partial(_norm_proj_kernel, splits=splits),
        grid=(t // tm,),
        in_specs=[pl.BlockSpec((tm, d), lambda i: (i, 0)),
                  pl.BlockSpec((1, d), lambda i: (0, 0)),
                  pl.BlockSpec((d, n), lambda i: (0, 0))],
        out_specs=[pl.BlockSpec((tm, s), lambda i: (i, 0)) for s in splits],
        out_shape=[jax.ShapeDtypeStruct((t, s), F32) for s in splits],
        compiler_params=_cparams("parallel"),
        name="norm_in_proj",
    )(x2, g.reshape(1, d), w_bf16)


def _compress_kernel(k_ref, v_ref, pek_ref, pev_ref, wk1_ref, wk2_ref, wv1_ref, wv2_ref,
                     kc_ref, vc_ref, *, n_blk):
    hp = lax.Precision.HIGHEST
    dh = NSA_HEAD_DIM
    per = CMP_BLOCK // CMP_STRIDE
    assert per == 2

    def one(src_ref, pe_ref, w1_ref, w2_ref, dst_ref):
        pe8 = jnp.broadcast_to(pe_ref[...], (8, CMP_BLOCK * dh))
        const = jnp.dot(pe8, w1_ref[...], precision=hp, preferred_element_type=F32)[0:1, :]
        acc = [[jnp.zeros((n_blk, dh), F32) for _ in range(per)] for _ in range(NSA_KV_GROUPS)]
        for lo in range(CMP_STRIDE):
            rows = src_ref[pl.ds(lo, n_blk, stride=CMP_STRIDE), :]
            for g in range(NSA_KV_GROUPS):
                rg = rows[:, g * dh:(g + 1) * dh]
                for hi in range(per):
                    l = hi * CMP_STRIDE + lo
                    acc[g][hi] = acc[g][hi] + jnp.dot(rg, w1_ref[l * dh:(l + 1) * dh, :], precision=hp,
                                                     preferred_element_type=F32)
        for g in range(NSA_KV_GROUPS):
            pre = acc[g][0] + pltpu.roll(acc[g][1], n_blk - 1, 0) + const
            out = jnp.dot(_gelu(pre), w2_ref[...], precision=hp, preferred_element_type=F32)
            rid = lax.broadcasted_iota(jnp.int32, (n_blk, dh), 0)
            dst_ref[0, g] = jnp.where(rid < n_blk - 1, out, 0.0)

    one(k_ref, pek_ref, wk1_ref, wk2_ref, kc_ref)
    one(v_ref, pev_ref, wv1_ref, wv2_ref, vc_ref)


def _compress(qkv, b, s, pe_k, pe_v, wk1, wk2, wv1, wv2):
    n_blk = s // CMP_STRIDE
    dh = NSA_HEAD_DIM
    kcol = NSA_WIDTH // LANE
    full = lambda shape: pl.BlockSpec(shape, lambda i: tuple(0 for _ in shape))
    out_spec = pl.BlockSpec((1, NSA_KV_GROUPS, n_blk, dh), lambda i: (i, 0, 0, 0))
    out_shape = jax.ShapeDtypeStruct((b, NSA_KV_GROUPS, n_blk, dh), F32)
    return pl.pallas_call(
        functools.partial(_compress_kernel, n_blk=n_blk),
        grid=(b,),
        in_specs=[pl.BlockSpec((s, LANE), lambda i: (i, kcol)),
                  pl.BlockSpec((s, LANE), lambda i: (i, kcol + 1)),
                  full((1, CMP_BLOCK * dh)), full((1, CMP_BLOCK * dh)),
                  full((CMP_BLOCK * dh, dh)), full((dh, dh)),
                  full((CMP_BLOCK * dh, dh)), full((dh, dh))],
        out_specs=[out_spec, out_spec],
        out_shape=[out_shape, out_shape],
        compiler_params=_cparams("parallel"),
        name="nsa_compress",
    )(qkv, qkv, pe_k.reshape(1, -1), pe_v.reshape(1, -1), wk1, wk2, wv1, wv2)


def _cmp_attn_kernel(q_ref, kc_ref, vc_ref, o_ref, sel_ref, *, tq, n_blk, n_slc):
    hp = lax.Precision.HIGHEST
    dh = NSA_HEAD_DIM
    q0 = pl.program_id(1) * tq
    t = q0 + lax.broadcasted_iota(jnp.int32, (tq, n_blk), 0)
    end = lax.broadcasted_iota(jnp.int32, (tq, n_blk), 1) * CMP_STRIDE + (CMP_BLOCK - 1)
    dist = t - end
    mask = dist >= 0
    distf = dist.astype(F32)
    nn = lax.broadcasted_iota(jnp.int32, (n_blk, n_slc), 0) * CMP_STRIDE
    jj = lax.broadcasted_iota(jnp.int32, (n_blk, n_slc), 1) * SLC_BLOCK
    overlap = ((nn <= jj + SLC_BLOCK - 1) & (nn + CMP_BLOCK - 1 >= jj)).astype(F32)
    tj = q0 + lax.broadcasted_iota(jnp.int32, (tq, n_slc), 0)
    jb = lax.broadcasted_iota(jnp.int32, (tq, n_slc), 1)
    t_blk = tj // SLC_BLOCK
    future = jb > t_blk
    forced = (jb == 0) | (jb == t_blk) | (jb == t_blk - 1)
    n_sel = min(SLC_TOPN, n_slc)

    for g in range(NSA_KV_GROUPS):
        kc = kc_ref[0, g].astype(BF16)
        vc = vc_ref[0, g].astype(BF16)
        psum = jnp.zeros((tq, n_blk), F32)
        for h in range(NSA_HPG):
            hh = g * NSA_HPG + h
            qh = (q_ref[:, hh * dh:(hh + 1) * dh] * (1.0 / math.sqrt(dh))).astype(BF16)
            s = lax.dot_general(qh, kc, _NT, preferred_element_type=F32)
            s = jnp.where(mask, s - SLOPES[hh] * distf, NEG)
            m = jnp.max(s, axis=-1, keepdims=True)
            e = jnp.where(mask, jnp.exp(s - m), 0.0)
            l = jnp.sum(e, axis=-1, keepdims=True)
            p = e / jnp.where(l > 0.0, l, 1.0)
            psum = psum + p
            o_ref[:, hh * dh:(hh + 1) * dh] = jnp.dot(p.astype(BF16), vc, preferred_element_type=F32)
        imp = jnp.dot(psum, overlap, precision=hp, preferred_element_type=F32)
        imp = jnp.where(forced, FORCED, jnp.where(future, NEG, imp))
        cnt = jnp.zeros((tq, n_slc), F32)
        for j2 in range(n_slc):
            col = imp[:, j2:j2 + 1]
            ge = jnp.where(col >= imp, 1.0, 0.0)
            gt = jnp.where(col > imp, 1.0, 0.0)
            cnt = cnt + jnp.where(jb > j2, ge, gt)
        sel_ref[g] = (cnt < float(n_sel)).astype(F32)


def _cmp_attn(qkv, kcmp, vcmp, b, s, tq):
    t = b * s
    nq = s // tq
    n_blk = kcmp.shape[2]
    n_slc = s // SLC_BLOCK
    cmp_spec = pl.BlockSpec((1, NSA_KV_GROUPS, n_blk, NSA_HEAD_DIM), lambda bi, i: (bi, 0, 0, 0))
    return pl.pallas_call(
        functools.partial(_cmp_attn_kernel, tq=tq, n_blk=n_blk, n_slc=n_slc),
        grid=(b, nq),
        in_specs=[pl.BlockSpec((tq, NSA_WIDTH), lambda bi, i: (bi * nq + i, 0)), cmp_spec, cmp_spec],
        out_specs=[pl.BlockSpec((tq, NSA_WIDTH), lambda bi, i: (bi * nq + i, 0)),
                   pl.BlockSpec((NSA_KV_GROUPS, tq, n_slc), lambda bi, i: (0, bi * nq + i, 0))],
        out_shape=[jax.ShapeDtypeStruct((t, NSA_WIDTH), F32),
                   jax.ShapeDtypeStruct((NSA_KV_GROUPS, t, n_slc), F32)],
        compiler_params=_cparams("parallel", "parallel"),
        name="nsa_cmp_attn_select",
    )(qkv, kcmp, vcmp)


ATT_TILE = 256
AUG_SLC = 2 * LANE
AUG_WIN = LANE


def _attn_prep_kernel(ks_ref, kw_ref, kas_ref, kaw_ref, *, ts):
    dh = NSA_HEAD_DIM
    s0 = pl.program_id(1) * ts
    pos = s0 + lax.broadcasted_iota(jnp.int32, (ts, dh), 0)
    lane = lax.broadcasted_iota(jnp.int32, (ts, dh), 1)
    onehot = jnp.where(lane == pos // SLC_BLOCK, 1.0, 0.0)
    off = (pos % ATT_TILE).astype(F32)
    cols = jnp.where(lane == 0, 1.0, jnp.where(lane == 1, off, 0.0))
    zeros = jnp.zeros((ts, dh), F32)
    for g in range(NSA_KV_GROUPS):
        ks = ks_ref[:, g * dh:(g + 1) * dh]
        kas_ref[0, g] = jnp.concatenate([ks, onehot, cols, zeros], axis=1).astype(BF16)
        kw = kw_ref[:, g * dh:(g + 1) * dh]
        kaw_ref[0, g] = jnp.concatenate([kw, cols], axis=1).astype(BF16)


def _attn_prep(qkv, b, s, ts):
    kcol = NSA_WIDTH // LANE + 2
    return pl.pallas_call(
        functools.partial(_attn_prep_kernel, ts=ts),
        grid=(b, s // ts),
        in_specs=[pl.BlockSpec((ts, LANE), lambda bi, i: (bi * (s // ts) + i, kcol)),
                  pl.BlockSpec((ts, LANE), lambda bi, i: (bi * (s // ts) + i, kcol + 2))],
        out_specs=[pl.BlockSpec((1, NSA_KV_GROUPS, ts, AUG_SLC), lambda bi, i: (bi, 0, i, 0)),
                   pl.BlockSpec((1, NSA_KV_GROUPS, ts, AUG_WIN), lambda bi, i: (bi, 0, i, 0))],
        out_shape=[jax.ShapeDtypeStruct((b, NSA_KV_GROUPS, s, AUG_SLC), BF16),
                   jax.ShapeDtypeStruct((b, NSA_KV_GROUPS, s, AUG_WIN), BF16)],
        compiler_params=_cparams("parallel", "parallel"),
        name="nsa_key_augment",
    )(qkv, qkv)


def _flash_kernel(*refs, mode, tq):
    if mode == "slc":
        q_ref, ka_ref, v_ref, sel_ref, o_ref, qa_scr, m_scr, l_scr, acc_scr = refs
    else:
        q_ref, ka_ref, v_ref, o_ref, qa_scr, m_scr, l_scr, acc_scr = refs
        sel_ref = None
    dh = NSA_HEAD_DIM
    tk = tq
    rows = NSA_HPG * tq
    i = pl.program_id(1)
    q0 = i * tq
    qid = lax.broadcasted_iota(jnp.int32, (tk, rows), 1) % tq
    kid = lax.broadcasted_iota(jnp.int32, (tk, rows), 0)
    head_of_row = lax.broadcasted_iota(jnp.int32, (1, rows), 1) // tq
    rq = lax.broadcasted_iota(jnp.int32, (tq, dh), 0).astype(F32)
    lq = lax.broadcasted_iota(jnp.int32, (tq, dh), 1)

    for g in range(NSA_KV_GROUPS):
        slopes = [SLOPES[g * NSA_HPG + h] for h in range(NSA_HPG)]
        slope_row = jnp.full((1, rows), slopes[-1], F32)
        for h in range(NSA_HPG - 2, -1, -1):
            slope_row = jnp.where(head_of_row == h, slopes[h], slope_row)
        blocks = []
        for h in range(NSA_HPG):
            hh = g * NSA_HPG + h
            qh = q_ref[:, hh * dh:(hh + 1) * dh] * (1.0 / math.sqrt(dh))
            cols = jnp.where(lq == 0, -slopes[h] * rq, jnp.where(lq == 1, slopes[h], 0.0))
            if mode == "slc":
                n_slc = sel_ref.shape[-1]
                notsel = (sel_ref[g] - 1.0) * (-NEG)
                if n_slc < dh:
                    notsel = jnp.concatenate([notsel, jnp.zeros((tq, dh - n_slc), F32)], axis=1)
                parts = [qh, notsel, cols, jnp.zeros((tq, dh), F32)]
            else:
                parts = [qh, cols]
            blocks.append(jnp.concatenate(parts, axis=1))
        qa_scr[...] = jnp.concatenate(blocks, axis=0).astype(BF16)
        m_scr[...] = jnp.full(m_scr.shape, NEG, F32)
        l_scr[...] = jnp.zeros(l_scr.shape, F32)
        acc_scr[g * dh:(g + 1) * dh, :] = jnp.zeros((dh, rows), F32)

        def tiles(specs, g=g, slope_row=slope_row):
            sts, shifts, vgs = [], [], []
            for kt, mask in specs:
                k0 = pl.multiple_of(kt * tk, tk)
                ka = ka_ref[0, g, pl.ds(k0, tk), :]
                vgs.append(v_ref[pl.ds(k0, tk), g * dh:(g + 1) * dh].astype(BF16))
                st = lax.dot_general(ka, qa_scr[...], _NT, preferred_element_type=F32)
                sts.append(st if mask is None else jnp.where(mask, st, NEG))
                shifts.append(slope_row * jnp.asarray(k0 - q0, F32))
            m_old = m_scr[...]
            m_new = m_old
            for st, shift in zip(sts, shifts):
                m_new = jnp.maximum(m_new, jnp.max(st, axis=0, keepdims=True) + shift)
            alpha = jnp.exp(m_old - m_new)
            l_new = alpha * l_scr[...]
            acc = alpha * acc_scr[g * dh:(g + 1) * dh, :]
            for st, shift, vg in zip(sts, shifts, vgs):
                p = jnp.exp(st - (m_new - shift))
                l_new = l_new + jnp.sum(p, axis=0, keepdims=True)
                acc = acc + lax.dot_general(vg, p.astype(BF16), _TN, preferred_element_type=F32)
            l_scr[...] = l_new
            acc_scr[g * dh:(g + 1) * dh, :] = acc
            m_scr[...] = m_new

        diag = kid <= qid
        if mode == "slc":
            def body(k2, carry, tiles=tiles):
                tiles([(2 * k2, None), (2 * k2 + 1, None)])
                return carry
            lax.fori_loop(0, i // 2, body, 0)

            @pl.when(i % 2 == 1)
            def _(tiles=tiles):
                tiles([(i - 1, None), (i, diag)])

            @pl.when(i % 2 == 0)
            def _(tiles=tiles):
                tiles([(i, diag)])
        else:
            @pl.when(i == 0)
            def _(tiles=tiles):
                tiles([(i, diag)])

            @pl.when(i == 1)
            def _(tiles=tiles):
                tiles([(i, diag), (i - 1, None)])

            @pl.when(i >= 2)
            def _(tiles=tiles):
                tiles([(i, diag), (i - 1, None), (i - 2, kid > qid)])

        acc_scr[g * dh:(g + 1) * dh, :] = acc_scr[g * dh:(g + 1) * dh, :] / l_scr[...]

    out = acc_scr[...].T
    for g in range(NSA_KV_GROUPS):
        for h in range(NSA_HPG):
            hh = g * NSA_HPG + h
            o_ref[:, hh * dh:(hh + 1) * dh] = out[h * tq:(h + 1) * tq, g * dh:(g + 1) * dh]


def _flash(qkv, kaug, sel, b, s, mode):
    tq = ATT_TILE
    assert WINDOW == 2 * tq and s % tq == 0
    t = b * s
    nq = s // tq
    aug = kaug.shape[-1]
    vcol = NSA_WIDTH // LANE + {"slc": 3, "win": 5}[mode]
    in_specs = [pl.BlockSpec((tq, NSA_WIDTH), lambda bi, i: (bi * nq + i, 0)),
                pl.BlockSpec((1, NSA_KV_GROUPS, s, aug), lambda bi, i: (bi, 0, 0, 0)),
                pl.BlockSpec((s, LANE), lambda bi, i: (bi, vcol))]
    args = [qkv, kaug, qkv]
    if mode == "slc":
        assert sel.shape[-1] <= NSA_HEAD_DIM
        in_specs.append(pl.BlockSpec((NSA_KV_GROUPS, tq, sel.shape[-1]), lambda bi, i: (0, bi * nq + i, 0)))
        args.append(sel)
    rows = NSA_HPG * tq
    return pl.pallas_call(
        functools.partial(_flash_kernel, mode=mode, tq=tq),
        grid=(b, nq),
        in_specs=in_specs,
        out_specs=pl.BlockSpec((tq, NSA_WIDTH), lambda bi, i: (bi * nq + i, 0)),
        out_shape=jax.ShapeDtypeStruct((t, NSA_WIDTH), F32),
        scratch_shapes=[pltpu.VMEM((rows, aug), BF16),
                        pltpu.VMEM((1, rows), F32),
                        pltpu.VMEM((1, rows), F32),
                        pltpu.VMEM((NSA_KV_GROUPS * NSA_HEAD_DIM, rows), F32)],
        compiler_params=_cparams("parallel", "parallel"),
        name="nsa_flash_" + mode,
    )(*args)


HALO = 32


def _mix_kernel(ocmp_ref, oslc_ref, owin_ref, gates_ref, glu_ref, halo_ref, merge_ref, x_ref,
                wexp_ref, wnsa_ref, wdw_ref, bdw_ref, gln_ref, bln_ref, wconv_ref, wo_ref,
                o_ref, uext_scr, *, tm, tiles_per_seq):
    hp = lax.Precision.HIGHEST
    i = pl.program_id(0)
    gts = _sigmoid(gates_ref[...])
    gexp = jnp.dot(gts, wexp_ref[...], precision=hp, preferred_element_type=F32)
    w = NSA_WIDTH
    o_nsa = gexp[:, :w] * ocmp_ref[...] + gexp[:, w:2 * w] * oslc_ref[...] + gexp[:, 2 * w:] * owin_ref[...]
    y_a = jnp.dot(o_nsa.astype(BF16), wnsa_ref[...], preferred_element_type=F32)

    c = CONV_CH
    gl = glu_ref[...]
    u = gl[:, :c] * _sigmoid(gl[:, c:])
    hl = halo_ref[...]
    uh = hl[:, :c] * _sigmoid(hl[:, c:])
    uh = jnp.where(i % tiles_per_seq == 0, 0.0, uh)
    uext_scr[0:HALO, :] = uh
    uext_scr[HALO:HALO + tm, :] = u
    acc = jnp.zeros((tm, c), F32)
    for k in range(CONV_WIDTH):
        acc = acc + uext_scr[pl.ds(HALO - (CONV_WIDTH - 1) + k, tm), :] * wdw_ref[k:k + 1, :]
    cv = acc + bdw_ref[...]
    mu = jnp.mean(cv, axis=-1, keepdims=True)
    var = jnp.mean(jnp.square(cv - mu), axis=-1, keepdims=True)
    un = (cv - mu) * lax.rsqrt(var + EPS) * gln_ref[...] + bln_ref[...]
    act = un * _sigmoid(un)
    y_b = jnp.dot(act.astype(BF16), wconv_ref[...], preferred_element_type=F32)

    d = x_ref.shape[-1]
    mg = merge_ref[...]
    z = _sigmoid(mg[:, :d]) * y_a + _sigmoid(mg[:, d:]) * y_b
    o_ref[...] = x_ref[...] + jnp.dot(z.astype(BF16), wo_ref[...], preferred_element_type=F32)


def _mix(ocmp, oslc, owin, gates, glu, merge, x2, wexp, wnsa, wdw, bdw, gln, bln, wconv, wo, s, tm):
    t, d = x2.shape
    row = lambda n: pl.BlockSpec((tm, n), lambda i: (i, 0))
    full = lambda a: pl.BlockSpec(a.shape, lambda i: tuple(0 for _ in a.shape))
    halo_spec = pl.BlockSpec((HALO, glu.shape[1]), lambda i: (jnp.maximum(i * (tm // HALO) - 1, 0), 0))
    weights = [wexp, wnsa, wdw, bdw, gln, bln, wconv, wo]
    return pl.pallas_call(
        functools.partial(_mix_kernel, tm=tm, tiles_per_seq=s // tm),
        grid=(t // tm,),
        in_specs=[row(NSA_WIDTH), row(NSA_WIDTH), row(NSA_WIDTH), row(GATE_PAD), row(glu.shape[1]),
                  halo_spec, row(merge.shape[1]), row(d)] + [full(a) for a in weights],
        out_specs=row(d),
        out_shape=jax.ShapeDtypeStruct((t, d), F32),
        scratch_shapes=[pltpu.VMEM((HALO + tm, CONV_CH), F32)],
        compiler_params=_cparams("parallel"),
        name="mixer_merge",
    )(ocmp, oslc, owin, gates, glu, glu, merge, x2, *weights)


def _extract_topk(jobs, k_top):
    def body(k, carry):
        for cur_ref, rank_ref, val_ref in jobs:
            cur = cur_ref[...]
            n_rows = cur.shape[0]
            rid = lax.broadcasted_iota(jnp.int32, cur.shape, 0)
            v = jnp.max(cur, axis=0, keepdims=True)
            idx = jnp.min(jnp.where(cur == v, rid, n_rows), axis=0, keepdims=True)
            hit = rid == idx
            rank_ref[...] = jnp.where(hit, jnp.asarray(k, F32), rank_ref[...])
            cur_ref[...] = jnp.where(hit, -jnp.inf, cur)
            val_ref[pl.ds(k, 1), :] = v
        return carry

    lax.fori_loop(0, k_top, body, 0)


def _peer_stats_kernel(x_ref, g_ref, wq_ref, keys_ref, xn_ref, e1_ref, n1_ref, e2_ref, r2_ref,
                       c1_scr, c2_scr, r1_scr, r2_scr, v1_scr, v2_scr, cand_scr, csel_scr, cval_scr, *, tm):
    nk = PEER_NKEYS
    kt = PEER_TOPK
    x = x_ref[...]
    xn = (x * lax.rsqrt(jnp.mean(x * x, axis=-1, keepdims=True) + EPS) * g_ref[...]).astype(BF16)
    xn_ref[...] = xn
    big = float(nk)
    n_cand = cand_scr.shape[0]

    def head(h, carry):
        qp = jnp.dot(xn, wq_ref[h], preferred_element_type=F32)
        s1 = lax.dot_general(keys_ref[h, 0], qp[:, :PEER_HALF].astype(BF16), _NT,
                             preferred_element_type=F32)
        s2 = lax.dot_general(keys_ref[h, 1], qp[:, PEER_HALF:].astype(BF16), _NT,
                             preferred_element_type=F32)
        c1_scr[...] = s1
        c2_scr[...] = s2
        r1_scr[...] = jnp.full((nk, tm), big, F32)
        r2_scr[...] = jnp.full((nk, tm), big, F32)
        _extract_topk([(c1_scr, r1_scr, v1_scr), (c2_scr, r2_scr, v2_scr)], kt)
        v1 = v1_scr[...]
        v2 = v2_scr[...]
        pieces = [v1[a:a + 1, :] + v2[0:kt // (a + 1), :] for a in range(kt)]
        pad = n_cand - sum(kt // (a + 1) for a in range(kt))
        pieces.append(jnp.full((pad, tm), -jnp.inf, F32))
        cand = jnp.concatenate(pieces, axis=0)
        cand_scr[...] = cand
        csel_scr[...] = jnp.full((n_cand, tm), big, F32)
        _extract_topk([(cand_scr, csel_scr, cval_scr)], kt)
        sel_f = jnp.where(csel_scr[...] < big, 1.0, 0.0)
        top = v1[0:1, :] + v2[0:1, :]
        z = jnp.sum(sel_f * jnp.exp(cand - top), axis=0, keepdims=True)
        r1 = r1_scr[...]
        n1 = jnp.zeros((nk, tm), F32)
        off = 0
        for a in range(kt):
            cnt = kt // (a + 1)
            n_a = jnp.sum(sel_f[off:off + cnt, :], axis=0, keepdims=True)
            off += cnt
            n1 = jnp.where(r1 == float(a), n_a, n1)
        e1_ref[h] = jnp.exp(s1 - v1[0:1, :])
        n1_ref[h] = n1
        e2_ref[h] = jnp.exp(s2 - v2[0:1, :]) / z
        r2_ref[h] = r2_scr[...]
        return carry

    lax.fori_loop(0, PEER_HEADS, head, 0)


def _peer_stats(x1, g, wq_h, keys_bf16, tm):
    t, d = x1.shape
    nk, kt = PEER_NKEYS, PEER_TOPK
    stat_spec = pl.BlockSpec((PEER_HEADS, nk, tm), lambda i: (0, 0, i))
    stat_shape = jax.ShapeDtypeStruct((PEER_HEADS, nk, t), F32)
    full = lambda a: pl.BlockSpec(a.shape, lambda i: tuple(0 for _ in a.shape))
    n_cells = sum(kt // (a + 1) for a in range(kt))
    n_cand = -(-n_cells // 8) * 8
    return pl.pallas_call(
        functools.partial(_peer_stats_kernel, tm=tm),
        grid=(t // tm,),
        in_specs=[pl.BlockSpec((tm, d), lambda i: (i, 0)), pl.BlockSpec((1, d), lambda i: (0, 0)),
                  full(wq_h), full(keys_bf16)],
        out_specs=[pl.BlockSpec((tm, d), lambda i: (i, 0))] + [stat_spec] * 4,
        out_shape=[jax.ShapeDtypeStruct((t, d), BF16)] + [stat_shape] * 4,
        scratch_shapes=[pltpu.VMEM((nk, tm), F32), pltpu.VMEM((nk, tm), F32),
                        pltpu.VMEM((nk, tm), F32), pltpu.VMEM((nk, tm), F32),
                        pltpu.VMEM((kt, tm), F32), pltpu.VMEM((kt, tm), F32),
                        pltpu.VMEM((n_cand, tm), F32), pltpu.VMEM((n_cand, tm), F32),
                        pltpu.VMEM((kt, tm), F32)],
        compiler_params=_cparams("parallel"),
        name="peer_stats",
    )(x1, g.reshape(1, d), wq_h, keys_bf16)


PEER_KEY_GROUP = 4


PEER_STAGES = 3


def _peer_dense_kernel(xn_ref, u_ref, vt_ref, e1_ref, n1_ref, e2_ref, r2_ref, o_ref, ht_scr, at_scr,
                       *, tm, eb, nb):
    nk = PEER_NKEYS
    s = pl.program_id(0)
    n_keys = eb // nk

    @pl.when(s == 0)
    def _():
        ht_scr[...] = jnp.zeros(ht_scr.shape, F32)
        at_scr[...] = jnp.zeros(at_scr.shape, BF16)

    @pl.when((s < PEER_STAGES) | ((s - (PEER_STAGES - 1)) % nb == 0))
    def _():
        o_ref[...] = jnp.zeros(o_ref.shape, F32)

    def step(slot):
        o_ref[...] += jnp.dot(vt_ref[...], at_scr[slot], preferred_element_type=F32)

        jb = jnp.maximum(s - 1, 0) % nb
        n1_rows = [[n1_ref[h, pl.ds(jb * n_keys + c, 1), :].astype(BF16) for c in range(n_keys)]
                   for h in range(PEER_HEADS)]
        e1_rows = [[e1_ref[h, pl.ds(jb * n_keys + c, 1), :].astype(BF16) for c in range(n_keys)]
                   for h in range(PEER_HEADS)]
        at_scr[1 - slot] = _gelu(ht_scr[1 - slot]).astype(BF16)
        for lt in range(tm // LANE):
            lanes = slice(lt * LANE, (lt + 1) * LANE)
            gates = []
            for c0 in range(0, n_keys, PEER_KEY_GROUP):
                grp = [jnp.zeros((nk, LANE), BF16) for _ in range(PEER_KEY_GROUP)]
                for h in range(PEER_HEADS):
                    r2 = r2_ref[h, :, lanes].astype(BF16)
                    e2 = e2_ref[h, :, lanes].astype(BF16)
                    for c in range(PEER_KEY_GROUP):
                        n1 = n1_rows[h][c0 + c][:, lanes]
                        e1 = e1_rows[h][c0 + c][:, lanes]
                        grp[c] = grp[c] + jnp.where(r2 < n1, e2, jnp.zeros_like(e2)) * e1
                gates.extend(grp)
            gate = jnp.concatenate(gates, axis=0)
            at_scr[1 - slot, :, lanes] = at_scr[1 - slot, :, lanes] * gate

        ht_scr[slot] = lax.dot_general(u_ref[...], xn_ref[...], _NT, preferred_element_type=F32)

    step(s % 2)


def _peer_dense(xn, u_bf16, vt_bf16, e1, n1, e2, r2, tm, eb):
    t, d = xn.shape
    n_exp = u_bf16.shape[0]
    nb = n_exp // eb
    n_items = (t // tm) * nb
    item = lambda s, lag: jnp.clip(s - lag, 0, n_items - 1)
    stat_spec = pl.BlockSpec((PEER_HEADS, PEER_NKEYS, tm), lambda s: (0, 0, item(s, 1) // nb))
    return pl.pallas_call(
        functools.partial(_peer_dense_kernel, tm=tm, eb=eb, nb=nb),
        grid=(n_items + PEER_STAGES - 1,),
        in_specs=[pl.BlockSpec((tm, d), lambda s: (item(s, 0) // nb, 0)),
                  pl.BlockSpec((eb, d), lambda s: (item(s, 0) % nb, 0)),
                  pl.BlockSpec((d, eb), lambda s: (0, item(s, 2) % nb)),
                  stat_spec, stat_spec, stat_spec, stat_spec],
        out_specs=pl.BlockSpec((d, tm), lambda s: (0, item(s, 2) // nb)),
        out_shape=jax.ShapeDtypeStruct((d, t), F32),
        scratch_shapes=[pltpu.VMEM((2, eb, tm), F32), pltpu.VMEM((2, eb, tm), BF16)],
        compiler_params=_cparams("arbitrary"),
        name="peer_dense",
    )(xn, u_bf16, vt_bf16, e1, n1, e2, r2)


def _final_kernel(x1_ref, yt_ref, g_ref, o_ref):
    y = x1_ref[...] + yt_ref[...].T
    o_ref[...] = y * lax.rsqrt(jnp.mean(y * y, axis=-1, keepdims=True) + EPS) * g_ref[...]


def _final_norm(x1, yt, g_final, tm):
    t, d = x1.shape
    return pl.pallas_call(
        _final_kernel,
        grid=(t // tm,),
        in_specs=[pl.BlockSpec((tm, d), lambda i: (i, 0)), pl.BlockSpec((d, tm), lambda i: (0, i)),
                  pl.BlockSpec((1, d), lambda i: (0, 0))],
        out_specs=pl.BlockSpec((tm, d), lambda i: (i, 0)),
        out_shape=jax.ShapeDtypeStruct((t, d), F32),
        compiler_params=_cparams("parallel"),
        name="peer_residual_final_norm",
    )(x1, yt, g_final.reshape(1, d))


def _gate_expand_matrix():
    r = jnp.arange(GATE_PAD)[:, None]
    c = jnp.arange(3 * NSA_WIDTH)[None, :]
    return ((r < GATE_COLS) & (r == (c // NSA_WIDTH) * NSA_HEADS + (c % NSA_WIDTH) // NSA_HEAD_DIM)).astype(F32)


def kernel(x, g_mix, w_in, pe_cmp_k, pe_cmp_v, w_cmp_k1, w_cmp_k2, w_cmp_v1, w_cmp_v2, w_nsa_out, w_dw, b_dw,
           g_conv_ln, b_conv_ln, w_conv_out, w_o, g_ffn, w_peer_q, peer_sub_keys, peer_u, peer_v, g_final):
    b, s, d = x.shape
    t = b * s
    depth = w_in.shape[0]
    assert depth == 1, "the fused final norm assumes a single layer"
    assert s % 512 == 0 and d % LANE == 0
    x2 = x.reshape(t, d)
    wexp = _gate_expand_matrix()
    l = 0
    gate_end = QKV_COLS + GATE_COLS
    w_pad = jnp.concatenate([w_in[l][:, :gate_end], jnp.zeros((d, GATE_PAD - GATE_COLS), F32),
                             w_in[l][:, gate_end:]], axis=1).astype(BF16)
    splits = (QKV_COLS, GATE_PAD, 2 * CONV_CH, 2 * d)
    qkv, gates, glu, merge = _norm_proj(x2, g_mix[l], w_pad, splits, tm=256)

    kcmp, vcmp = _compress(qkv, b, s, pe_cmp_k[l], pe_cmp_v[l], w_cmp_k1[l], w_cmp_k2[l],
                           w_cmp_v1[l], w_cmp_v2[l])
    o_cmp, sel = _cmp_attn(qkv, kcmp, vcmp, b, s, tq=256)
    kaug_slc, kaug_win = _attn_prep(qkv, b, s, ts=512)
    o_slc = _flash(qkv, kaug_slc, sel, b, s, "slc")
    o_win = _flash(qkv, kaug_win, None, b, s, "win")

    x1 = _mix(o_cmp, o_slc, o_win, gates, glu, merge, x2, wexp,
              w_nsa_out[l].astype(BF16), w_dw[l].reshape(CONV_WIDTH, CONV_CH), b_dw[l].reshape(1, -1),
              g_conv_ln[l].reshape(1, -1), b_conv_ln[l].reshape(1, -1),
              w_conv_out[l].astype(BF16), w_o[l].astype(BF16), s, tm=256)

    wq_h = w_peer_q[l].reshape(d, PEER_HEADS, PEER_QDIM).transpose(1, 0, 2).astype(BF16)
    xn, e1, n1, e2, r2 = _peer_stats(x1, g_ffn[l], wq_h, peer_sub_keys[l].astype(BF16), tm=256)
    yt = _peer_dense(xn, peer_u[l].astype(BF16), peer_v[l].T.astype(BF16), e1, n1, e2, r2,
                     tm=512, eb=1024)
    out = _final_norm(x1, yt, g_final, tm=512)
    return out.reshape(b, s, d)
```

```python
import functools
import math

import jax
import jax.numpy as jnp
from jax import lax
from jax.experimental import pallas as pl
from jax.experimental.pallas import tpu as pltpu

F32 = jnp.float32
BF16 = jnp.bfloat16

NSA_HEADS = 8
NSA_KV_GROUPS = 2
NSA_HPG = NSA_HEADS // NSA_KV_GROUPS
NSA_HEAD_DIM = 64
NSA_WIDTH = NSA_HEADS * NSA_HEAD_DIM
NSA_KV_WIDTH = NSA_KV_GROUPS * NSA_HEAD_DIM
CMP_BLOCK = 32
CMP_STRIDE = 16
SLC_BLOCK = 64
SLC_TOPN = 16
WINDOW = 512
CONV_CH = 512
CONV_WIDTH = 31
PEER_HEADS = 8
PEER_NKEYS = 128
PEER_QDIM = 256
PEER_HALF = PEER_QDIM // 2
PEER_TOPK = 16
EPS = 1e-6
NEG = -1e30
FORCED = 1e9
SLOPES = tuple(2.0 ** (-8.0 * (h + 1) / NSA_HEADS) for h in range(NSA_HEADS))

LANE = 128
GATE_COLS = 3 * NSA_HEADS
GATE_PAD = LANE
QKV_COLS = NSA_WIDTH + 6 * NSA_KV_WIDTH
VMEM_LIMIT = 56 * 1024 * 1024

_NT = (((1,), (1,)), ((), ()))
_TN = (((0,), (0,)), ((), ()))


def _cparams(*sem):
    return pltpu.CompilerParams(dimension_semantics=sem, vmem_limit_bytes=VMEM_LIMIT)


def _sigmoid(x):
    return 1.0 / (1.0 + jnp.exp(-x))


def _gelu(x):
    return 0.5 * x * (1.0 + lax.erf(x * (1.0 / math.sqrt(2.0))))


def _norm_proj_kernel(x_ref, g_ref, w_ref, *out_refs, splits):
    x = x_ref[...]
    xn = x * lax.rsqrt(jnp.mean(x * x, axis=-1, keepdims=True) + EPS) * g_ref[...]
    xb = xn.astype(BF16)
    off = 0
    for o_ref, n in zip(out_refs, splits):
        o_ref[...] = jnp.dot(xb, w_ref[:, off:off + n], preferred_element_type=F32)
        off += n


def _norm_proj(x2, g, w_bf16, splits, tm):
    t, d = x2.shape
    n = w_bf16.shape[1]
    return pl.pallas_call(
        functools.partial(_norm_proj_kernel, splits=splits),
        grid=(t // tm,),
        in_specs=[pl.BlockSpec((tm, d), lambda i: (i, 0)),
                  pl.BlockSpec((1, d), lambda i: (0, 0)),
                  pl.BlockSpec((d, n), lambda i: (0, 0))],
        out_specs=[pl.BlockSpec((tm, s), lambda i: (i, 0)) for s in splits],
        out_shape=[jax.ShapeDtypeStruct((t, s), F32) for s in splits],
        compiler_params=_cparams("parallel"),
        name="norm_in_proj",
    )(x2, g.reshape(1, d), w_bf16)


def _compress_kernel(k_ref, v_ref, pek_ref, pev_ref, wk1_ref, wk2_ref, wv1_ref, wv2_ref,
                     kc_ref, vc_ref, *, n_blk):
    hp = lax.Precision.HIGHEST
    dh = NSA_HEAD_DIM
    per = CMP_BLOCK // CMP_STRIDE
    assert per == 2

    def one(src_ref, pe_ref, w1_ref, w2_ref, dst_ref):
        pe8 = jnp.broadcast_to(pe_ref[...], (8, CMP_BLOCK * dh))
        const = jnp.dot(pe8, w1_ref[...], precision=hp, preferred_element_type=F32)[0:1, :]
        acc = [[jnp.zeros((n_blk, dh), F32) for _ in range(per)] for _ in range(NSA_KV_GROUPS)]
        for lo in range(CMP_STRIDE):
            rows = src_ref[pl.ds(lo, n_blk, stride=CMP_STRIDE), :]
            for g in range(NSA_KV_GROUPS):
                rg = rows[:, g * dh:(g + 1) * dh]
                for hi in range(per):
                    l = hi * CMP_STRIDE + lo
                    acc[g][hi] = acc[g][hi] + jnp.dot(rg, w1_ref[l * dh:(l + 1) * dh, :], precision=hp,
                                                     preferred_element_type=F32)
        for g in range(NSA_KV_GROUPS):
            pre = acc[g][0] + pltpu.roll(acc[g][1], n_blk - 1, 0) + const
            out = jnp.dot(_gelu(pre), w2_ref[...], precision=hp, preferred_element_type=F32)
            rid = lax.broadcasted_iota(jnp.int32, (n_blk, dh), 0)
            dst_ref[0, g] = jnp.where(rid < n_blk - 1, out, 0.0)

    one(k_ref, pek_ref, wk1_ref, wk2_ref, kc_ref)
    one(v_ref, pev_ref, wv1_ref, wv2_ref, vc_ref)


def _compress(qkv, b, s, pe_k, pe_v, wk1, wk2, wv1, wv2):
    n_blk = s // CMP_STRIDE
    dh = NSA_HEAD_DIM
    kcol = NSA_WIDTH // LANE
    full = lambda shape: pl.BlockSpec(shape, lambda i: tuple(0 for _ in shape))
    out_spec = pl.BlockSpec((1, NSA_KV_GROUPS, n_blk, dh), lambda i: (i, 0, 0, 0))
    out_shape = jax.ShapeDtypeStruct((b, NSA_KV_GROUPS, n_blk, dh), F32)
    return pl.pallas_call(
        functools.partial(_compress_kernel, n_blk=n_blk),
        grid=(b,),
        in_specs=[pl.BlockSpec((s, LANE), lambda i: (i, kcol)),
                  pl.BlockSpec((s, LANE), lambda i: (i, kcol + 1)),
                  full((1, CMP_BLOCK * dh)), full((1, CMP_BLOCK * dh)),
                  full((CMP_BLOCK * dh, dh)), full((dh, dh)),
                  full((CMP_BLOCK * dh, dh)), full((dh, dh))],
        out_specs=[out_spec, out_spec],
        out_shape=[out_shape, out_shape],
        compiler_params=_cparams("parallel"),
        name="nsa_compress",
    )(qkv, qkv, pe_k.reshape(1, -1), pe_v.reshape(1, -1), wk1, wk2, wv1, wv2)


def _cmp_attn_kernel(q_ref, kc_ref, vc_ref, o_ref, sel_ref, *, tq, n_blk, n_slc):
    hp = lax.Precision.HIGHEST
    dh = NSA_HEAD_DIM
    q0 = pl.program_id(1) * tq
    t = q0 + lax.broadcasted_iota(jnp.int32, (tq, n_blk), 0)
    end = lax.broadcasted_iota(jnp.int32, (tq, n_blk), 1) * CMP_STRIDE + (CMP_BLOCK - 1)
    dist = t - end
    mask = dist >= 0
    distf = dist.astype(F32)
    nn = lax.broadcasted_iota(jnp.int32, (n_blk, n_slc), 0) * CMP_STRIDE
    jj = lax.broadcasted_iota(jnp.int32, (n_blk, n_slc), 1) * SLC_BLOCK
    overlap = ((nn <= jj + SLC_BLOCK - 1) & (nn + CMP_BLOCK - 1 >= jj)).astype(F32)
    tj = q0 + lax.broadcasted_iota(jnp.int32, (tq, n_slc), 0)
    jb = lax.broadcasted_iota(jnp.int32, (tq, n_slc), 1)
    t_blk = tj // SLC_BLOCK
    future = jb > t_blk
    forced = (jb == 0) | (jb == t_blk) | (jb == t_blk - 1)
    n_sel = min(SLC_TOPN, n_slc)

    for g in range(NSA_KV_GROUPS):
        kc = kc_ref[0, g].astype(BF16)
        vc = vc_ref[0, g].astype(BF16)
        psum = jnp.zeros((tq, n_blk), F32)
        for h in range(NSA_HPG):
            hh = g * NSA_HPG + h
            qh = (q_ref[:, hh * dh:(hh + 1) * dh] * (1.0 / math.sqrt(dh))).astype(BF16)
            s = lax.dot_general(qh, kc, _NT, preferred_element_type=F32)
            s = jnp.where(mask, s - SLOPES[hh] * distf, NEG)
            m = jnp.max(s, axis=-1, keepdims=True)
            e = jnp.where(mask, jnp.exp(s - m), 0.0)
            l = jnp.sum(e, axis=-1, keepdims=True)
            p = e / jnp.where(l > 0.0, l, 1.0)
            psum = psum + p
            o_ref[:, hh * dh:(hh + 1) * dh] = jnp.dot(p.astype(BF16), vc, preferred_element_type=F32)
        imp = jnp.dot(psum, overlap, precision=hp, preferred_element_type=F32)
        imp = jnp.where(forced, FORCED, jnp.where(future, NEG, imp))
        cnt = jnp.zeros((tq, n_slc), F32)
        for j2 in range(n_slc):
            col = imp[:, j2:j2 + 1]
            ge = jnp.where(col >= imp, 1.0, 0.0)
            gt = jnp.where(col > imp, 1.0, 0.0)
            cnt = cnt + jnp.where(jb > j2, ge, gt)
        sel_ref[g] = (cnt < float(n_sel)).astype(F32)


def _cmp_attn(qkv, kcmp, vcmp, b, s, tq):
    t = b * s
    nq = s // tq
    n_blk = kcmp.shape[2]
    n_slc = s // SLC_BLOCK
    cmp_spec = pl.BlockSpec((1, NSA_KV_GROUPS, n_blk, NSA_HEAD_DIM), lambda bi, i: (bi, 0, 0, 0))
    return pl.pallas_call(
        functools.partial(_cmp_attn_kernel, tq=tq, n_blk=n_blk, n_slc=n_slc),
        grid=(b, nq),
        in_specs=[pl.BlockSpec((tq, NSA_WIDTH), lambda bi, i: (bi * nq + i, 0)), cmp_spec, cmp_spec],
        out_specs=[pl.BlockSpec((tq, NSA_WIDTH), lambda bi, i: (bi * nq + i, 0)),
                   pl.BlockSpec((NSA_KV_GROUPS, tq, n_slc), lambda bi, i: (0, bi * nq + i, 0))],
        out_shape=[jax.ShapeDtypeStruct((t, NSA_WIDTH), F32),
                   jax.ShapeDtypeStruct((NSA_KV_GROUPS, t, n_slc), F32)],
        compiler_params=_cparams("parallel", "parallel"),
        name="nsa_cmp_attn_select",
    )(qkv, kcmp, vcmp)


ATT_TILE = 256
AUG_SLC = 2 * LANE
AUG_WIN = LANE


def _attn_prep_kernel(ks_ref, kw_ref, kas_ref, kaw_ref, *, ts):
    dh = NSA_HEAD_DIM
    s0 = pl.program_id(1) * ts
    pos = s0 + lax.broadcasted_iota(jnp.int32, (ts, dh), 0)
    lane = lax.broadcasted_iota(jnp.int32, (ts, dh), 1)
    onehot = jnp.where(lane == pos // SLC_BLOCK, 1.0, 0.0)
    off = (pos % ATT_TILE).astype(F32)
    cols = jnp.where(lane == 0, 1.0, jnp.where(lane == 1, off, 0.0))
    zeros = jnp.zeros((ts, dh), F32)
    for g in range(NSA_KV_GROUPS):
        ks = ks_ref[:, g * dh:(g + 1) * dh]
        kas_ref[0, g] = jnp.concatenate([ks, onehot, cols, zeros], axis=1).astype(BF16)
        kw = kw_ref[:, g * dh:(g + 1) * dh]
        kaw_ref[0, g] = jnp.concatenate([kw, cols], axis=1).astype(BF16)


def _attn_prep(qkv, b, s, ts):
    kcol = NSA_WIDTH // LANE + 2
    return pl.pallas_call(
        functools.partial(_attn_prep_kernel, ts=ts),
        grid=(b, s // ts),
        in_specs=[pl.BlockSpec((ts, LANE), lambda bi, i: (bi * (s // ts) + i, kcol)),
                  pl.BlockSpec((ts, LANE), lambda bi, i: (bi * (s // ts) + i, kcol + 2))],
        out_specs=[pl.BlockSpec((1, NSA_KV_GROUPS, ts, AUG_SLC), lambda bi, i: (bi, 0, i, 0)),
                   pl.BlockSpec((1, NSA_KV_GROUPS, ts, AUG_WIN), lambda bi, i: (bi, 0, i, 0))],
        out_shape=[jax.ShapeDtypeStruct((b, NSA_KV_GROUPS, s, AUG_SLC), BF16),
                   jax.ShapeDtypeStruct((b, NSA_KV_GROUPS, s, AUG_WIN), BF16)],
        compiler_params=_cparams("parallel", "parallel"),
        name="nsa_key_augment",
    )(qkv, qkv)


def _flash_kernel(*refs, mode, tq):
    if mode == "slc":
        q_ref, ka_ref, v_ref, sel_ref, o_ref, qa_scr, m_scr, l_scr, acc_scr = refs
    else:
        q_ref, ka_ref, v_ref, o_ref, qa_scr, m_scr, l_scr, acc_scr = refs
        sel_ref = None
    dh = NSA_HEAD_DIM
    tk = tq
    rows = NSA_HPG * tq
    i = pl.program_id(1)
    q0 = i * tq
    qid = lax.broadcasted_iota(jnp.int32, (tk, rows), 1) % tq
    kid = lax.broadcasted_iota(jnp.int32, (tk, rows), 0)
    head_of_row = lax.broadcasted_iota(jnp.int32, (1, rows), 1) // tq
    rq = lax.broadcasted_iota(jnp.int32, (tq, dh), 0).astype(F32)
    lq = lax.broadcasted_iota(jnp.int32, (tq, dh), 1)

    for g in range(NSA_KV_GROUPS):
        slopes = [SLOPES[g * NSA_HPG + h] for h in range(NSA_HPG)]
        slope_row = jnp.full((1, rows), slopes[-1], F32)
        for h in range(NSA_HPG - 2, -1, -1):
            slope_row = jnp.where(head_of_row == h, slopes[h], slope_row)
        blocks = []
        for h in range(NSA_HPG):
            hh = g * NSA_HPG + h
            qh = q_ref[:, hh * dh:(hh + 1) * dh] * (1.0 / math.sqrt(dh))
            cols = jnp.where(lq == 0, -slopes[h] * rq, jnp.where(lq == 1, slopes[h], 0.0))
            if mode == "slc":
                n_slc = sel_ref.shape[-1]
                notsel = (sel_ref[g] - 1.0) * (-NEG)
                if n_slc < dh:
                    notsel = jnp.concatenate([notsel, jnp.zeros((tq, dh - n_slc), F32)], axis=1)
                parts = [qh, notsel, cols, jnp.zeros((tq, dh), F32)]
            else:
                parts = [qh, cols]
            blocks.append(jnp.concatenate(parts, axis=1))
        qa_scr[...] = jnp.concatenate(blocks, axis=0).astype(BF16)
        m_scr[...] = jnp.full(m_scr.shape, NEG, F32)
        l_scr[...] = jnp.zeros(l_scr.shape, F32)
        acc_scr[g * dh:(g + 1) * dh, :] = jnp.zeros((dh, rows), F32)

        def tiles(specs, g=g, slope_row=slope_row):
            sts, shifts, vgs = [], [], []
            for kt, mask in specs:
                k0 = pl.multiple_of(kt * tk, tk)
                ka = ka_ref[0, g, pl.ds(k0, tk), :]
                vgs.append(v_ref[pl.ds(k0, tk), g * dh:(g + 1) * dh].astype(BF16))
                st = lax.dot_general(ka, qa_scr[...], _NT, preferred_element_type=F32)
                sts.append(st if mask is None else jnp.where(mask, st, NEG))
                shifts.append(slope_row * jnp.asarray(k0 - q0, F32))
            m_old = m_scr[...]
            m_new = m_old
            for st, shift in zip(sts, shifts):
                m_new = jnp.maximum(m_new, jnp.max(st, axis=0, keepdims=True) + shift)
            alpha = jnp.exp(m_old - m_new)
            l_new = alpha * l_scr[...]
            acc = alpha * acc_scr[g * dh:(g + 1) * dh, :]
            for st, shift, vg in zip(sts, shifts, vgs):
                p = jnp.exp(st - (m_new - shift))
                l_new = l_new + jnp.sum(p, axis=0, keepdims=True)
                acc = acc + lax.dot_general(vg, p.astype(BF16), _TN, preferred_element_type=F32)
            l_scr[...] = l_new
            acc_scr[g * dh:(g + 1) * dh, :] = acc
            m_scr[...] = m_new

        diag = kid <= qid
        if mode == "slc":
            def body(k2, carry, tiles=tiles):
                tiles([(2 * k2, None), (2 * k2 + 1, None)])
                return carry
            lax.fori_loop(0, i // 2, body, 0)

            @pl.when(i % 2 == 1)
            def _(tiles=tiles):
                tiles([(i - 1, None), (i, diag)])

            @pl.when(i % 2 == 0)
            def _(tiles=tiles):
                tiles([(i, diag)])
        else:
            @pl.when(i == 0)
            def _(tiles=tiles):
                tiles([(i, diag)])

            @pl.when(i == 1)
            def _(tiles=tiles):
                tiles([(i, diag), (i - 1, None)])

            @pl.when(i >= 2)
            def _(tiles=tiles):
                tiles([(i, diag), (i - 1, None), (i - 2, kid > qid)])

        acc_scr[g * dh:(g + 1) * dh, :] = acc_scr[g * dh:(g + 1) * dh, :] / l_scr[...]

    out = acc_scr[...].T
    for g in range(NSA_KV_GROUPS):
        for h in range(NSA_HPG):
            hh = g * NSA_HPG + h
            o_ref[:, hh * dh:(hh + 1) * dh] = out[h * tq:(h + 1) * tq, g * dh:(g + 1) * dh]


def _flash(qkv, kaug, sel, b, s, mode):
    tq = ATT_TILE
    assert WINDOW == 2 * tq and s % tq == 0
    t = b * s
    nq = s // tq
    aug = kaug.shape[-1]
    vcol = NSA_WIDTH // LANE + {"slc": 3, "win": 5}[mode]
    in_specs = [pl.BlockSpec((tq, NSA_WIDTH), lambda bi, i: (bi * nq + i, 0)),
                pl.BlockSpec((1, NSA_KV_GROUPS, s, aug), lambda bi, i: (bi, 0, 0, 0)),
                pl.BlockSpec((s, LANE), lambda bi, i: (bi, vcol))]
    args = [qkv, kaug, qkv]
    if mode == "slc":
        assert sel.shape[-1] <= NSA_HEAD_DIM
        in_specs.append(pl.BlockSpec((NSA_KV_GROUPS, tq, sel.shape[-1]), lambda bi, i: (0, bi * nq + i, 0)))
        args.append(sel)
    rows = NSA_HPG * tq
    return pl.pallas_call(
        functools.partial(_flash_kernel, mode=mode, tq=tq),
        grid=(b, nq),
        in_specs=in_specs,
        out_specs=pl.BlockSpec((tq, NSA_WIDTH), lambda bi, i: (bi * nq + i, 0)),
        out_shape=jax.ShapeDtypeStruct((t, NSA_WIDTH), F32),
        scratch_shapes=[pltpu.VMEM((rows, aug), BF16),
                        pltpu.VMEM((1, rows), F32),
                        pltpu.VMEM((1, rows), F32),
                        pltpu.VMEM((NSA_KV_GROUPS * NSA_HEAD_DIM, rows), F32)],
        compiler_params=_cparams("parallel", "parallel"),
        name="nsa_flash_" + mode,
    )(*args)


HALO = 32


def _mix_kernel(ocmp_ref, oslc_ref, owin_ref, gates_ref, glu_ref, halo_ref, merge_ref, x_ref,
                wexp_ref, wnsa_ref, wdw_ref, bdw_ref, gln_ref, bln_ref, wconv_ref, wo_ref,
                o_ref, uext_scr, *, tm, tiles_per_seq):
    hp = lax.Precision.HIGHEST
    i = pl.program_id(0)
    gts = _sigmoid(gates_ref[...])
    gexp = jnp.dot(gts, wexp_ref[...], precision=hp, preferred_element_type=F32)
    w = NSA_WIDTH
    o_nsa = gexp[:, :w] * ocmp_ref[...] + gexp[:, w:2 * w] * oslc_ref[...] + gexp[:, 2 * w:] * owin_ref[...]
    y_a = jnp.dot(o_nsa.astype(BF16), wnsa_ref[...], preferred_element_type=F32)

    c = CONV_CH
    gl = glu_ref[...]
    u = gl[:, :c] * _sigmoid(gl[:, c:])
    hl = halo_ref[...]
    uh = hl[:, :c] * _sigmoid(hl[:, c:])
    uh = jnp.where(i % tiles_per_seq == 0, 0.0, uh)
    uext_scr[0:HALO, :] = uh
    uext_scr[HALO:HALO + tm, :] = u
    acc = jnp.zeros((tm, c), F32)
    for k in range(CONV_WIDTH):
        acc = acc + uext_scr[pl.ds(HALO - (CONV_WIDTH - 1) + k, tm), :] * wdw_ref[k:k + 1, :]
    cv = acc + bdw_ref[...]
    mu = jnp.mean(cv, axis=-1, keepdims=True)
    var = jnp.mean(jnp.square(cv - mu), axis=-1, keepdims=True)
    un = (cv - mu) * lax.rsqrt(var + EPS) * gln_ref[...] + bln_ref[...]
    act = un * _sigmoid(un)
    y_b = jnp.dot(act.astype(BF16), wconv_ref[...], preferred_element_type=F32)

    d = x_ref.shape[-1]
    mg = merge_ref[...]
    z = _sigmoid(mg[:, :d]) * y_a + _sigmoid(mg[:, d:]) * y_b
    o_ref[...] = x_ref[...] + jnp.dot(z.astype(BF16), wo_ref[...], preferred_element_type=F32)


def _mix(ocmp, oslc, owin, gates, glu, merge, x2, wexp, wnsa, wdw, bdw, gln, bln, wconv, wo, s, tm):
    t, d = x2.shape
    row = lambda n: pl.BlockSpec((tm, n), lambda i: (i, 0))
    full = lambda a: pl.BlockSpec(a.shape, lambda i: tuple(0 for _ in a.shape))
    halo_spec = pl.BlockSpec((HALO, glu.shape[1]), lambda i: (jnp.maximum(i * (tm // HALO) - 1, 0), 0))
    weights = [wexp, wnsa, wdw, bdw, gln, bln, wconv, wo]
    return pl.pallas_call(
        functools.partial(_mix_kernel, tm=tm, tiles_per_seq=s // tm),
        grid=(t // tm,),
        in_specs=[row(NSA_WIDTH), row(NSA_WIDTH), row(NSA_WIDTH), row(GATE_PAD), row(glu.shape[1]),
                  halo_spec, row(merge.shape[1]), row(d)] + [full(a) for a in weights],
        out_specs=row(d),
        out_shape=jax.ShapeDtypeStruct((t, d), F32),
        scratch_shapes=[pltpu.VMEM((HALO + tm, CONV_CH), F32)],
        compiler_params=_cparams("parallel"),
        name="mixer_merge",
    )(ocmp, oslc, owin, gates, glu, glu, merge, x2, *weights)


def _extract_topk(jobs, k_top):
    def body(k, carry):
        for cur_ref, rank_ref, val_ref in jobs:
            cur = cur_ref[...]
            n_rows = cur.shape[0]
            rid = lax.broadcasted_iota(jnp.int32, cur.shape, 0)
            v = jnp.max(cur, axis=0, keepdims=True)
            idx = jnp.min(jnp.where(cur == v, rid, n_rows), axis=0, keepdims=True)
            hit = rid == idx
            rank_ref[...] = jnp.where(hit, jnp.asarray(k, F32), rank_ref[...])
            cur_ref[...] = jnp.where(hit, -jnp.inf, cur)
            val_ref[pl.ds(k, 1), :] = v
        return carry

    lax.fori_loop(0, k_top, body, 0)


def _peer_stats_kernel(x_ref, g_ref, wq_ref, keys_ref, xn_ref, e1_ref, n1_ref, e2_ref, r2_ref,
                       c1_scr, c2_scr, r1_scr, r2_scr, v1_scr, v2_scr, cand_scr, csel_scr, cval_scr, *, tm):
    nk = PEER_NKEYS
    kt = PEER_TOPK
    x = x_ref[...]
    xn = (x * lax.rsqrt(jnp.mean(x * x, axis=-1, keepdims=True) + EPS) * g_ref[...]).astype(BF16)
    xn_ref[...] = xn
    big = float(nk)
    n_cand = cand_scr.shape[0]

    def head(h, carry):
        qp = jnp.dot(xn, wq_ref[h], preferred_element_type=F32)
        s1 = lax.dot_general(keys_ref[h, 0], qp[:, :PEER_HALF].astype(BF16), _NT,
                             preferred_element_type=F32)
        s2 = lax.dot_general(keys_ref[h, 1], qp[:, PEER_HALF:].astype(BF16), _NT,
                             preferred_element_type=F32)
        c1_scr[...] = s1
        c2_scr[...] = s2
        r1_scr[...] = jnp.full((nk, tm), big, F32)
        r2_scr[...] = jnp.full((nk, tm), big, F32)
        _extract_topk([(c1_scr, r1_scr, v1_scr), (c2_scr, r2_scr, v2_scr)], kt)
        v1 = v1_scr[...]
        v2 = v2_scr[...]
        pieces = [v1[a:a + 1, :] + v2[0:kt // (a + 1), :] for a in range(kt)]
        pad = n_cand - sum(kt // (a + 1) for a in range(kt))
        pieces.append(jnp.full((pad, tm), -jnp.inf, F32))
        cand = jnp.concatenate(pieces, axis=0)
        cand_scr[...] = cand
        csel_scr[...] = jnp.full((n_cand, tm), big, F32)
        _extract_topk([(cand_scr, csel_scr, cval_scr)], kt)
        sel_f = jnp.where(csel_scr[...] < big, 1.0, 0.0)
        top = v1[0:1, :] + v2[0:1, :]
        z = jnp.sum(sel_f * jnp.exp(cand - top), axis=0, keepdims=True)
        r1 = r1_scr[...]
        n1 = jnp.zeros((nk, tm), F32)
        off = 0
        for a in range(kt):
            cnt = kt // (a + 1)
            n_a = jnp.sum(sel_f[off:off + cnt, :], axis=0, keepdims=True)
            off += cnt
            n1 = jnp.where(r1 == float(a), n_a, n1)
        e1_ref[h] = jnp.exp(s1 - v1[0:1, :])
        n1_ref[h] = n1
        e2_ref[h] = jnp.exp(s2 - v2[0:1, :]) / z
        r2_ref[h] = r2_scr[...]
        return carry

    lax.fori_loop(0, PEER_HEADS, head, 0)


def _peer_stats(x1, g, wq_h, keys_bf16, tm):
    t, d = x1.shape
    nk, kt = PEER_NKEYS, PEER_TOPK
    stat_spec = pl.BlockSpec((PEER_HEADS, nk, tm), lambda i: (0, 0, i))
    stat_shape = jax.ShapeDtypeStruct((PEER_HEADS, nk, t), F32)
    full = lambda a: pl.BlockSpec(a.shape, lambda i: tuple(0 for _ in a.shape))
    n_cells = sum(kt // (a + 1) for a in range(kt))
    n_cand = -(-n_cells // 8) * 8
    return pl.pallas_call(
        functools.partial(_peer_stats_kernel, tm=tm),
        grid=(t // tm,),
        in_specs=[pl.BlockSpec((tm, d), lambda i: (i, 0)), pl.BlockSpec((1, d), lambda i: (0, 0)),
                  full(wq_h), full(keys_bf16)],
        out_specs=[pl.BlockSpec((tm, d), lambda i: (i, 0))] + [stat_spec] * 4,
        out_shape=[jax.ShapeDtypeStruct((t, d), BF16)] + [stat_shape] * 4,
        scratch_shapes=[pltpu.VMEM((nk, tm), F32), pltpu.VMEM((nk, tm), F32),
                        pltpu.VMEM((nk, tm), F32), pltpu.VMEM((nk, tm), F32),
                        pltpu.VMEM((kt, tm), F32), pltpu.VMEM((kt, tm), F32),
                        pltpu.VMEM((n_cand, tm), F32), pltpu.VMEM((n_cand, tm), F32),
                        pltpu.VMEM((kt, tm), F32)],
        compiler_params=_cparams("parallel"),
        name="peer_stats",
    )(x1, g.reshape(1, d), wq_h, keys_bf16)


PEER_KEY_GROUP = 4


PEER_STAGES = 3


def _peer_dense_kernel(xn_ref, u_ref, vt_ref, e1_ref, n1_ref, e2_ref, r2_ref, o_ref, ht_scr, at_scr,
                       *, tm, eb, nb):
    nk = PEER_NKEYS
    s = pl.program_id(0)
    n_keys = eb // nk

    @pl.when(s == 0)
    def _():
        ht_scr[...] = jnp.zeros(ht_scr.shape, F32)
        at_scr[...] = jnp.zeros(at_scr.shape, BF16)

    @pl.when((s < PEER_STAGES) | ((s - (PEER_STAGES - 1)) % nb == 0))
    def _():
        o_ref[...] = jnp.zeros(o_ref.shape, F32)

    def step(slot):
        o_ref[...] += jnp.dot(vt_ref[...], at_scr[slot], preferred_element_type=F32)

        jb = jnp.maximum(s - 1, 0) % nb
        n1_rows = [[n1_ref[h, pl.ds(jb * n_keys + c, 1), :].astype(BF16) for c in range(n_keys)]
                   for h in range(PEER_HEADS)]
        e1_rows = [[e1_ref[h, pl.ds(jb * n_keys + c, 1), :].astype(BF16) for c in range(n_keys)]
                   for h in range(PEER_HEADS)]
        at_scr[1 - slot] = _gelu(ht_scr[1 - slot]).astype(BF16)
        for lt in range(tm // LANE):
            lanes = slice(lt * LANE, (lt + 1) * LANE)
            gates = []
            for c0 in range(0, n_keys, PEER_KEY_GROUP):
                grp = [jnp.zeros((nk, LANE), BF16) for _ in range(PEER_KEY_GROUP)]
                for h in range(PEER_HEADS):
                    r2 = r2_ref[h, :, lanes].astype(BF16)
                    e2 = e2_ref[h, :, lanes].astype(BF16)
                    for c in range(PEER_KEY_GROUP):
                        n1 = n1_rows[h][c0 + c][:, lanes]
                        e1 = e1_rows[h][c0 + c][:, lanes]
                        grp[c] = grp[c] + jnp.where(r2 < n1, e2, jnp.zeros_like(e2)) * e1
                gates.extend(grp)
            gate = jnp.concatenate(gates, axis=0)
            at_scr[1 - slot, :, lanes] = at_scr[1 - slot, :, lanes] * gate

        ht_scr[slot] = lax.dot_general(u_ref[...], xn_ref[...], _NT, preferred_element_type=F32)

    step(s % 2)


def _peer_dense(xn, u_bf16, vt_bf16, e1, n1, e2, r2, tm, eb):
    t, d = xn.shape
    n_exp = u_bf16.shape[0]
    nb = n_exp // eb
    n_items = (t // tm) * nb
    item = lambda s, lag: jnp.clip(s - lag, 0, n_items - 1)
    stat_spec = pl.BlockSpec((PEER_HEADS, PEER_NKEYS, tm), lambda s: (0, 0, item(s, 1) // nb),
                             pipeline_mode=pl.Buffered(1))
    return pl.pallas_call(
        functools.partial(_peer_dense_kernel, tm=tm, eb=eb, nb=nb),
        grid=(n_items + PEER_STAGES - 1,),
        in_specs=[pl.BlockSpec((tm, d), lambda s: (item(s, 0) // nb, 0)),
                  pl.BlockSpec((eb, d), lambda s: (item(s, 0) % nb, 0)),
                  pl.BlockSpec((None, d, eb), lambda s: (item(s, 2) % nb, 0, 0)),
                  stat_spec, stat_spec, stat_spec, stat_spec],
        out_specs=pl.BlockSpec((d, tm), lambda s: (0, item(s, 2) // nb)),
        out_shape=jax.ShapeDtypeStruct((d, t), F32),
        scratch_shapes=[pltpu.VMEM((2, eb, tm), F32), pltpu.VMEM((2, eb, tm), BF16)],
        compiler_params=_cparams("arbitrary"),
        name="peer_dense",
    )(xn, u_bf16, vt_bf16, e1, n1, e2, r2)


def _final_kernel(x1_ref, yt_ref, g_ref, o_ref):
    y = x1_ref[...] + yt_ref[...].T
    o_ref[...] = y * lax.rsqrt(jnp.mean(y * y, axis=-1, keepdims=True) + EPS) * g_ref[...]


def _final_norm(x1, yt, g_final, tm):
    t, d = x1.shape
    return pl.pallas_call(
        _final_kernel,
        grid=(t // tm,),
        in_specs=[pl.BlockSpec((tm, d), lambda i: (i, 0)), pl.BlockSpec((d, tm), lambda i: (0, i)),
                  pl.BlockSpec((1, d), lambda i: (0, 0))],
        out_specs=pl.BlockSpec((tm, d), lambda i: (i, 0)),
        out_shape=jax.ShapeDtypeStruct((t, d), F32),
        compiler_params=_cparams("parallel"),
        name="peer_residual_final_norm",
    )(x1, yt, g_final.reshape(1, d))


def _gate_expand_matrix():
    r = jnp.arange(GATE_PAD)[:, None]
    c = jnp.arange(3 * NSA_WIDTH)[None, :]
    return ((r < GATE_COLS) & (r == (c // NSA_WIDTH) * NSA_HEADS + (c % NSA_WIDTH) // NSA_HEAD_DIM)).astype(F32)


def kernel(x, g_mix, w_in, pe_cmp_k, pe_cmp_v, w_cmp_k1, w_cmp_k2, w_cmp_v1, w_cmp_v2, w_nsa_out, w_dw, b_dw,
           g_conv_ln, b_conv_ln, w_conv_out, w_o, g_ffn, w_peer_q, peer_sub_keys, peer_u, peer_v, g_final):
    b, s, d = x.shape
    t = b * s
    depth = w_in.shape[0]
    assert depth == 1, "the fused final norm assumes a single layer"
    assert s % 512 == 0 and d % LANE == 0
    x2 = x.reshape(t, d)
    wexp = _gate_expand_matrix()
    l = 0
    gate_end = QKV_COLS + GATE_COLS
    w_pad = jnp.concatenate([w_in[l][:, :gate_end], jnp.zeros((d, GATE_PAD - GATE_COLS), F32),
                             w_in[l][:, gate_end:]], axis=1).astype(BF16)
    splits = (QKV_COLS, GATE_PAD, 2 * CONV_CH, 2 * d)
    qkv, gates, glu, merge = _norm_proj(x2, g_mix[l], w_pad, splits, tm=256)

    kcmp, vcmp = _compress(qkv, b, s, pe_cmp_k[l], pe_cmp_v[l], w_cmp_k1[l], w_cmp_k2[l],
                           w_cmp_v1[l], w_cmp_v2[l])
    o_cmp, sel = _cmp_attn(qkv, kcmp, vcmp, b, s, tq=256)
    kaug_slc, kaug_win = _attn_prep(qkv, b, s, ts=512)
    o_slc = _flash(qkv, kaug_slc, sel, b, s, "slc")
    o_win = _flash(qkv, kaug_win, None, b, s, "win")

    x1 = _mix(o_cmp, o_slc, o_win, gates, glu, merge, x2, wexp,
              w_nsa_out[l].astype(BF16), w_dw[l].reshape(CONV_WIDTH, CONV_CH), b_dw[l].reshape(1, -1),
              g_conv_ln[l].reshape(1, -1), b_conv_ln[l].reshape(1, -1),
              w_conv_out[l].astype(BF16), w_o[l].astype(BF16), s, tm=256)

    wq_h = w_peer_q[l].reshape(d, PEER_HEADS, PEER_QDIM).transpose(1, 0, 2).astype(BF16)
    xn, e1, n1, e2, r2 = _peer_stats(x1, g_ffn[l], wq_h, peer_sub_keys[l].astype(BF16), tm=256)
    eb = 1024
    n_exp = peer_v.shape[1]
    vt_blocks = peer_v[l].astype(BF16).reshape(n_exp // eb, eb, d).transpose(0, 2, 1)
    yt = _peer_dense(xn, peer_u[l].astype(BF16), vt_blocks, e1, n1, e2, r2, tm=1024, eb=eb)
    out = _final_norm(x1, yt, g_final, tm=512)
    return out.reshape(b, s, d)
```

```python
import functools
import math

import jax
import jax.numpy as jnp
from jax import lax
from jax.experimental import pallas as pl
from jax.experimental.pallas import tpu as pltpu

F32 = jnp.float32
BF16 = jnp.bfloat16

NSA_HEADS = 8
NSA_KV_GROUPS = 2
NSA_HPG = NSA_HEADS // NSA_KV_GROUPS
NSA_HEAD_DIM = 64
NSA_WIDTH = NSA_HEADS * NSA_HEAD_DIM
NSA_KV_WIDTH = NSA_KV_GROUPS * NSA_HEAD_DIM
CMP_BLOCK = 32
CMP_STRIDE = 16
SLC_BLOCK = 64
SLC_TOPN = 16
WINDOW = 512
CONV_CH = 512
CONV_WIDTH = 31
PEER_HEADS = 8
PEER_NKEYS = 128
PEER_QDIM = 256
PEER_HALF = PEER_QDIM // 2
PEER_TOPK = 16
EPS = 1e-6
NEG = -1e30
FORCED = 1e9
SLOPES = tuple(2.0 ** (-8.0 * (h + 1) / NSA_HEADS) for h in range(NSA_HEADS))

LANE = 128
GATE_COLS = 3 * NSA_HEADS
GATE_PAD = LANE
QKV_COLS = NSA_WIDTH + 6 * NSA_KV_WIDTH
VMEM_LIMIT = 56 * 1024 * 1024

_NT = (((1,), (1,)), ((), ()))
_TN = (((0,), (0,)), ((), ()))


def _cparams(*sem):
    return pltpu.CompilerParams(dimension_semantics=sem, vmem_limit_bytes=VMEM_LIMIT)


def _sigmoid(x):
    return 1.0 / (1.0 + jnp.exp(-x))


def _gelu(x):
    return 0.5 * x * (1.0 + lax.erf(x * (1.0 / math.sqrt(2.0))))


def _norm_proj_kernel(x_ref, g_ref, w_ref, *out_refs, splits):
    x = x_ref[...]
    xn = x * lax.rsqrt(jnp.mean(x * x, axis=-1, keepdims=True) + EPS) * g_ref[...]
    xb = xn.astype(BF16)
    off = 0
    for o_ref, n in zip(out_refs, splits):
        o_ref[...] = jnp.dot(xb, w_ref[:, off:off + n], preferred_element_type=F32)
        off += n


def _norm_proj(x2, g, w_bf16, splits, tm):
    t, d = x2.shape
    n = w_bf16.shape[1]
    return pl.pallas_call(
        functools.partial(_norm_proj_kernel, splits=splits),
        grid=(t // tm,),
        in_specs=[pl.BlockSpec((tm, d), lambda i: (i, 0)),
                  pl.BlockSpec((1, d), lambda i: (0, 0)),
                  pl.BlockSpec((d, n), lambda i: (0, 0))],
        out_specs=[pl.BlockSpec((tm, s), lambda i: (i, 0)) for s in splits],
        out_shape=[jax.ShapeDtypeStruct((t, s), F32) for s in splits],
        compiler_params=_cparams("parallel"),
        name="norm_in_proj",
    )(x2, g.reshape(1, d), w_bf16)


def _compress_kernel(k_ref, v_ref, pek_ref, pev_ref, wk1_ref, wk2_ref, wv1_ref, wv2_ref,
                     kc_ref, vc_ref, *, n_blk):
    hp = lax.Precision.HIGHEST
    dh = NSA_HEAD_DIM
    per = CMP_BLOCK // CMP_STRIDE
    assert per == 2

    def one(src_ref, pe_ref, w1_ref, w2_ref, dst_ref):
        pe8 = jnp.broadcast_to(pe_ref[...], (8, CMP_BLOCK * dh))
        const = jnp.dot(pe8, w1_ref[...], precision=hp, preferred_element_type=F32)[0:1, :]
        acc = [[jnp.zeros((n_blk, dh), F32) for _ in range(per)] for _ in range(NSA_KV_GROUPS)]
        for lo in range(CMP_STRIDE):
            rows = src_ref[pl.ds(lo, n_blk, stride=CMP_STRIDE), :]
            for g in range(NSA_KV_GROUPS):
                rg = rows[:, g * dh:(g + 1) * dh]
                for hi in range(per):
                    l = hi * CMP_STRIDE + lo
                    acc[g][hi] = acc[g][hi] + jnp.dot(rg, w1_ref[l * dh:(l + 1) * dh, :], precision=hp,
                                                     preferred_element_type=F32)
        for g in range(NSA_KV_GROUPS):
            pre = acc[g][0] + pltpu.roll(acc[g][1], n_blk - 1, 0) + const
            out = jnp.dot(_gelu(pre), w2_ref[...], precision=hp, preferred_element_type=F32)
            rid = lax.broadcasted_iota(jnp.int32, (n_blk, dh), 0)
            dst_ref[0, g] = jnp.where(rid < n_blk - 1, out, 0.0)

    one(k_ref, pek_ref, wk1_ref, wk2_ref, kc_ref)
    one(v_ref, pev_ref, wv1_ref, wv2_ref, vc_ref)


def _compress(qkv, b, s, pe_k, pe_v, wk1, wk2, wv1, wv2):
    n_blk = s // CMP_STRIDE
    dh = NSA_HEAD_DIM
    kcol = NSA_WIDTH // LANE
    full = lambda shape: pl.BlockSpec(shape, lambda i: tuple(0 for _ in shape))
    out_spec = pl.BlockSpec((1, NSA_KV_GROUPS, n_blk, dh), lambda i: (i, 0, 0, 0))
    out_shape = jax.ShapeDtypeStruct((b, NSA_KV_GROUPS, n_blk, dh), F32)
    return pl.pallas_call(
        functools.partial(_compress_kernel, n_blk=n_blk),
        grid=(b,),
        in_specs=[pl.BlockSpec((s, LANE), lambda i: (i, kcol)),
                  pl.BlockSpec((s, LANE), lambda i: (i, kcol + 1)),
                  full((1, CMP_BLOCK * dh)), full((1, CMP_BLOCK * dh)),
                  full((CMP_BLOCK * dh, dh)), full((dh, dh)),
                  full((CMP_BLOCK * dh, dh)), full((dh, dh))],
        out_specs=[out_spec, out_spec],
        out_shape=[out_shape, out_shape],
        compiler_params=_cparams("parallel"),
        name="nsa_compress",
    )(qkv, qkv, pe_k.reshape(1, -1), pe_v.reshape(1, -1), wk1, wk2, wv1, wv2)


def _cmp_attn_kernel(q_ref, kc_ref, vc_ref, o_ref, sel_ref, ot_scr, *, tq, n_blk, n_slc):
    hp = lax.Precision.HIGHEST
    dh = NSA_HEAD_DIM
    sub = 8
    rows = NSA_HPG * tq
    q0 = pl.program_id(1) * tq
    n_id = lax.broadcasted_iota(jnp.int32, (n_blk, rows), 0)
    t_id = q0 + lax.broadcasted_iota(jnp.int32, (n_blk, rows), 1) % tq
    mask = n_id * CMP_STRIDE + (CMP_BLOCK - 1) <= t_id
    jj = lax.broadcasted_iota(jnp.int32, (n_slc, n_blk), 0) * SLC_BLOCK
    nn = lax.broadcasted_iota(jnp.int32, (n_slc, n_blk), 1) * CMP_STRIDE
    overlap_t = jnp.where((nn <= jj + SLC_BLOCK - 1) & (nn + CMP_BLOCK - 1 >= jj), 1.0, 0.0)
    jb = lax.broadcasted_iota(jnp.int32, (n_slc, tq), 0)
    t_blk = (q0 + lax.broadcasted_iota(jnp.int32, (n_slc, tq), 1)) // SLC_BLOCK
    future = jb > t_blk
    forced = (jb == 0) | (jb == t_blk) | (jb == t_blk - 1)
    n_sel = min(SLC_TOPN, n_slc)
    lk = lax.broadcasted_iota(jnp.int32, (n_blk, dh), 1)
    nk_f = (lax.broadcasted_iota(jnp.int32, (n_blk, dh), 0) * CMP_STRIDE).astype(F32)
    cols_k = jnp.where(lk == 0, nk_f, 0.0)
    lq = lax.broadcasted_iota(jnp.int32, (tq, dh), 1)

    for g in range(NSA_KV_GROUPS):
        kca = jnp.concatenate([kc_ref[0, g], cols_k], axis=1).astype(BF16)
        vc = vc_ref[0, g].astype(BF16)
        blocks = []
        for h in range(NSA_HPG):
            hh = g * NSA_HPG + h
            qh = q_ref[:, hh * dh:(hh + 1) * dh] * (1.0 / math.sqrt(dh))
            blocks.append(jnp.concatenate([qh, jnp.where(lq == 0, SLOPES[hh], 0.0)], axis=1))
        qa = jnp.concatenate(blocks, axis=0).astype(BF16)
        st = lax.dot_general(kca, qa, _NT, preferred_element_type=F32)
        st = jnp.where(mask, st, NEG)
        m = jnp.max(st, axis=0, keepdims=True)
        e = jnp.where(mask, jnp.exp(st - m), 0.0)
        l = jnp.sum(e, axis=0, keepdims=True)
        p = e / jnp.where(l > 0.0, l, 1.0)
        ot_scr[g * dh:(g + 1) * dh, :] = lax.dot_general(vc, p.astype(BF16), _TN, preferred_element_type=F32)
        psum = p[:, 0:tq]
        for h in range(1, NSA_HPG):
            psum = psum + p[:, h * tq:(h + 1) * tq]
        imp = jnp.dot(overlap_t, psum, precision=hp, preferred_element_type=F32)
        imp = jnp.where(forced, FORCED, jnp.where(future, NEG, imp))
        slabs = [imp[v * sub:(v + 1) * sub, :] for v in range(n_slc // sub)]
        cnts = [jnp.zeros((sub, tq), F32) for _ in slabs]
        jrow = lax.broadcasted_iota(jnp.int32, (sub, tq), 0)
        for j2 in range(n_slc):
            row = imp[j2:j2 + 1, :]
            for v, slab in enumerate(slabs):
                if v * sub > j2:
                    before = row >= slab
                elif v * sub + sub - 1 <= j2:
                    before = row > slab
                else:
                    before = jnp.where(jrow + v * sub > j2, jnp.where(row >= slab, 1.0, 0.0),
                                       jnp.where(row > slab, 1.0, 0.0)) > 0.5
                cnts[v] = cnts[v] + jnp.where(before, 1.0, 0.0)
        sel_t = jnp.where(jnp.concatenate(cnts, axis=0) < float(n_sel), 1.0, 0.0)
        if n_slc < LANE:
            sel_t = jnp.concatenate([sel_t, jnp.zeros((LANE - n_slc, tq), F32)], axis=0)
        sel_ref[g] = sel_t.T[:, :n_slc]

    out = ot_scr[...].T
    for g in range(NSA_KV_GROUPS):
        for h in range(NSA_HPG):
            hh = g * NSA_HPG + h
            o_ref[:, hh * dh:(hh + 1) * dh] = out[h * tq:(h + 1) * tq, g * dh:(g + 1) * dh]


def _cmp_attn(qkv, kcmp, vcmp, b, s, tq):
    t = b * s
    nq = s // tq
    n_blk = kcmp.shape[2]
    n_slc = s // SLC_BLOCK
    cmp_spec = pl.BlockSpec((1, NSA_KV_GROUPS, n_blk, NSA_HEAD_DIM), lambda bi, i: (bi, 0, 0, 0))
    return pl.pallas_call(
        functools.partial(_cmp_attn_kernel, tq=tq, n_blk=n_blk, n_slc=n_slc),
        grid=(b, nq),
        in_specs=[pl.BlockSpec((tq, NSA_WIDTH), lambda bi, i: (bi * nq + i, 0)), cmp_spec, cmp_spec],
        out_specs=[pl.BlockSpec((tq, NSA_WIDTH), lambda bi, i: (bi * nq + i, 0)),
                   pl.BlockSpec((NSA_KV_GROUPS, tq, n_slc), lambda bi, i: (0, bi * nq + i, 0))],
        out_shape=[jax.ShapeDtypeStruct((t, NSA_WIDTH), F32),
                   jax.ShapeDtypeStruct((NSA_KV_GROUPS, t, n_slc), F32)],
        scratch_shapes=[pltpu.VMEM((NSA_KV_GROUPS * NSA_HEAD_DIM, NSA_HPG * tq), F32)],
        compiler_params=_cparams("parallel", "parallel"),
        name="nsa_cmp_attn_select",
    )(qkv, kcmp, vcmp)


ATT_TILE = 256
AUG_SLC = 2 * LANE
AUG_WIN = LANE


def _attn_prep_kernel(ks_ref, kw_ref, kas_ref, kaw_ref, *, ts):
    dh = NSA_HEAD_DIM
    s0 = pl.program_id(1) * ts
    pos = s0 + lax.broadcasted_iota(jnp.int32, (ts, dh), 0)
    lane = lax.broadcasted_iota(jnp.int32, (ts, dh), 1)
    onehot = jnp.where(lane == pos // SLC_BLOCK, 1.0, 0.0)
    off = (pos % ATT_TILE).astype(F32)
    cols = jnp.where(lane == 0, 1.0, jnp.where(lane == 1, off, 0.0))
    zeros = jnp.zeros((ts, dh), F32)
    for g in range(NSA_KV_GROUPS):
        ks = ks_ref[:, g * dh:(g + 1) * dh]
        kas_ref[0, g] = jnp.concatenate([ks, onehot, cols, zeros], axis=1).astype(BF16)
        kw = kw_ref[:, g * dh:(g + 1) * dh]
        kaw_ref[0, g] = jnp.concatenate([kw, cols], axis=1).astype(BF16)


def _attn_prep(qkv, b, s, ts):
    kcol = NSA_WIDTH // LANE + 2
    return pl.pallas_call(
        functools.partial(_attn_prep_kernel, ts=ts),
        grid=(b, s // ts),
        in_specs=[pl.BlockSpec((ts, LANE), lambda bi, i: (bi * (s // ts) + i, kcol)),
                  pl.BlockSpec((ts, LANE), lambda bi, i: (bi * (s // ts) + i, kcol + 2))],
        out_specs=[pl.BlockSpec((1, NSA_KV_GROUPS, ts, AUG_SLC), lambda bi, i: (bi, 0, i, 0)),
                   pl.BlockSpec((1, NSA_KV_GROUPS, ts, AUG_WIN), lambda bi, i: (bi, 0, i, 0))],
        out_shape=[jax.ShapeDtypeStruct((b, NSA_KV_GROUPS, s, AUG_SLC), BF16),
                   jax.ShapeDtypeStruct((b, NSA_KV_GROUPS, s, AUG_WIN), BF16)],
        compiler_params=_cparams("parallel", "parallel"),
        name="nsa_key_augment",
    )(qkv, qkv)


def _flash_kernel(*refs, mode, tq):
    if mode == "slc":
        q_ref, ka_ref, v_ref, sel_ref, o_ref, qa_scr, m_scr, l_scr, acc_scr = refs
    else:
        q_ref, ka_ref, v_ref, o_ref, qa_scr, m_scr, l_scr, acc_scr = refs
        sel_ref = None
    dh = NSA_HEAD_DIM
    tk = tq
    rows = NSA_HPG * tq
    i = pl.program_id(1)
    q0 = i * tq
    qid = lax.broadcasted_iota(jnp.int32, (tk, rows), 1) % tq
    kid = lax.broadcasted_iota(jnp.int32, (tk, rows), 0)
    head_of_row = lax.broadcasted_iota(jnp.int32, (1, rows), 1) // tq
    rq = lax.broadcasted_iota(jnp.int32, (tq, dh), 0).astype(F32)
    lq = lax.broadcasted_iota(jnp.int32, (tq, dh), 1)

    for g in range(NSA_KV_GROUPS):
        slopes = [SLOPES[g * NSA_HPG + h] for h in range(NSA_HPG)]
        slope_row = jnp.full((1, rows), slopes[-1], F32)
        for h in range(NSA_HPG - 2, -1, -1):
            slope_row = jnp.where(head_of_row == h, slopes[h], slope_row)
        blocks = []
        for h in range(NSA_HPG):
            hh = g * NSA_HPG + h
            qh = q_ref[:, hh * dh:(hh + 1) * dh] * (1.0 / math.sqrt(dh))
            cols = jnp.where(lq == 0, -slopes[h] * rq, jnp.where(lq == 1, slopes[h], 0.0))
            if mode == "slc":
                n_slc = sel_ref.shape[-1]
                notsel = (sel_ref[g] - 1.0) * (-NEG)
                if n_slc < dh:
                    notsel = jnp.concatenate([notsel, jnp.zeros((tq, dh - n_slc), F32)], axis=1)
                parts = [qh, notsel, cols, jnp.zeros((tq, dh), F32)]
            else:
                parts = [qh, cols]
            blocks.append(jnp.concatenate(parts, axis=1))
        qa_scr[...] = jnp.concatenate(blocks, axis=0).astype(BF16)
        m_scr[...] = jnp.full(m_scr.shape, NEG, F32)
        l_scr[...] = jnp.zeros(l_scr.shape, F32)
        acc_scr[g * dh:(g + 1) * dh, :] = jnp.zeros((dh, rows), F32)

        def tiles(specs, g=g, slope_row=slope_row):
            sts, shifts, vgs = [], [], []
            for kt, mask in specs:
                k0 = pl.multiple_of(kt * tk, tk)
                ka = ka_ref[0, g, pl.ds(k0, tk), :]
                vgs.append(v_ref[pl.ds(k0, tk), g * dh:(g + 1) * dh].astype(BF16))
                st = lax.dot_general(ka, qa_scr[...], _NT, preferred_element_type=F32)
                sts.append(st if mask is None else jnp.where(mask, st, NEG))
                shifts.append(slope_row * jnp.asarray(k0 - q0, F32))
            m_old = m_scr[...]
            m_new = m_old
            for st, shift in zip(sts, shifts):
                m_new = jnp.maximum(m_new, jnp.max(st, axis=0, keepdims=True) + shift)
            alpha = jnp.exp(m_old - m_new)
            l_new = alpha * l_scr[...]
            acc = alpha * acc_scr[g * dh:(g + 1) * dh, :]
            for st, shift, vg in zip(sts, shifts, vgs):
                p = jnp.exp(st - (m_new - shift))
                l_new = l_new + jnp.sum(p, axis=0, keepdims=True)
                acc = acc + lax.dot_general(vg, p.astype(BF16), _TN, preferred_element_type=F32)
            l_scr[...] = l_new
            acc_scr[g * dh:(g + 1) * dh, :] = acc
            m_scr[...] = m_new

        diag = kid <= qid
        if mode == "slc":
            def body(k2, carry, tiles=tiles):
                tiles([(2 * k2, None), (2 * k2 + 1, None)])
                return carry
            lax.fori_loop(0, i // 2, body, 0)

            @pl.when(i % 2 == 1)
            def _(tiles=tiles):
                tiles([(i - 1, None), (i, diag)])

            @pl.when(i % 2 == 0)
            def _(tiles=tiles):
                tiles([(i, diag)])
        else:
            @pl.when(i == 0)
            def _(tiles=tiles):
                tiles([(i, diag)])

            @pl.when(i == 1)
            def _(tiles=tiles):
                tiles([(i, diag), (i - 1, None)])

            @pl.when(i >= 2)
            def _(tiles=tiles):
                tiles([(i, diag), (i - 1, None), (i - 2, kid > qid)])

        acc_scr[g * dh:(g + 1) * dh, :] = acc_scr[g * dh:(g + 1) * dh, :] / l_scr[...]

    out = acc_scr[...].T
    for g in range(NSA_KV_GROUPS):
        for h in range(NSA_HPG):
            hh = g * NSA_HPG + h
            o_ref[:, hh * dh:(hh + 1) * dh] = out[h * tq:(h + 1) * tq, g * dh:(g + 1) * dh]


def _flash(qkv, kaug, sel, b, s, mode):
    tq = ATT_TILE
    assert WINDOW == 2 * tq and s % tq == 0
    t = b * s
    nq = s // tq
    aug = kaug.shape[-1]
    vcol = NSA_WIDTH // LANE + {"slc": 3, "win": 5}[mode]
    in_specs = [pl.BlockSpec((tq, NSA_WIDTH), lambda bi, i: (bi * nq + i, 0)),
                pl.BlockSpec((1, NSA_KV_GROUPS, s, aug), lambda bi, i: (bi, 0, 0, 0)),
                pl.BlockSpec((s, LANE), lambda bi, i: (bi, vcol))]
    args = [qkv, kaug, qkv]
    if mode == "slc":
        assert sel.shape[-1] <= NSA_HEAD_DIM
        in_specs.append(pl.BlockSpec((NSA_KV_GROUPS, tq, sel.shape[-1]), lambda bi, i: (0, bi * nq + i, 0)))
        args.append(sel)
    rows = NSA_HPG * tq
    return pl.pallas_call(
        functools.partial(_flash_kernel, mode=mode, tq=tq),
        grid=(b, nq),
        in_specs=in_specs,
        out_specs=pl.BlockSpec((tq, NSA_WIDTH), lambda bi, i: (bi * nq + i, 0)),
        out_shape=jax.ShapeDtypeStruct((t, NSA_WIDTH), F32),
        scratch_shapes=[pltpu.VMEM((rows, aug), BF16),
                        pltpu.VMEM((1, rows), F32),
                        pltpu.VMEM((1, rows), F32),
                        pltpu.VMEM((NSA_KV_GROUPS * NSA_HEAD_DIM, rows), F32)],
        compiler_params=_cparams("parallel", "parallel"),
        name="nsa_flash_" + mode,
    )(*args)


HALO = 32
CONV_SHIFTS = 8


def _mix_kernel(ocmp_ref, oslc_ref, owin_ref, gates_ref, glu_ref, halo_ref, merge_ref, x_ref,
                wexp_ref, wnsa_ref, wdw_ref, bdw_ref, gln_ref, bln_ref, wconv_ref, wo_ref,
                o_ref, uext_scr, *, tm, tiles_per_seq):
    i = pl.program_id(0)
    gts = _sigmoid(gates_ref[...])
    g_hi = gts.astype(BF16)
    r1 = gts - g_hi.astype(F32)
    g_mid = r1.astype(BF16)
    g_lo = (r1 - g_mid.astype(F32)).astype(BF16)
    wexp = wexp_ref[...]
    gexp = (jnp.dot(g_hi, wexp, preferred_element_type=F32) + jnp.dot(g_mid, wexp, preferred_element_type=F32)
            + jnp.dot(g_lo, wexp, preferred_element_type=F32))
    w = NSA_WIDTH
    o_nsa = gexp[:, :w] * ocmp_ref[...] + gexp[:, w:2 * w] * oslc_ref[...] + gexp[:, 2 * w:] * owin_ref[...]
    y_a = jnp.dot(o_nsa.astype(BF16), wnsa_ref[...], preferred_element_type=F32)

    c = CONV_CH
    gl = glu_ref[...]
    u = gl[:, :c] * _sigmoid(gl[:, c:])
    hl = halo_ref[...]
    uh = hl[:, :c] * _sigmoid(hl[:, c:])
    uh = jnp.where(i % tiles_per_seq == 0, 0.0, uh)
    uext_scr[0, 0:HALO, :] = uh
    uext_scr[0, HALO:HALO + tm, :] = u
    span = HALO + tm - CONV_SHIFTS
    for j in range(1, CONV_SHIFTS):
        uext_scr[j, 0:span, :] = uext_scr[0, pl.ds(j, span), :]
    acc = jnp.zeros((tm, c), F32)
    for k in range(CONV_WIDTH):
        first = HALO - (CONV_WIDTH - 1) + k
        acc = acc + uext_scr[first % CONV_SHIFTS, pl.ds(first - first % CONV_SHIFTS, tm), :] * wdw_ref[k:k + 1, :]
    cv = acc + bdw_ref[...]
    mu = jnp.mean(cv, axis=-1, keepdims=True)
    var = jnp.mean(jnp.square(cv - mu), axis=-1, keepdims=True)
    un = (cv - mu) * lax.rsqrt(var + EPS) * gln_ref[...] + bln_ref[...]
    act = un * _sigmoid(un)
    y_b = jnp.dot(act.astype(BF16), wconv_ref[...], preferred_element_type=F32)

    d = x_ref.shape[-1]
    mg = merge_ref[...]
    z = _sigmoid(mg[:, :d]) * y_a + _sigmoid(mg[:, d:]) * y_b
    o_ref[...] = x_ref[...] + jnp.dot(z.astype(BF16), wo_ref[...], preferred_element_type=F32)


def _mix(ocmp, oslc, owin, gates, glu, merge, x2, wexp, wnsa, wdw, bdw, gln, bln, wconv, wo, s, tm):
    t, d = x2.shape
    row = lambda n: pl.BlockSpec((tm, n), lambda i: (i, 0))
    full = lambda a: pl.BlockSpec(a.shape, lambda i: tuple(0 for _ in a.shape))
    halo_spec = pl.BlockSpec((HALO, glu.shape[1]), lambda i: (jnp.maximum(i * (tm // HALO) - 1, 0), 0))
    weights = [wexp, wnsa, wdw, bdw, gln, bln, wconv, wo]
    return pl.pallas_call(
        functools.partial(_mix_kernel, tm=tm, tiles_per_seq=s // tm),
        grid=(t // tm,),
        in_specs=[row(NSA_WIDTH), row(NSA_WIDTH), row(NSA_WIDTH), row(GATE_PAD), row(glu.shape[1]),
                  halo_spec, row(merge.shape[1]), row(d)] + [full(a) for a in weights],
        out_specs=row(d),
        out_shape=jax.ShapeDtypeStruct((t, d), F32),
        scratch_shapes=[pltpu.VMEM((CONV_SHIFTS, HALO + tm, CONV_CH), F32)],
        compiler_params=_cparams("parallel"),
        name="mixer_merge",
    )(ocmp, oslc, owin, gates, glu, glu, merge, x2, *weights)


def _extract_topk(jobs, k_top):
    def body(k, carry):
        for cur_ref, rank_ref, val_ref in jobs:
            cur = cur_ref[...]
            n_rows = cur.shape[0]
            rid = lax.broadcasted_iota(jnp.int32, cur.shape, 0)
            v = jnp.max(cur, axis=0, keepdims=True)
            idx = jnp.min(jnp.where(cur == v, rid, n_rows), axis=0, keepdims=True)
            hit = rid == idx
            rank_ref[...] = jnp.where(hit, jnp.asarray(k, F32), rank_ref[...])
            cur_ref[...] = jnp.where(hit, -jnp.inf, cur)
            val_ref[pl.ds(k, 1), :] = v
        return carry

    lax.fori_loop(0, k_top, body, 0)


def _peer_stats_kernel(x_ref, g_ref, wq_ref, keys_ref, xn_ref, e1_ref, n1_ref, e2_ref, r2_ref,
                       c1_scr, c2_scr, r1_scr, r2_scr, v1_scr, v2_scr, cand_scr, csel_scr, cval_scr, *, tm):
    nk = PEER_NKEYS
    kt = PEER_TOPK
    x = x_ref[...]
    xn = (x * lax.rsqrt(jnp.mean(x * x, axis=-1, keepdims=True) + EPS) * g_ref[...]).astype(BF16)
    xn_ref[...] = xn
    big = float(nk)
    n_cand = cand_scr.shape[0]

    def head(h, carry):
        qp = jnp.dot(xn, wq_ref[h], preferred_element_type=F32)
        s1 = lax.dot_general(keys_ref[h, 0], qp[:, :PEER_HALF].astype(BF16), _NT,
                             preferred_element_type=F32)
        s2 = lax.dot_general(keys_ref[h, 1], qp[:, PEER_HALF:].astype(BF16), _NT,
                             preferred_element_type=F32)
        c1_scr[...] = s1
        c2_scr[...] = s2
        r1_scr[...] = jnp.full((nk, tm), big, F32)
        r2_scr[...] = jnp.full((nk, tm), big, F32)
        _extract_topk([(c1_scr, r1_scr, v1_scr), (c2_scr, r2_scr, v2_scr)], kt)
        v1 = v1_scr[...]
        v2 = v2_scr[...]
        pieces = [v1[a:a + 1, :] + v2[0:kt // (a + 1), :] for a in range(kt)]
        pad = n_cand - sum(kt // (a + 1) for a in range(kt))
        pieces.append(jnp.full((pad, tm), -jnp.inf, F32))
        cand = jnp.concatenate(pieces, axis=0)
        cand_scr[...] = cand
        csel_scr[...] = jnp.full((n_cand, tm), big, F32)
        _extract_topk([(cand_scr, csel_scr, cval_scr)], kt)
        sel_f = jnp.where(csel_scr[...] < big, 1.0, 0.0)
        top = v1[0:1, :] + v2[0:1, :]
        z = jnp.sum(sel_f * jnp.exp(cand - top), axis=0, keepdims=True)
        r1 = r1_scr[...]
        n1 = jnp.zeros((nk, tm), F32)
        off = 0
        for a in range(kt):
            cnt = kt // (a + 1)
            n_a = jnp.sum(sel_f[off:off + cnt, :], axis=0, keepdims=True)
            off += cnt
            n1 = jnp.where(r1 == float(a), n_a, n1)
        e1_ref[h] = jnp.exp(s1 - v1[0:1, :])
        n1_ref[h] = n1
        e2_ref[h] = jnp.exp(s2 - v2[0:1, :]) / z
        r2_ref[h] = r2_scr[...]
        return carry

    lax.fori_loop(0, PEER_HEADS, head, 0)


def _peer_stats(x1, g, wq_h, keys_bf16, tm):
    t, d = x1.shape
    nk, kt = PEER_NKEYS, PEER_TOPK
    stat_spec = pl.BlockSpec((PEER_HEADS, nk, tm), lambda i: (0, 0, i))
    stat_shape = jax.ShapeDtypeStruct((PEER_HEADS, nk, t), F32)
    full = lambda a: pl.BlockSpec(a.shape, lambda i: tuple(0 for _ in a.shape))
    n_cells = sum(kt // (a + 1) for a in range(kt))
    n_cand = -(-n_cells // 8) * 8
    return pl.pallas_call(
        functools.partial(_peer_stats_kernel, tm=tm),
        grid=(t // tm,),
        in_specs=[pl.BlockSpec((tm, d), lambda i: (i, 0)), pl.BlockSpec((1, d), lambda i: (0, 0)),
                  full(wq_h), full(keys_bf16)],
        out_specs=[pl.BlockSpec((tm, d), lambda i: (i, 0))] + [stat_spec] * 4,
        out_shape=[jax.ShapeDtypeStruct((t, d), BF16)] + [stat_shape] * 4,
        scratch_shapes=[pltpu.VMEM((nk, tm), F32), pltpu.VMEM((nk, tm), F32),
                        pltpu.VMEM((nk, tm), F32), pltpu.VMEM((nk, tm), F32),
                        pltpu.VMEM((kt, tm), F32), pltpu.VMEM((kt, tm), F32),
                        pltpu.VMEM((n_cand, tm), F32), pltpu.VMEM((n_cand, tm), F32),
                        pltpu.VMEM((kt, tm), F32)],
        compiler_params=_cparams("parallel"),
        name="peer_stats",
    )(x1, g.reshape(1, d), wq_h, keys_bf16)


PEER_KEY_GROUP = 4


PEER_STAGES = 3


def _peer_dense_kernel(xn_ref, u_ref, vt_ref, e1_ref, n1_ref, e2_ref, r2_ref, o_ref, ht_scr, at_scr,
                       *, tm, eb, nb):
    nk = PEER_NKEYS
    s = pl.program_id(0)
    n_keys = eb // nk

    @pl.when(s == 0)
    def _():
        ht_scr[...] = jnp.zeros(ht_scr.shape, F32)
        at_scr[...] = jnp.zeros(at_scr.shape, BF16)

    @pl.when((s < PEER_STAGES) | ((s - (PEER_STAGES - 1)) % nb == 0))
    def _():
        o_ref[...] = jnp.zeros(o_ref.shape, F32)

    def step(slot):
        o_ref[...] += jnp.dot(vt_ref[...], at_scr[slot], preferred_element_type=F32)

        jb = jnp.maximum(s - 1, 0) % nb
        n1_rows = [[n1_ref[h, pl.ds(jb * n_keys + c, 1), :].astype(BF16) for c in range(n_keys)]
                   for h in range(PEER_HEADS)]
        e1_rows = [[e1_ref[h, pl.ds(jb * n_keys + c, 1), :].astype(BF16) for c in range(n_keys)]
                   for h in range(PEER_HEADS)]
        at_scr[1 - slot] = _gelu(ht_scr[1 - slot]).astype(BF16)
        for lt in range(tm // LANE):
            lanes = slice(lt * LANE, (lt + 1) * LANE)
            gates = []
            for c0 in range(0, n_keys, PEER_KEY_GROUP):
                grp = [jnp.zeros((nk, LANE), BF16) for _ in range(PEER_KEY_GROUP)]
                for h in range(PEER_HEADS):
                    r2 = r2_ref[h, :, lanes].astype(BF16)
                    e2 = e2_ref[h, :, lanes].astype(BF16)
                    for c in range(PEER_KEY_GROUP):
                        n1 = n1_rows[h][c0 + c][:, lanes]
                        e1 = e1_rows[h][c0 + c][:, lanes]
                        grp[c] = grp[c] + jnp.where(r2 < n1, e2, jnp.zeros_like(e2)) * e1
                gates.extend(grp)
            gate = jnp.concatenate(gates, axis=0)
            at_scr[1 - slot, :, lanes] = at_scr[1 - slot, :, lanes] * gate

        ht_scr[slot] = lax.dot_general(u_ref[...], xn_ref[...], _NT, preferred_element_type=F32)

    step(s % 2)


def _peer_dense(xn, u_bf16, vt_bf16, e1, n1, e2, r2, tm, eb):
    t, d = xn.shape
    n_exp = u_bf16.shape[0]
    nb = n_exp // eb
    n_items = (t // tm) * nb
    item = lambda s, lag: jnp.clip(s - lag, 0, n_items - 1)
    stat_spec = pl.BlockSpec((PEER_HEADS, PEER_NKEYS, tm), lambda s: (0, 0, item(s, 1) // nb),
                             pipeline_mode=pl.Buffered(1))
    return pl.pallas_call(
        functools.partial(_peer_dense_kernel, tm=tm, eb=eb, nb=nb),
        grid=(n_items + PEER_STAGES - 1,),
        in_specs=[pl.BlockSpec((tm, d), lambda s: (item(s, 0) // nb, 0)),
                  pl.BlockSpec((eb, d), lambda s: (item(s, 0) % nb, 0)),
                  pl.BlockSpec((None, d, eb), lambda s: (item(s, 2) % nb, 0, 0)),
                  stat_spec, stat_spec, stat_spec, stat_spec],
        out_specs=pl.BlockSpec((d, tm), lambda s: (0, item(s, 2) // nb)),
        out_shape=jax.ShapeDtypeStruct((d, t), F32),
        scratch_shapes=[pltpu.VMEM((2, eb, tm), F32), pltpu.VMEM((2, eb, tm), BF16)],
        compiler_params=_cparams("arbitrary"),
        name="peer_dense",
    )(xn, u_bf16, vt_bf16, e1, n1, e2, r2)


def _final_kernel(x1_ref, yt_ref, g_ref, o_ref):
    y = x1_ref[...] + yt_ref[...].T
    o_ref[...] = y * lax.rsqrt(jnp.mean(y * y, axis=-1, keepdims=True) + EPS) * g_ref[...]


def _final_norm(x1, yt, g_final, tm):
    t, d = x1.shape
    return pl.pallas_call(
        _final_kernel,
        grid=(t // tm,),
        in_specs=[pl.BlockSpec((tm, d), lambda i: (i, 0)), pl.BlockSpec((d, tm), lambda i: (0, i)),
                  pl.BlockSpec((1, d), lambda i: (0, 0))],
        out_specs=pl.BlockSpec((tm, d), lambda i: (i, 0)),
        out_shape=jax.ShapeDtypeStruct((t, d), F32),
        compiler_params=_cparams("parallel"),
        name="peer_residual_final_norm",
    )(x1, yt, g_final.reshape(1, d))


def _gate_expand_matrix():
    r = jnp.arange(GATE_PAD)[:, None]
    c = jnp.arange(3 * NSA_WIDTH)[None, :]
    return ((r < GATE_COLS) & (r == (c // NSA_WIDTH) * NSA_HEADS + (c % NSA_WIDTH) // NSA_HEAD_DIM)).astype(F32)


def kernel(x, g_mix, w_in, pe_cmp_k, pe_cmp_v, w_cmp_k1, w_cmp_k2, w_cmp_v1, w_cmp_v2, w_nsa_out, w_dw, b_dw,
           g_conv_ln, b_conv_ln, w_conv_out, w_o, g_ffn, w_peer_q, peer_sub_keys, peer_u, peer_v, g_final):
    b, s, d = x.shape
    t = b * s
    depth = w_in.shape[0]
    assert depth == 1, "the fused final norm assumes a single layer"
    assert s % 512 == 0 and d % LANE == 0
    x2 = x.reshape(t, d)
    wexp = _gate_expand_matrix().astype(BF16)
    l = 0
    gate_end = QKV_COLS + GATE_COLS
    w_pad = jnp.concatenate([w_in[l][:, :gate_end], jnp.zeros((d, GATE_PAD - GATE_COLS), F32),
                             w_in[l][:, gate_end:]], axis=1).astype(BF16)
    splits = (QKV_COLS, GATE_PAD, 2 * CONV_CH, 2 * d)
    qkv, gates, glu, merge = _norm_proj(x2, g_mix[l], w_pad, splits, tm=256)

    kcmp, vcmp = _compress(qkv, b, s, pe_cmp_k[l], pe_cmp_v[l], w_cmp_k1[l], w_cmp_k2[l],
                           w_cmp_v1[l], w_cmp_v2[l])
    o_cmp, sel = _cmp_attn(qkv, kcmp, vcmp, b, s, tq=256)
    kaug_slc, kaug_win = _attn_prep(qkv, b, s, ts=512)
    o_slc = _flash(qkv, kaug_slc, sel, b, s, "slc")
    o_win = _flash(qkv, kaug_win, None, b, s, "win")

    x1 = _mix(o_cmp, o_slc, o_win, gates, glu, merge, x2, wexp,
              w_nsa_out[l].astype(BF16), w_dw[l].reshape(CONV_WIDTH, CONV_CH), b_dw[l].reshape(1, -1),
              g_conv_ln[l].reshape(1, -1), b_conv_ln[l].reshape(1, -1),
              w_conv_out[l].astype(BF16), w_o[l].astype(BF16), s, tm=256)

    wq_h = w_peer_q[l].reshape(d, PEER_HEADS, PEER_QDIM).transpose(1, 0, 2).astype(BF16)
    xn, e1, n1, e2, r2 = _peer_stats(x1, g_ffn[l], wq_h, peer_sub_keys[l].astype(BF16), tm=256)
    eb = 1024
    n_exp = peer_v.shape[1]
    vt_blocks = peer_v[l].astype(BF16).reshape(n_exp // eb, eb, d).transpose(0, 2, 1)
    yt = _peer_dense(xn, peer_u[l].astype(BF16), vt_blocks, e1, n1, e2, r2, tm=1024, eb=eb)
    out = _final_norm(x1, yt, g_final, tm=512)
    return out.reshape(b, s, d)
```

```python
import functools
import math

import jax
import jax.numpy as jnp
from jax import lax
from jax.experimental import pallas as pl
from jax.experimental.pallas import tpu as pltpu

F32 = jnp.float32
BF16 = jnp.bfloat16

NSA_HEADS = 8
NSA_KV_GROUPS = 2
NSA_HPG = NSA_HEADS // NSA_KV_GROUPS
NSA_HEAD_DIM = 64
NSA_WIDTH = NSA_HEADS * NSA_HEAD_DIM
NSA_KV_WIDTH = NSA_KV_GROUPS * NSA_HEAD_DIM
CMP_BLOCK = 32
CMP_STRIDE = 16
SLC_BLOCK = 64
SLC_TOPN = 16
WINDOW = 512
CONV_CH = 512
CONV_WIDTH = 31
PEER_HEADS = 8
PEER_NKEYS = 128
PEER_QDIM = 256
PEER_HALF = PEER_QDIM // 2
PEER_TOPK = 16
EPS = 1e-6
NEG = -1e30
FORCED = 1e9
SLOPES = tuple(2.0 ** (-8.0 * (h + 1) / NSA_HEADS) for h in range(NSA_HEADS))

LANE = 128
BF16_ROWS = 16
GATE_COLS = 3 * NSA_HEADS
GATE_PAD = LANE
QKV_COLS = NSA_WIDTH + 6 * NSA_KV_WIDTH
VMEM_LIMIT = 56 * 1024 * 1024

_NT = (((1,), (1,)), ((), ()))
_TN = (((0,), (0,)), ((), ()))


def _cparams(*sem):
    return pltpu.CompilerParams(dimension_semantics=sem, vmem_limit_bytes=VMEM_LIMIT)


def _sigmoid(x):
    return 1.0 / (1.0 + jnp.exp(-x))


def _gelu(x):
    return 0.5 * x * (1.0 + lax.erf(x * (1.0 / math.sqrt(2.0))))


def _norm_proj_kernel(x_ref, g_ref, w_ref, *out_refs, splits):
    x = x_ref[...]
    xn = x * lax.rsqrt(jnp.mean(x * x, axis=-1, keepdims=True) + EPS) * g_ref[...]
    xb = xn.astype(BF16)
    off = 0
    for o_ref, n in zip(out_refs, splits):
        o_ref[...] = jnp.dot(xb, w_ref[:, off:off + n], preferred_element_type=F32)
        off += n


def _norm_proj(x2, g, w_bf16, splits, tm):
    t, d = x2.shape
    n = w_bf16.shape[1]
    return pl.pallas_call(
        functools.partial(_norm_proj_kernel, splits=splits),
        grid=(t // tm,),
        in_specs=[pl.BlockSpec((tm, d), lambda i: (i, 0)),
                  pl.BlockSpec((1, d), lambda i: (0, 0)),
                  pl.BlockSpec((d, n), lambda i: (0, 0))],
        out_specs=[pl.BlockSpec((tm, s), lambda i: (i, 0)) for s in splits],
        out_shape=[jax.ShapeDtypeStruct((t, s), F32) for s in splits],
        compiler_params=_cparams("parallel"),
        name="norm_in_proj",
    )(x2, g.reshape(1, d), w_bf16)


def _compress_kernel(k_ref, v_ref, pek_ref, pev_ref, wk1_ref, wk2_ref, wv1_ref, wv2_ref,
                     kc_ref, vc_ref, *, n_blk):
    hp = lax.Precision.HIGHEST
    dh = NSA_HEAD_DIM
    per = CMP_BLOCK // CMP_STRIDE
    assert per == 2

    def one(src_ref, pe_ref, w1_ref, w2_ref, dst_ref):
        pe8 = jnp.broadcast_to(pe_ref[...], (8, CMP_BLOCK * dh))
        const = jnp.dot(pe8, w1_ref[...], precision=hp, preferred_element_type=F32)[0:1, :]
        acc = [[jnp.zeros((n_blk, dh), F32) for _ in range(per)] for _ in range(NSA_KV_GROUPS)]
        for lo in range(CMP_STRIDE):
            rows = src_ref[pl.ds(lo, n_blk, stride=CMP_STRIDE), :]
            for g in range(NSA_KV_GROUPS):
                rg = rows[:, g * dh:(g + 1) * dh]
                for hi in range(per):
                    l = hi * CMP_STRIDE + lo
                    acc[g][hi] = acc[g][hi] + jnp.dot(rg, w1_ref[l * dh:(l + 1) * dh, :], precision=hp,
                                                     preferred_element_type=F32)
        for g in range(NSA_KV_GROUPS):
            pre = acc[g][0] + pltpu.roll(acc[g][1], n_blk - 1, 0) + const
            out = jnp.dot(_gelu(pre), w2_ref[...], precision=hp, preferred_element_type=F32)
            rid = lax.broadcasted_iota(jnp.int32, (n_blk, dh), 0)
            dst_ref[0, g] = jnp.where(rid < n_blk - 1, out, 0.0)

    one(k_ref, pek_ref, wk1_ref, wk2_ref, kc_ref)
    one(v_ref, pev_ref, wv1_ref, wv2_ref, vc_ref)


def _compress(qkv, b, s, pe_k, pe_v, wk1, wk2, wv1, wv2):
    n_blk = s // CMP_STRIDE
    dh = NSA_HEAD_DIM
    kcol = NSA_WIDTH // LANE
    full = lambda shape: pl.BlockSpec(shape, lambda i: tuple(0 for _ in shape))
    out_spec = pl.BlockSpec((1, NSA_KV_GROUPS, n_blk, dh), lambda i: (i, 0, 0, 0))
    out_shape = jax.ShapeDtypeStruct((b, NSA_KV_GROUPS, n_blk, dh), F32)
    return pl.pallas_call(
        functools.partial(_compress_kernel, n_blk=n_blk),
        grid=(b,),
        in_specs=[pl.BlockSpec((s, LANE), lambda i: (i, kcol)),
                  pl.BlockSpec((s, LANE), lambda i: (i, kcol + 1)),
                  full((1, CMP_BLOCK * dh)), full((1, CMP_BLOCK * dh)),
                  full((CMP_BLOCK * dh, dh)), full((dh, dh)),
                  full((CMP_BLOCK * dh, dh)), full((dh, dh))],
        out_specs=[out_spec, out_spec],
        out_shape=[out_shape, out_shape],
        compiler_params=_cparams("parallel"),
        name="nsa_compress",
    )(qkv, qkv, pe_k.reshape(1, -1), pe_v.reshape(1, -1), wk1, wk2, wv1, wv2)


def _cmp_attn_kernel(q_ref, kc_ref, vc_ref, o_ref, sel_ref, ot_scr, *, tq, n_blk, n_slc):
    hp = lax.Precision.HIGHEST
    dh = NSA_HEAD_DIM
    sub = 8
    rows = NSA_HPG * tq
    q0 = pl.program_id(1) * tq
    n_id = lax.broadcasted_iota(jnp.int32, (n_blk, rows), 0)
    t_id = q0 + lax.broadcasted_iota(jnp.int32, (n_blk, rows), 1) % tq
    mask = n_id * CMP_STRIDE + (CMP_BLOCK - 1) <= t_id
    jj = lax.broadcasted_iota(jnp.int32, (n_slc, n_blk), 0) * SLC_BLOCK
    nn = lax.broadcasted_iota(jnp.int32, (n_slc, n_blk), 1) * CMP_STRIDE
    overlap_t = jnp.where((nn <= jj + SLC_BLOCK - 1) & (nn + CMP_BLOCK - 1 >= jj), 1.0, 0.0)
    jb = lax.broadcasted_iota(jnp.int32, (n_slc, tq), 0)
    t_blk = (q0 + lax.broadcasted_iota(jnp.int32, (n_slc, tq), 1)) // SLC_BLOCK
    future = jb > t_blk
    forced = (jb == 0) | (jb == t_blk) | (jb == t_blk - 1)
    n_sel = min(SLC_TOPN, n_slc)
    lk = lax.broadcasted_iota(jnp.int32, (n_blk, dh), 1)
    nk_f = (lax.broadcasted_iota(jnp.int32, (n_blk, dh), 0) * CMP_STRIDE).astype(F32)
    cols_k = jnp.where(lk == 0, nk_f, 0.0)
    lq = lax.broadcasted_iota(jnp.int32, (tq, dh), 1)

    for g in range(NSA_KV_GROUPS):
        kca = jnp.concatenate([kc_ref[0, g], cols_k], axis=1).astype(BF16)
        vc = vc_ref[0, g].astype(BF16)
        blocks = []
        for h in range(NSA_HPG):
            hh = g * NSA_HPG + h
            qh = q_ref[:, hh * dh:(hh + 1) * dh] * (1.0 / math.sqrt(dh))
            blocks.append(jnp.concatenate([qh, jnp.where(lq == 0, SLOPES[hh], 0.0)], axis=1))
        qa = jnp.concatenate(blocks, axis=0).astype(BF16)
        st = lax.dot_general(kca, qa, _NT, preferred_element_type=F32)
        st = jnp.where(mask, st, NEG)
        m = jnp.max(st, axis=0, keepdims=True)
        e = jnp.where(mask, jnp.exp(st - m), 0.0)
        l = jnp.sum(e, axis=0, keepdims=True)
        p = e / jnp.where(l > 0.0, l, 1.0)
        ot_scr[g * dh:(g + 1) * dh, :] = lax.dot_general(vc, p.astype(BF16), _TN, preferred_element_type=F32)
        psum = p[:, 0:tq]
        for h in range(1, NSA_HPG):
            psum = psum + p[:, h * tq:(h + 1) * tq]
        imp = jnp.dot(overlap_t, psum, precision=hp, preferred_element_type=F32)
        imp = jnp.where(forced, FORCED, jnp.where(future, NEG, imp))
        slabs = [imp[v * sub:(v + 1) * sub, :] for v in range(n_slc // sub)]
        cnts = [jnp.zeros((sub, tq), F32) for _ in slabs]
        jrow = lax.broadcasted_iota(jnp.int32, (sub, tq), 0)
        for j2 in range(n_slc):
            row = imp[j2:j2 + 1, :]
            for v, slab in enumerate(slabs):
                if v * sub > j2:
                    before = row >= slab
                elif v * sub + sub - 1 <= j2:
                    before = row > slab
                else:
                    before = jnp.where(jrow + v * sub > j2, jnp.where(row >= slab, 1.0, 0.0),
                                       jnp.where(row > slab, 1.0, 0.0)) > 0.5
                cnts[v] = cnts[v] + jnp.where(before, 1.0, 0.0)
        sel_t = jnp.where(jnp.concatenate(cnts, axis=0) < float(n_sel), 1.0, 0.0)
        if n_slc < LANE:
            sel_t = jnp.concatenate([sel_t, jnp.zeros((LANE - n_slc, tq), F32)], axis=0)
        sel_ref[g] = sel_t.T[:, :n_slc]

    out = ot_scr[...].T
    for g in range(NSA_KV_GROUPS):
        for h in range(NSA_HPG):
            hh = g * NSA_HPG + h
            o_ref[:, hh * dh:(hh + 1) * dh] = out[h * tq:(h + 1) * tq, g * dh:(g + 1) * dh]


def _cmp_attn(qkv, kcmp, vcmp, b, s, tq):
    t = b * s
    nq = s // tq
    n_blk = kcmp.shape[2]
    n_slc = s // SLC_BLOCK
    cmp_spec = pl.BlockSpec((1, NSA_KV_GROUPS, n_blk, NSA_HEAD_DIM), lambda bi, i: (bi, 0, 0, 0))
    return pl.pallas_call(
        functools.partial(_cmp_attn_kernel, tq=tq, n_blk=n_blk, n_slc=n_slc),
        grid=(b, nq),
        in_specs=[pl.BlockSpec((tq, NSA_WIDTH), lambda bi, i: (bi * nq + i, 0)), cmp_spec, cmp_spec],
        out_specs=[pl.BlockSpec((tq, NSA_WIDTH), lambda bi, i: (bi * nq + i, 0)),
                   pl.BlockSpec((NSA_KV_GROUPS, tq, n_slc), lambda bi, i: (0, bi * nq + i, 0))],
        out_shape=[jax.ShapeDtypeStruct((t, NSA_WIDTH), F32),
                   jax.ShapeDtypeStruct((NSA_KV_GROUPS, t, n_slc), F32)],
        scratch_shapes=[pltpu.VMEM((NSA_KV_GROUPS * NSA_HEAD_DIM, NSA_HPG * tq), F32)],
        compiler_params=_cparams("parallel", "parallel"),
        name="nsa_cmp_attn_select",
    )(qkv, kcmp, vcmp)


ATT_TILE = 256
AUG_SLC = 2 * LANE
AUG_WIN = LANE


def _attn_prep_kernel(ks_ref, kw_ref, kas_ref, kaw_ref, *, ts):
    dh = NSA_HEAD_DIM
    s0 = pl.program_id(1) * ts
    pos = s0 + lax.broadcasted_iota(jnp.int32, (ts, dh), 0)
    lane = lax.broadcasted_iota(jnp.int32, (ts, dh), 1)
    onehot = jnp.where(lane == pos // SLC_BLOCK, 1.0, 0.0)
    off = (pos % ATT_TILE).astype(F32)
    cols = jnp.where(lane == 0, 1.0, jnp.where(lane == 1, off, 0.0))
    zeros = jnp.zeros((ts, dh), F32)
    for g in range(NSA_KV_GROUPS):
        ks = ks_ref[:, g * dh:(g + 1) * dh]
        kas_ref[0, g] = jnp.concatenate([ks, onehot, cols, zeros], axis=1).astype(BF16)
        kw = kw_ref[:, g * dh:(g + 1) * dh]
        kaw_ref[0, g] = jnp.concatenate([kw, cols], axis=1).astype(BF16)


def _attn_prep(qkv, b, s, ts):
    kcol = NSA_WIDTH // LANE + 2
    return pl.pallas_call(
        functools.partial(_attn_prep_kernel, ts=ts),
        grid=(b, s // ts),
        in_specs=[pl.BlockSpec((ts, LANE), lambda bi, i: (bi * (s // ts) + i, kcol)),
                  pl.BlockSpec((ts, LANE), lambda bi, i: (bi * (s // ts) + i, kcol + 2))],
        out_specs=[pl.BlockSpec((1, NSA_KV_GROUPS, ts, AUG_SLC), lambda bi, i: (bi, 0, i, 0)),
                   pl.BlockSpec((1, NSA_KV_GROUPS, ts, AUG_WIN), lambda bi, i: (bi, 0, i, 0))],
        out_shape=[jax.ShapeDtypeStruct((b, NSA_KV_GROUPS, s, AUG_SLC), BF16),
                   jax.ShapeDtypeStruct((b, NSA_KV_GROUPS, s, AUG_WIN), BF16)],
        compiler_params=_cparams("parallel", "parallel"),
        name="nsa_key_augment",
    )(qkv, qkv)


def _flash_kernel(*refs, mode, tq):
    if mode == "slc":
        q_ref, ka_ref, v_ref, sel_ref, o_ref, qa_scr, m_scr, l_scr, acc_scr = refs
    else:
        q_ref, ka_ref, v_ref, o_ref, qa_scr, m_scr, l_scr, acc_scr = refs
        sel_ref = None
    dh = NSA_HEAD_DIM
    tk = tq
    rows = NSA_HPG * tq
    i = pl.program_id(1)
    q0 = i * tq
    qid = lax.broadcasted_iota(jnp.int32, (tk, rows), 1) % tq
    kid = lax.broadcasted_iota(jnp.int32, (tk, rows), 0)
    head_of_row = lax.broadcasted_iota(jnp.int32, (1, rows), 1) // tq
    rq = lax.broadcasted_iota(jnp.int32, (tq, dh), 0).astype(F32)
    lq = lax.broadcasted_iota(jnp.int32, (tq, dh), 1)

    slope_rows = []
    for g in range(NSA_KV_GROUPS):
        slopes = [SLOPES[g * NSA_HPG + h] for h in range(NSA_HPG)]
        slope_row = jnp.full((1, rows), slopes[-1], F32)
        for h in range(NSA_HPG - 2, -1, -1):
            slope_row = jnp.where(head_of_row == h, slopes[h], slope_row)
        slope_rows.append(slope_row)
        blocks = []
        for h in range(NSA_HPG):
            hh = g * NSA_HPG + h
            qh = q_ref[:, hh * dh:(hh + 1) * dh] * (1.0 / math.sqrt(dh))
            cols = jnp.where(lq == 0, -slopes[h] * rq, jnp.where(lq == 1, slopes[h], 0.0))
            if mode == "slc":
                n_slc = sel_ref.shape[-1]
                notsel = (sel_ref[g] - 1.0) * (-NEG)
                if n_slc < dh:
                    notsel = jnp.concatenate([notsel, jnp.zeros((tq, dh - n_slc), F32)], axis=1)
                parts = [qh, notsel, cols, jnp.zeros((tq, dh), F32)]
            else:
                parts = [qh, cols]
            blocks.append(jnp.concatenate(parts, axis=1))
        qa_scr[g] = jnp.concatenate(blocks, axis=0).astype(BF16)
    m_scr[...] = jnp.full(m_scr.shape, NEG, F32)
    l_scr[...] = jnp.zeros(l_scr.shape, F32)
    acc_scr[...] = jnp.zeros(acc_scr.shape, F32)

    def tiles(specs):
        for g in range(NSA_KV_GROUPS):
            sts, shifts, vgs = [], [], []
            for kt, mask in specs:
                k0 = pl.multiple_of(kt * tk, tk)
                ka = ka_ref[0, g, pl.ds(k0, tk), :]
                vgs.append(v_ref[pl.ds(k0, tk), g * dh:(g + 1) * dh].astype(BF16))
                st = lax.dot_general(ka, qa_scr[g], _NT, preferred_element_type=F32)
                sts.append(st if mask is None else jnp.where(mask, st, NEG))
                shifts.append(slope_rows[g] * jnp.asarray(k0 - q0, F32))
            m_old = m_scr[g]
            m_new = m_old
            for st, shift in zip(sts, shifts):
                m_new = jnp.maximum(m_new, jnp.max(st, axis=0, keepdims=True) + shift)
            alpha = jnp.exp(m_old - m_new)
            l_new = alpha * l_scr[g]
            acc = alpha * acc_scr[g * dh:(g + 1) * dh, :]
            for st, shift, vg in zip(sts, shifts, vgs):
                p = jnp.exp(st - (m_new - shift))
                l_new = l_new + jnp.sum(p, axis=0, keepdims=True)
                acc = acc + lax.dot_general(vg, p.astype(BF16), _TN, preferred_element_type=F32)
            l_scr[g] = l_new
            acc_scr[g * dh:(g + 1) * dh, :] = acc
            m_scr[g] = m_new

    diag = kid <= qid
    if mode == "slc":
        def body(k2, carry):
            tiles([(2 * k2, None), (2 * k2 + 1, None)])
            return carry
        lax.fori_loop(0, i // 2, body, 0)

        @pl.when(i % 2 == 1)
        def _():
            tiles([(i - 1, None), (i, diag)])

        @pl.when(i % 2 == 0)
        def _():
            tiles([(i, diag)])
    else:
        @pl.when(i == 0)
        def _():
            tiles([(i, diag)])

        @pl.when(i == 1)
        def _():
            tiles([(i, diag), (i - 1, None)])

        @pl.when(i >= 2)
        def _():
            tiles([(i, diag), (i - 1, None), (i - 2, kid > qid)])

    for g in range(NSA_KV_GROUPS):
        acc_scr[g * dh:(g + 1) * dh, :] = acc_scr[g * dh:(g + 1) * dh, :] / l_scr[g]

    out = acc_scr[...].T
    for g in range(NSA_KV_GROUPS):
        for h in range(NSA_HPG):
            hh = g * NSA_HPG + h
            o_ref[:, hh * dh:(hh + 1) * dh] = out[h * tq:(h + 1) * tq, g * dh:(g + 1) * dh]


def _flash(qkv, kaug, sel, b, s, mode):
    tq = ATT_TILE
    assert WINDOW == 2 * tq and s % tq == 0
    t = b * s
    nq = s // tq
    aug = kaug.shape[-1]
    vcol = NSA_WIDTH // LANE + {"slc": 3, "win": 5}[mode]
    in_specs = [pl.BlockSpec((tq, NSA_WIDTH), lambda bi, i: (bi * nq + i, 0)),
                pl.BlockSpec((1, NSA_KV_GROUPS, s, aug), lambda bi, i: (bi, 0, 0, 0)),
                pl.BlockSpec((s, LANE), lambda bi, i: (bi, vcol))]
    args = [qkv, kaug, qkv]
    if mode == "slc":
        assert sel.shape[-1] <= NSA_HEAD_DIM
        in_specs.append(pl.BlockSpec((NSA_KV_GROUPS, tq, sel.shape[-1]), lambda bi, i: (0, bi * nq + i, 0)))
        args.append(sel)
    rows = NSA_HPG * tq
    return pl.pallas_call(
        functools.partial(_flash_kernel, mode=mode, tq=tq),
        grid=(b, nq),
        in_specs=in_specs,
        out_specs=pl.BlockSpec((tq, NSA_WIDTH), lambda bi, i: (bi * nq + i, 0)),
        out_shape=jax.ShapeDtypeStruct((t, NSA_WIDTH), F32),
        scratch_shapes=[pltpu.VMEM((NSA_KV_GROUPS, rows, aug), BF16),
                        pltpu.VMEM((NSA_KV_GROUPS, 1, rows), F32),
                        pltpu.VMEM((NSA_KV_GROUPS, 1, rows), F32),
                        pltpu.VMEM((NSA_KV_GROUPS * NSA_HEAD_DIM, rows), F32)],
        compiler_params=_cparams("parallel", "parallel"),
        name="nsa_flash_" + mode,
    )(*args)


HALO = 32
CONV_SHIFTS = 8


def _mix_kernel(ocmp_ref, oslc_ref, owin_ref, gates_ref, glu_ref, halo_ref, merge_ref, x_ref,
                wexp_ref, wnsa_ref, wdw_ref, bdw_ref, gln_ref, bln_ref, wconv_ref, wo_ref,
                o_ref, uext_scr, *, tm, tiles_per_seq):
    i = pl.program_id(0)
    gts = _sigmoid(gates_ref[...])
    g_hi = gts.astype(BF16)
    r1 = gts - g_hi.astype(F32)
    g_mid = r1.astype(BF16)
    g_lo = (r1 - g_mid.astype(F32)).astype(BF16)
    wexp = wexp_ref[...]
    gexp = (jnp.dot(g_hi, wexp, preferred_element_type=F32) + jnp.dot(g_mid, wexp, preferred_element_type=F32)
            + jnp.dot(g_lo, wexp, preferred_element_type=F32))
    w = NSA_WIDTH
    o_nsa = gexp[:, :w] * ocmp_ref[...] + gexp[:, w:2 * w] * oslc_ref[...] + gexp[:, 2 * w:] * owin_ref[...]
    y_a = jnp.dot(o_nsa.astype(BF16), wnsa_ref[...], preferred_element_type=F32)

    c = CONV_CH
    gl = glu_ref[...]
    u = gl[:, :c] * _sigmoid(gl[:, c:])
    hl = halo_ref[...]
    uh = hl[:, :c] * _sigmoid(hl[:, c:])
    uh = jnp.where(i % tiles_per_seq == 0, 0.0, uh)
    uext_scr[0, 0:HALO, :] = uh
    uext_scr[0, HALO:HALO + tm, :] = u
    span = HALO + tm - CONV_SHIFTS
    for j in range(1, CONV_SHIFTS):
        uext_scr[j, 0:span, :] = uext_scr[0, pl.ds(j, span), :]
    acc = jnp.zeros((tm, c), F32)
    for k in range(CONV_WIDTH):
        first = HALO - (CONV_WIDTH - 1) + k
        acc = acc + uext_scr[first % CONV_SHIFTS, pl.ds(first - first % CONV_SHIFTS, tm), :] * wdw_ref[k:k + 1, :]
    cv = acc + bdw_ref[...]
    mu = jnp.mean(cv, axis=-1, keepdims=True)
    var = jnp.mean(jnp.square(cv - mu), axis=-1, keepdims=True)
    un = (cv - mu) * lax.rsqrt(var + EPS) * gln_ref[...] + bln_ref[...]
    act = un * _sigmoid(un)
    y_b = jnp.dot(act.astype(BF16), wconv_ref[...], preferred_element_type=F32)

    d = x_ref.shape[-1]
    mg = merge_ref[...]
    z = _sigmoid(mg[:, :d]) * y_a + _sigmoid(mg[:, d:]) * y_b
    o_ref[...] = x_ref[...] + jnp.dot(z.astype(BF16), wo_ref[...], preferred_element_type=F32)


def _mix(ocmp, oslc, owin, gates, glu, merge, x2, wexp, wnsa, wdw, bdw, gln, bln, wconv, wo, s, tm):
    t, d = x2.shape
    row = lambda n: pl.BlockSpec((tm, n), lambda i: (i, 0))
    full = lambda a: pl.BlockSpec(a.shape, lambda i: tuple(0 for _ in a.shape))
    halo_spec = pl.BlockSpec((HALO, glu.shape[1]), lambda i: (jnp.maximum(i * (tm // HALO) - 1, 0), 0))
    weights = [wexp, wnsa, wdw, bdw, gln, bln, wconv, wo]
    return pl.pallas_call(
        functools.partial(_mix_kernel, tm=tm, tiles_per_seq=s // tm),
        grid=(t // tm,),
        in_specs=[row(NSA_WIDTH), row(NSA_WIDTH), row(NSA_WIDTH), row(GATE_PAD), row(glu.shape[1]),
                  halo_spec, row(merge.shape[1]), row(d)] + [full(a) for a in weights],
        out_specs=row(d),
        out_shape=jax.ShapeDtypeStruct((t, d), F32),
        scratch_shapes=[pltpu.VMEM((CONV_SHIFTS, HALO + tm, CONV_CH), F32)],
        compiler_params=_cparams("parallel"),
        name="mixer_merge",
    )(ocmp, oslc, owin, gates, glu, glu, merge, x2, *weights)


def _extract_topk(jobs, k_top):
    def body(k, carry):
        for cur_ref, rank_ref, val_ref in jobs:
            cur = cur_ref[...]
            n_rows = cur.shape[0]
            rid = lax.broadcasted_iota(jnp.int32, cur.shape, 0)
            v = jnp.max(cur, axis=0, keepdims=True)
            idx = jnp.min(jnp.where(cur == v, rid, n_rows), axis=0, keepdims=True)
            hit = rid == idx
            rank_ref[...] = jnp.where(hit, jnp.asarray(k, F32), rank_ref[...])
            cur_ref[...] = jnp.where(hit, -jnp.inf, cur)
            val_ref[pl.ds(k, 1), :] = v
        return carry

    lax.fori_loop(0, k_top, body, 0)


def _peer_stats_kernel(x_ref, g_ref, wq_ref, keys_ref, xn_ref, e1_ref, n1_ref, e2_ref, r2_ref,
                       c1_scr, c2_scr, r1_scr, r2_scr, v1_scr, v2_scr, cand_scr, csel_scr, cval_scr, *, tm):
    nk = PEER_NKEYS
    kt = PEER_TOPK
    x = x_ref[...]
    xn = (x * lax.rsqrt(jnp.mean(x * x, axis=-1, keepdims=True) + EPS) * g_ref[...]).astype(BF16)
    xn_ref[...] = xn
    big = float(nk)
    n_cand = cand_scr.shape[0]

    def head(h, carry):
        qp = jnp.dot(xn, wq_ref[h], preferred_element_type=F32)
        s1 = lax.dot_general(keys_ref[h, 0], qp[:, :PEER_HALF].astype(BF16), _NT,
                             preferred_element_type=F32)
        s2 = lax.dot_general(keys_ref[h, 1], qp[:, PEER_HALF:].astype(BF16), _NT,
                             preferred_element_type=F32)
        c1_scr[...] = s1
        c2_scr[...] = s2
        r1_scr[...] = jnp.full((nk, tm), big, F32)
        r2_scr[...] = jnp.full((nk, tm), big, F32)
        _extract_topk([(c1_scr, r1_scr, v1_scr), (c2_scr, r2_scr, v2_scr)], kt)
        v1 = v1_scr[...]
        v2 = v2_scr[...]
        pieces = [v1[a:a + 1, :] + v2[0:kt // (a + 1), :] for a in range(kt)]
        pad = n_cand - sum(kt // (a + 1) for a in range(kt))
        pieces.append(jnp.full((pad, tm), -jnp.inf, F32))
        cand = jnp.concatenate(pieces, axis=0)
        cand_scr[...] = cand
        csel_scr[...] = jnp.full((n_cand, tm), big, F32)
        _extract_topk([(cand_scr, csel_scr, cval_scr)], kt)
        sel_f = jnp.where(csel_scr[...] < big, 1.0, 0.0)
        top = v1[0:1, :] + v2[0:1, :]
        z = jnp.sum(sel_f * jnp.exp(cand - top), axis=0, keepdims=True)
        r1 = r1_scr[...]
        n1 = jnp.zeros((nk, tm), F32)
        off = 0
        for a in range(kt):
            cnt = kt // (a + 1)
            n_a = jnp.sum(sel_f[off:off + cnt, :], axis=0, keepdims=True)
            off += cnt
            n1 = jnp.where(r1 == float(a), n_a, n1)
        e1_ref[h] = jnp.exp(s1 - v1[0:1, :])
        n1_ref[h] = n1
        e2_ref[h] = (0.5 * jnp.exp(s2 - v2[0:1, :])) / z
        r2_ref[h] = r2_scr[...]
        return carry

    lax.fori_loop(0, PEER_HEADS, head, 0)


def _peer_stats(x1, g, wq_h, keys_bf16, tm):
    t, d = x1.shape
    nk, kt = PEER_NKEYS, PEER_TOPK
    stat_spec = pl.BlockSpec((PEER_HEADS, nk, tm), lambda i: (0, 0, i))
    stat_shape = jax.ShapeDtypeStruct((PEER_HEADS, nk, t), F32)
    full = lambda a: pl.BlockSpec(a.shape, lambda i: tuple(0 for _ in a.shape))
    n_cells = sum(kt // (a + 1) for a in range(kt))
    n_cand = -(-n_cells // 8) * 8
    return pl.pallas_call(
        functools.partial(_peer_stats_kernel, tm=tm),
        grid=(t // tm,),
        in_specs=[pl.BlockSpec((tm, d), lambda i: (i, 0)), pl.BlockSpec((1, d), lambda i: (0, 0)),
                  full(wq_h), full(keys_bf16)],
        out_specs=[pl.BlockSpec((tm, d), lambda i: (i, 0))] + [stat_spec] * 4,
        out_shape=[jax.ShapeDtypeStruct((t, d), BF16)] + [stat_shape] * 4,
        scratch_shapes=[pltpu.VMEM((nk, tm), F32), pltpu.VMEM((nk, tm), F32),
                        pltpu.VMEM((nk, tm), F32), pltpu.VMEM((nk, tm), F32),
                        pltpu.VMEM((kt, tm), F32), pltpu.VMEM((kt, tm), F32),
                        pltpu.VMEM((n_cand, tm), F32), pltpu.VMEM((n_cand, tm), F32),
                        pltpu.VMEM((kt, tm), F32)],
        compiler_params=_cparams("parallel"),
        name="peer_stats",
    )(x1, g.reshape(1, d), wq_h, keys_bf16)


PEER_KEY_GROUP = 8


PEER_STAGES = 3


def _peer_dense_kernel(xn_ref, u_ref, vt_ref, e1_ref, n1_ref, e2_ref, r2_ref, o_ref, ht_scr, at_scr,
                       *, tm, eb, nb):
    nk = PEER_NKEYS
    s = pl.program_id(0)
    n_keys = eb // nk

    @pl.when(s == 0)
    def _():
        ht_scr[...] = jnp.zeros(ht_scr.shape, F32)
        at_scr[...] = jnp.zeros(at_scr.shape, BF16)

    @pl.when((s < PEER_STAGES) | ((s - (PEER_STAGES - 1)) % nb == 0))
    def _():
        o_ref[...] = jnp.zeros(o_ref.shape, F32)

    def step(slot):
        o_ref[...] += jnp.dot(vt_ref[...], at_scr[slot], preferred_element_type=F32)

        jb = jnp.maximum(s - 1, 0) % nb
        n1_rows = [[n1_ref[h, pl.ds(jb * n_keys + c, 1), :] for c in range(n_keys)]
                   for h in range(PEER_HEADS)]
        e1_rows = [[e1_ref[h, pl.ds(jb * n_keys + c, 1), :] for c in range(n_keys)]
                   for h in range(PEER_HEADS)]
        ht = ht_scr[1 - slot]
        at_scr[1 - slot] = (ht * (1.0 + lax.erf(ht * (1.0 / math.sqrt(2.0))))).astype(BF16)
        pk = BF16_ROWS
        for lt in range(tm // LANE):
            lanes = slice(lt * LANE, (lt + 1) * LANE)
            gates = []
            for c0 in range(0, n_keys, PEER_KEY_GROUP):
                grp = [jnp.zeros((nk // pk, pk, LANE), BF16) for _ in range(PEER_KEY_GROUP)]
                for h in range(PEER_HEADS):
                    r2 = r2_ref[h, :, lanes].astype(BF16).reshape(nk // pk, pk, LANE)
                    e2 = e2_ref[h, :, lanes].astype(BF16).reshape(nk // pk, pk, LANE)
                    for c in range(PEER_KEY_GROUP):
                        n1 = jnp.broadcast_to(n1_rows[h][c0 + c][:, lanes], (pk, LANE)).astype(BF16)
                        e1 = jnp.broadcast_to(e1_rows[h][c0 + c][:, lanes], (pk, LANE)).astype(BF16)
                        grp[c] = grp[c] + jnp.where(r2 < n1, e2, jnp.zeros_like(e2)) * e1
                gates.extend(g.reshape(nk, LANE) for g in grp)
            gate = jnp.concatenate(gates, axis=0)
            at_scr[1 - slot, :, lanes] = at_scr[1 - slot, :, lanes] * gate

        ht_scr[slot] = lax.dot_general(u_ref[...], xn_ref[...], _NT, preferred_element_type=F32)

    step(s % 2)


def _peer_dense(xn, u_bf16, vt_bf16, e1, n1, e2, r2, tm, eb):
    t, d = xn.shape
    n_exp = u_bf16.shape[0]
    nb = n_exp // eb
    n_items = (t // tm) * nb
    item = lambda s, lag: jnp.clip(s - lag, 0, n_items - 1)
    stat_spec = pl.BlockSpec((PEER_HEADS, PEER_NKEYS, tm), lambda s: (0, 0, item(s, 1) // nb),
                             pipeline_mode=pl.Buffered(1))
    return pl.pallas_call(
        functools.partial(_peer_dense_kernel, tm=tm, eb=eb, nb=nb),
        grid=(n_items + PEER_STAGES - 1,),
        in_specs=[pl.BlockSpec((tm, d), lambda s: (item(s, 0) // nb, 0)),
                  pl.BlockSpec((eb, d), lambda s: (item(s, 0) % nb, 0)),
                  pl.BlockSpec((None, d, eb), lambda s: (item(s, 2) % nb, 0, 0)),
                  stat_spec, stat_spec, stat_spec, stat_spec],
        out_specs=pl.BlockSpec((d, tm), lambda s: (0, item(s, 2) // nb)),
        out_shape=jax.ShapeDtypeStruct((d, t), F32),
        scratch_shapes=[pltpu.VMEM((2, eb, tm), F32), pltpu.VMEM((2, eb, tm), BF16)],
        compiler_params=_cparams("arbitrary"),
        name="peer_dense",
    )(xn, u_bf16, vt_bf16, e1, n1, e2, r2)


def _final_kernel(x1_ref, yt_ref, g_ref, o_ref):
    y = x1_ref[...] + yt_ref[...].T
    o_ref[...] = y * lax.rsqrt(jnp.mean(y * y, axis=-1, keepdims=True) + EPS) * g_ref[...]


def _final_norm(x1, yt, g_final, tm):
    t, d = x1.shape
    return pl.pallas_call(
        _final_kernel,
        grid=(t // tm,),
        in_specs=[pl.BlockSpec((tm, d), lambda i: (i, 0)), pl.BlockSpec((d, tm), lambda i: (0, i)),
                  pl.BlockSpec((1, d), lambda i: (0, 0))],
        out_specs=pl.BlockSpec((tm, d), lambda i: (i, 0)),
        out_shape=jax.ShapeDtypeStruct((t, d), F32),
        compiler_params=_cparams("parallel"),
        name="peer_residual_final_norm",
    )(x1, yt, g_final.reshape(1, d))


def _gate_expand_matrix():
    r = jnp.arange(GATE_PAD)[:, None]
    c = jnp.arange(3 * NSA_WIDTH)[None, :]
    return ((r < GATE_COLS) & (r == (c // NSA_WIDTH) * NSA_HEADS + (c % NSA_WIDTH) // NSA_HEAD_DIM)).astype(F32)


def kernel(x, g_mix, w_in, pe_cmp_k, pe_cmp_v, w_cmp_k1, w_cmp_k2, w_cmp_v1, w_cmp_v2, w_nsa_out, w_dw, b_dw,
           g_conv_ln, b_conv_ln, w_conv_out, w_o, g_ffn, w_peer_q, peer_sub_keys, peer_u, peer_v, g_final):
    b, s, d = x.shape
    t = b * s
    depth = w_in.shape[0]
    assert depth == 1, "the fused final norm assumes a single layer"
    assert s % 512 == 0 and d % LANE == 0
    x2 = x.reshape(t, d)
    wexp = _gate_expand_matrix().astype(BF16)
    l = 0
    gate_end = QKV_COLS + GATE_COLS
    w_pad = jnp.concatenate([w_in[l][:, :gate_end], jnp.zeros((d, GATE_PAD - GATE_COLS), F32),
                             w_in[l][:, gate_end:]], axis=1).astype(BF16)
    splits = (QKV_COLS, GATE_PAD, 2 * CONV_CH, 2 * d)
    qkv, gates, glu, merge = _norm_proj(x2, g_mix[l], w_pad, splits, tm=256)

    kcmp, vcmp = _compress(qkv, b, s, pe_cmp_k[l], pe_cmp_v[l], w_cmp_k1[l], w_cmp_k2[l],
                           w_cmp_v1[l], w_cmp_v2[l])
    o_cmp, sel = _cmp_attn(qkv, kcmp, vcmp, b, s, tq=256)
    kaug_slc, kaug_win = _attn_prep(qkv, b, s, ts=512)
    o_slc = _flash(qkv, kaug_slc, sel, b, s, "slc")
    o_win = _flash(qkv, kaug_win, None, b, s, "win")

    x1 = _mix(o_cmp, o_slc, o_win, gates, glu, merge, x2, wexp,
              w_nsa_out[l].astype(BF16), w_dw[l].reshape(CONV_WIDTH, CONV_CH), b_dw[l].reshape(1, -1),
              g_conv_ln[l].reshape(1, -1), b_conv_ln[l].reshape(1, -1),
              w_conv_out[l].astype(BF16), w_o[l].astype(BF16), s, tm=256)

    wq_h = w_peer_q[l].reshape(d, PEER_HEADS, PEER_QDIM).transpose(1, 0, 2).astype(BF16)
    xn, e1, n1, e2, r2 = _peer_stats(x1, g_ffn[l], wq_h, peer_sub_keys[l].astype(BF16), tm=256)
    eb = 1024
    n_exp = peer_v.shape[1]
    vt_blocks = peer_v[l].astype(BF16).reshape(n_exp // eb, eb, d).transpose(0, 2, 1)
    yt = _peer_dense(xn, peer_u[l].astype(BF16), vt_blocks, e1, n1, e2, r2, tm=1024, eb=eb)
    out = _final_norm(x1, yt, g_final, tm=512)
    return out.reshape(b, s, d)
```

```python
import functools
import math

import jax
import jax.numpy as jnp
from jax import lax
from jax.experimental import pallas as pl
from jax.experimental.pallas import tpu as pltpu

F32 = jnp.float32
BF16 = jnp.bfloat16

NSA_HEADS = 8
NSA_KV_GROUPS = 2
NSA_HPG = NSA_HEADS // NSA_KV_GROUPS
NSA_HEAD_DIM = 64
NSA_WIDTH = NSA_HEADS * NSA_HEAD_DIM
NSA_KV_WIDTH = NSA_KV_GROUPS * NSA_HEAD_DIM
CMP_BLOCK = 32
CMP_STRIDE = 16
SLC_BLOCK = 64
SLC_TOPN = 16
WINDOW = 512
CONV_CH = 512
CONV_WIDTH = 31
PEER_HEADS = 8
PEER_NKEYS = 128
PEER_QDIM = 256
PEER_HALF = PEER_QDIM // 2
PEER_TOPK = 16
EPS = 1e-6
NEG = -1e30
FORCED = 1e9
SLOPES = tuple(2.0 ** (-8.0 * (h + 1) / NSA_HEADS) for h in range(NSA_HEADS))

LANE = 128
BF16_ROWS = 16
GATE_COLS = 3 * NSA_HEADS
GATE_PAD = LANE
QKV_COLS = NSA_WIDTH + 6 * NSA_KV_WIDTH
VMEM_LIMIT = 56 * 1024 * 1024

_NT = (((1,), (1,)), ((), ()))
_TN = (((0,), (0,)), ((), ()))


def _cparams(*sem):
    return pltpu.CompilerParams(dimension_semantics=sem, vmem_limit_bytes=VMEM_LIMIT)


def _sigmoid(x):
    return 1.0 / (1.0 + jnp.exp(-x))


def _gelu(x):
    return 0.5 * x * (1.0 + lax.erf(x * (1.0 / math.sqrt(2.0))))


def _norm_proj_kernel(x_ref, g_ref, w_ref, *out_refs, splits):
    x = x_ref[...]
    xn = x * lax.rsqrt(jnp.mean(x * x, axis=-1, keepdims=True) + EPS) * g_ref[...]
    xb = xn.astype(BF16)
    off = 0
    for o_ref, n in zip(out_refs, splits):
        o_ref[...] = jnp.dot(xb, w_ref[:, off:off + n], preferred_element_type=F32)
        off += n


def _norm_proj(x2, g, w_bf16, splits, tm):
    t, d = x2.shape
    n = w_bf16.shape[1]
    return pl.pallas_call(
        functools.partial(_norm_proj_kernel, splits=splits),
        grid=(t // tm,),
        in_specs=[pl.BlockSpec((tm, d), lambda i: (i, 0)),
                  pl.BlockSpec((1, d), lambda i: (0, 0)),
                  pl.BlockSpec((d, n), lambda i: (0, 0))],
        out_specs=[pl.BlockSpec((tm, s), lambda i: (i, 0)) for s in splits],
        out_shape=[jax.ShapeDtypeStruct((t, s), F32) for s in splits],
        compiler_params=_cparams("parallel"),
        name="norm_in_proj",
    )(x2, g.reshape(1, d), w_bf16)


def _compress_kernel(k_ref, v_ref, pek_ref, pev_ref, wk1_ref, wk2_ref, wv1_ref, wv2_ref,
                     kc_ref, vc_ref, *, n_blk):
    hp = lax.Precision.HIGHEST
    dh = NSA_HEAD_DIM
    per = CMP_BLOCK // CMP_STRIDE
    assert per == 2

    def one(src_ref, pe_ref, w1_ref, w2_ref, dst_ref):
        pe8 = jnp.broadcast_to(pe_ref[...], (8, CMP_BLOCK * dh))
        const = jnp.dot(pe8, w1_ref[...], precision=hp, preferred_element_type=F32)[0:1, :]
        acc = [[jnp.zeros((n_blk, dh), F32) for _ in range(per)] for _ in range(NSA_KV_GROUPS)]
        for lo in range(CMP_STRIDE):
            rows = src_ref[pl.ds(lo, n_blk, stride=CMP_STRIDE), :]
            for g in range(NSA_KV_GROUPS):
                rg = rows[:, g * dh:(g + 1) * dh]
                for hi in range(per):
                    l = hi * CMP_STRIDE + lo
                    acc[g][hi] = acc[g][hi] + jnp.dot(rg, w1_ref[l * dh:(l + 1) * dh, :], precision=hp,
                                                     preferred_element_type=F32)
        for g in range(NSA_KV_GROUPS):
            pre = acc[g][0] + pltpu.roll(acc[g][1], n_blk - 1, 0) + const
            out = jnp.dot(_gelu(pre), w2_ref[...], precision=hp, preferred_element_type=F32)
            rid = lax.broadcasted_iota(jnp.int32, (n_blk, dh), 0)
            dst_ref[0, g] = jnp.where(rid < n_blk - 1, out, 0.0)

    one(k_ref, pek_ref, wk1_ref, wk2_ref, kc_ref)
    one(v_ref, pev_ref, wv1_ref, wv2_ref, vc_ref)


def _compress(qkv, b, s, pe_k, pe_v, wk1, wk2, wv1, wv2):
    n_blk = s // CMP_STRIDE
    dh = NSA_HEAD_DIM
    kcol = NSA_WIDTH // LANE
    full = lambda shape: pl.BlockSpec(shape, lambda i: tuple(0 for _ in shape))
    out_spec = pl.BlockSpec((1, NSA_KV_GROUPS, n_blk, dh), lambda i: (i, 0, 0, 0))
    out_shape = jax.ShapeDtypeStruct((b, NSA_KV_GROUPS, n_blk, dh), F32)
    return pl.pallas_call(
        functools.partial(_compress_kernel, n_blk=n_blk),
        grid=(b,),
        in_specs=[pl.BlockSpec((s, LANE), lambda i: (i, kcol)),
                  pl.BlockSpec((s, LANE), lambda i: (i, kcol + 1)),
                  full((1, CMP_BLOCK * dh)), full((1, CMP_BLOCK * dh)),
                  full((CMP_BLOCK * dh, dh)), full((dh, dh)),
                  full((CMP_BLOCK * dh, dh)), full((dh, dh))],
        out_specs=[out_spec, out_spec],
        out_shape=[out_shape, out_shape],
        compiler_params=_cparams("parallel"),
        name="nsa_compress",
    )(qkv, qkv, pe_k.reshape(1, -1), pe_v.reshape(1, -1), wk1, wk2, wv1, wv2)


def _cmp_attn_kernel(q_ref, kc_ref, vc_ref, o_ref, sel_ref, ot_scr, *, tq, n_blk, n_slc):
    hp = lax.Precision.HIGHEST
    dh = NSA_HEAD_DIM
    sub = 8
    rows = NSA_HPG * tq
    q0 = pl.program_id(1) * tq
    n_id = lax.broadcasted_iota(jnp.int32, (n_blk, rows), 0)
    t_id = q0 + lax.broadcasted_iota(jnp.int32, (n_blk, rows), 1) % tq
    mask = n_id * CMP_STRIDE + (CMP_BLOCK - 1) <= t_id
    jj = lax.broadcasted_iota(jnp.int32, (n_slc, n_blk), 0) * SLC_BLOCK
    nn = lax.broadcasted_iota(jnp.int32, (n_slc, n_blk), 1) * CMP_STRIDE
    overlap_t = jnp.where((nn <= jj + SLC_BLOCK - 1) & (nn + CMP_BLOCK - 1 >= jj), 1.0, 0.0)
    jb = lax.broadcasted_iota(jnp.int32, (n_slc, tq), 0)
    t_blk = (q0 + lax.broadcasted_iota(jnp.int32, (n_slc, tq), 1)) // SLC_BLOCK
    future = jb > t_blk
    forced = (jb == 0) | (jb == t_blk) | (jb == t_blk - 1)
    n_sel = min(SLC_TOPN, n_slc)
    lk = lax.broadcasted_iota(jnp.int32, (n_blk, dh), 1)
    nk_f = (lax.broadcasted_iota(jnp.int32, (n_blk, dh), 0) * CMP_STRIDE).astype(F32)
    cols_k = jnp.where(lk == 0, nk_f, 0.0)
    lq = lax.broadcasted_iota(jnp.int32, (tq, dh), 1)

    for g in range(NSA_KV_GROUPS):
        kca = jnp.concatenate([kc_ref[0, g], cols_k], axis=1).astype(BF16)
        vc = vc_ref[0, g].astype(BF16)
        blocks = []
        for h in range(NSA_HPG):
            hh = g * NSA_HPG + h
            qh = q_ref[:, hh * dh:(hh + 1) * dh] * (1.0 / math.sqrt(dh))
            blocks.append(jnp.concatenate([qh, jnp.where(lq == 0, SLOPES[hh], 0.0)], axis=1))
        qa = jnp.concatenate(blocks, axis=0).astype(BF16)
        st = lax.dot_general(kca, qa, _NT, preferred_element_type=F32)
        st = jnp.where(mask, st, NEG)
        m = jnp.max(st, axis=0, keepdims=True)
        e = jnp.where(mask, jnp.exp(st - m), 0.0)
        l = jnp.sum(e, axis=0, keepdims=True)
        p = e / jnp.where(l > 0.0, l, 1.0)
        ot_scr[g * dh:(g + 1) * dh, :] = lax.dot_general(vc, p.astype(BF16), _TN, preferred_element_type=F32)
        psum = p[:, 0:tq]
        for h in range(1, NSA_HPG):
            psum = psum + p[:, h * tq:(h + 1) * tq]
        imp = jnp.dot(overlap_t, psum, precision=hp, preferred_element_type=F32)
        imp = jnp.where(forced, FORCED, jnp.where(future, NEG, imp))
        slabs = [imp[v * sub:(v + 1) * sub, :] for v in range(n_slc // sub)]
        cnts = [jnp.zeros((sub, tq), F32) for _ in slabs]
        jrow = lax.broadcasted_iota(jnp.int32, (sub, tq), 0)
        for j2 in range(n_slc):
            row = imp[j2:j2 + 1, :]
            for v, slab in enumerate(slabs):
                if v * sub > j2:
                    before = row >= slab
                elif v * sub + sub - 1 <= j2:
                    before = row > slab
                else:
                    before = jnp.where(jrow + v * sub > j2, jnp.where(row >= slab, 1.0, 0.0),
                                       jnp.where(row > slab, 1.0, 0.0)) > 0.5
                cnts[v] = cnts[v] + jnp.where(before, 1.0, 0.0)
        sel_t = jnp.where(jnp.concatenate(cnts, axis=0) < float(n_sel), 1.0, 0.0)
        if n_slc < LANE:
            sel_t = jnp.concatenate([sel_t, jnp.zeros((LANE - n_slc, tq), F32)], axis=0)
        sel_ref[g] = sel_t.T[:, :n_slc]

    out = ot_scr[...].T
    for g in range(NSA_KV_GROUPS):
        for h in range(NSA_HPG):
            hh = g * NSA_HPG + h
            o_ref[:, hh * dh:(hh + 1) * dh] = out[h * tq:(h + 1) * tq, g * dh:(g + 1) * dh]


def _cmp_attn(qkv, kcmp, vcmp, b, s, tq):
    t = b * s
    nq = s // tq
    n_blk = kcmp.shape[2]
    n_slc = s // SLC_BLOCK
    cmp_spec = pl.BlockSpec((1, NSA_KV_GROUPS, n_blk, NSA_HEAD_DIM), lambda bi, i: (bi, 0, 0, 0))
    return pl.pallas_call(
        functools.partial(_cmp_attn_kernel, tq=tq, n_blk=n_blk, n_slc=n_slc),
        grid=(b, nq),
        in_specs=[pl.BlockSpec((tq, NSA_WIDTH), lambda bi, i: (bi * nq + i, 0)), cmp_spec, cmp_spec],
        out_specs=[pl.BlockSpec((tq, NSA_WIDTH), lambda bi, i: (bi * nq + i, 0)),
                   pl.BlockSpec((NSA_KV_GROUPS, tq, n_slc), lambda bi, i: (0, bi * nq + i, 0))],
        out_shape=[jax.ShapeDtypeStruct((t, NSA_WIDTH), F32),
                   jax.ShapeDtypeStruct((NSA_KV_GROUPS, t, n_slc), F32)],
        scratch_shapes=[pltpu.VMEM((NSA_KV_GROUPS * NSA_HEAD_DIM, NSA_HPG * tq), F32)],
        compiler_params=_cparams("parallel", "parallel"),
        name="nsa_cmp_attn_select",
    )(qkv, kcmp, vcmp)


ATT_TILE = 256
AUG_SLC = 2 * LANE
AUG_WIN = LANE


def _attn_prep_kernel(ks_ref, kw_ref, kas_ref, kaw_ref, *, ts):
    dh = NSA_HEAD_DIM
    s0 = pl.program_id(1) * ts
    pos = s0 + lax.broadcasted_iota(jnp.int32, (ts, dh), 0)
    lane = lax.broadcasted_iota(jnp.int32, (ts, dh), 1)
    onehot = jnp.where(lane == pos // SLC_BLOCK, 1.0, 0.0)
    off = (pos % ATT_TILE).astype(F32)
    cols = jnp.where(lane == 0, 1.0, jnp.where(lane == 1, off, 0.0))
    zeros = jnp.zeros((ts, dh), F32)
    for g in range(NSA_KV_GROUPS):
        ks = ks_ref[:, g * dh:(g + 1) * dh]
        kas_ref[0, g] = jnp.concatenate([ks, onehot, cols, zeros], axis=1).astype(BF16)
        kw = kw_ref[:, g * dh:(g + 1) * dh]
        kaw_ref[0, g] = jnp.concatenate([kw, cols], axis=1).astype(BF16)


def _attn_prep(qkv, b, s, ts):
    kcol = NSA_WIDTH // LANE + 2
    return pl.pallas_call(
        functools.partial(_attn_prep_kernel, ts=ts),
        grid=(b, s // ts),
        in_specs=[pl.BlockSpec((ts, LANE), lambda bi, i: (bi * (s // ts) + i, kcol)),
                  pl.BlockSpec((ts, LANE), lambda bi, i: (bi * (s // ts) + i, kcol + 2))],
        out_specs=[pl.BlockSpec((1, NSA_KV_GROUPS, ts, AUG_SLC), lambda bi, i: (bi, 0, i, 0)),
                   pl.BlockSpec((1, NSA_KV_GROUPS, ts, AUG_WIN), lambda bi, i: (bi, 0, i, 0))],
        out_shape=[jax.ShapeDtypeStruct((b, NSA_KV_GROUPS, s, AUG_SLC), BF16),
                   jax.ShapeDtypeStruct((b, NSA_KV_GROUPS, s, AUG_WIN), BF16)],
        compiler_params=_cparams("parallel", "parallel"),
        name="nsa_key_augment",
    )(qkv, qkv)


def _flash_kernel(*refs, mode, tq):
    if mode == "slc":
        q_ref, ka_ref, v_ref, sel_ref, o_ref, qa_scr, m_scr, l_scr, acc_scr = refs
    else:
        q_ref, ka_ref, v_ref, o_ref, qa_scr, m_scr, l_scr, acc_scr = refs
        sel_ref = None
    dh = NSA_HEAD_DIM
    tk = tq
    rows = NSA_HPG * tq
    i = pl.program_id(1)
    q0 = i * tq
    qid = lax.broadcasted_iota(jnp.int32, (tk, rows), 1) % tq
    kid = lax.broadcasted_iota(jnp.int32, (tk, rows), 0)
    head_of_row = lax.broadcasted_iota(jnp.int32, (1, rows), 1) // tq
    rq = lax.broadcasted_iota(jnp.int32, (tq, dh), 0).astype(F32)
    lq = lax.broadcasted_iota(jnp.int32, (tq, dh), 1)

    slope_rows = []
    for g in range(NSA_KV_GROUPS):
        slopes = [SLOPES[g * NSA_HPG + h] for h in range(NSA_HPG)]
        slope_row = jnp.full((1, rows), slopes[-1], F32)
        for h in range(NSA_HPG - 2, -1, -1):
            slope_row = jnp.where(head_of_row == h, slopes[h], slope_row)
        slope_rows.append(slope_row)
        blocks = []
        for h in range(NSA_HPG):
            hh = g * NSA_HPG + h
            qh = q_ref[:, hh * dh:(hh + 1) * dh] * (1.0 / math.sqrt(dh))
            cols = jnp.where(lq == 0, -slopes[h] * rq, jnp.where(lq == 1, slopes[h], 0.0))
            if mode == "slc":
                n_slc = sel_ref.shape[-1]
                notsel = (sel_ref[g] - 1.0) * (-NEG)
                if n_slc < dh:
                    notsel = jnp.concatenate([notsel, jnp.zeros((tq, dh - n_slc), F32)], axis=1)
                parts = [qh, notsel, cols, jnp.zeros((tq, dh), F32)]
            else:
                parts = [qh, cols]
            blocks.append(jnp.concatenate(parts, axis=1))
        qa_scr[g] = jnp.concatenate(blocks, axis=0).astype(BF16)
    m_scr[...] = jnp.full(m_scr.shape, NEG, F32)
    l_scr[...] = jnp.zeros(l_scr.shape, F32)
    acc_scr[...] = jnp.zeros(acc_scr.shape, F32)

    def tiles(specs):
        for g in range(NSA_KV_GROUPS):
            sts, shifts, vgs = [], [], []
            for kt, mask in specs:
                k0 = pl.multiple_of(kt * tk, tk)
                ka = ka_ref[0, g, pl.ds(k0, tk), :]
                vgs.append(v_ref[pl.ds(k0, tk), g * dh:(g + 1) * dh].astype(BF16))
                st = lax.dot_general(ka, qa_scr[g], _NT, preferred_element_type=F32)
                sts.append(st if mask is None else jnp.where(mask, st, NEG))
                shifts.append(slope_rows[g] * jnp.asarray(k0 - q0, F32))
            m_old = m_scr[g]
            m_new = m_old
            for st, shift in zip(sts, shifts):
                m_new = jnp.maximum(m_new, jnp.max(st, axis=0, keepdims=True) + shift)
            alpha = jnp.exp(m_old - m_new)
            l_new = alpha * l_scr[g]
            acc = alpha * acc_scr[g * dh:(g + 1) * dh, :]
            for st, shift, vg in zip(sts, shifts, vgs):
                p = jnp.exp(st - (m_new - shift))
                l_new = l_new + jnp.sum(p, axis=0, keepdims=True)
                acc = acc + lax.dot_general(vg, p.astype(BF16), _TN, preferred_element_type=F32)
            l_scr[g] = l_new
            acc_scr[g * dh:(g + 1) * dh, :] = acc
            m_scr[g] = m_new

    diag = kid <= qid
    if mode == "slc":
        def body(k2, carry):
            tiles([(2 * k2, None), (2 * k2 + 1, None)])
            return carry
        lax.fori_loop(0, i // 2, body, 0)

        @pl.when(i % 2 == 1)
        def _():
            tiles([(i - 1, None), (i, diag)])

        @pl.when(i % 2 == 0)
        def _():
            tiles([(i, diag)])
    else:
        @pl.when(i == 0)
        def _():
            tiles([(i, diag)])

        @pl.when(i == 1)
        def _():
            tiles([(i, diag), (i - 1, None)])

        @pl.when(i >= 2)
        def _():
            tiles([(i, diag), (i - 1, None), (i - 2, kid > qid)])

    for g in range(NSA_KV_GROUPS):
        acc_scr[g * dh:(g + 1) * dh, :] = acc_scr[g * dh:(g + 1) * dh, :] / l_scr[g]

    out = acc_scr[...].T
    for g in range(NSA_KV_GROUPS):
        for h in range(NSA_HPG):
            hh = g * NSA_HPG + h
            o_ref[:, hh * dh:(hh + 1) * dh] = out[h * tq:(h + 1) * tq, g * dh:(g + 1) * dh]


def _flash(qkv, kaug, sel, b, s, mode):
    tq = ATT_TILE
    assert WINDOW == 2 * tq and s % tq == 0
    t = b * s
    nq = s // tq
    aug = kaug.shape[-1]
    vcol = NSA_WIDTH // LANE + {"slc": 3, "win": 5}[mode]
    in_specs = [pl.BlockSpec((tq, NSA_WIDTH), lambda bi, i: (bi * nq + i, 0)),
                pl.BlockSpec((1, NSA_KV_GROUPS, s, aug), lambda bi, i: (bi, 0, 0, 0)),
                pl.BlockSpec((s, LANE), lambda bi, i: (bi, vcol))]
    args = [qkv, kaug, qkv]
    if mode == "slc":
        assert sel.shape[-1] <= NSA_HEAD_DIM
        in_specs.append(pl.BlockSpec((NSA_KV_GROUPS, tq, sel.shape[-1]), lambda bi, i: (0, bi * nq + i, 0)))
        args.append(sel)
    rows = NSA_HPG * tq
    return pl.pallas_call(
        functools.partial(_flash_kernel, mode=mode, tq=tq),
        grid=(b, nq),
        in_specs=in_specs,
        out_specs=pl.BlockSpec((tq, NSA_WIDTH), lambda bi, i: (bi * nq + i, 0)),
        out_shape=jax.ShapeDtypeStruct((t, NSA_WIDTH), F32),
        scratch_shapes=[pltpu.VMEM((NSA_KV_GROUPS, rows, aug), BF16),
                        pltpu.VMEM((NSA_KV_GROUPS, 1, rows), F32),
                        pltpu.VMEM((NSA_KV_GROUPS, 1, rows), F32),
                        pltpu.VMEM((NSA_KV_GROUPS * NSA_HEAD_DIM, rows), F32)],
        compiler_params=_cparams("parallel", "parallel"),
        name="nsa_flash_" + mode,
    )(*args)


HALO = 32
CONV_SHIFTS = 8


def _mix_kernel(ocmp_ref, oslc_ref, owin_ref, gates_ref, glu_ref, halo_ref, merge_ref, x_ref,
                wexp_ref, wnsa_ref, wdw_ref, bdw_ref, gln_ref, bln_ref, wconv_ref, wo_ref,
                o_ref, uext_scr, *, tm, tiles_per_seq):
    i = pl.program_id(0)
    gts = _sigmoid(gates_ref[...])
    g_hi = gts.astype(BF16)
    r1 = gts - g_hi.astype(F32)
    g_mid = r1.astype(BF16)
    g_lo = (r1 - g_mid.astype(F32)).astype(BF16)
    wexp = wexp_ref[...]
    gexp = (jnp.dot(g_hi, wexp, preferred_element_type=F32) + jnp.dot(g_mid, wexp, preferred_element_type=F32)
            + jnp.dot(g_lo, wexp, preferred_element_type=F32))
    w = NSA_WIDTH
    o_nsa = gexp[:, :w] * ocmp_ref[...] + gexp[:, w:2 * w] * oslc_ref[...] + gexp[:, 2 * w:] * owin_ref[...]
    y_a = jnp.dot(o_nsa.astype(BF16), wnsa_ref[...], preferred_element_type=F32)

    c = CONV_CH
    gl = glu_ref[...]
    u = gl[:, :c] * _sigmoid(gl[:, c:])
    hl = halo_ref[...]
    uh = hl[:, :c] * _sigmoid(hl[:, c:])
    uh = jnp.where(i % tiles_per_seq == 0, 0.0, uh)
    uext_scr[0, 0:HALO, :] = uh
    uext_scr[0, HALO:HALO + tm, :] = u
    span = HALO + tm - CONV_SHIFTS
    for j in range(1, CONV_SHIFTS):
        uext_scr[j, 0:span, :] = uext_scr[0, pl.ds(j, span), :]
    acc = jnp.zeros((tm, c), F32)
    for k in range(CONV_WIDTH):
        first = HALO - (CONV_WIDTH - 1) + k
        acc = acc + uext_scr[first % CONV_SHIFTS, pl.ds(first - first % CONV_SHIFTS, tm), :] * wdw_ref[k:k + 1, :]
    cv = acc + bdw_ref[...]
    mu = jnp.mean(cv, axis=-1, keepdims=True)
    var = jnp.mean(jnp.square(cv - mu), axis=-1, keepdims=True)
    un = (cv - mu) * lax.rsqrt(var + EPS) * gln_ref[...] + bln_ref[...]
    act = un * _sigmoid(un)
    y_b = jnp.dot(act.astype(BF16), wconv_ref[...], preferred_element_type=F32)

    d = x_ref.shape[-1]
    mg = merge_ref[...]
    z = _sigmoid(mg[:, :d]) * y_a + _sigmoid(mg[:, d:]) * y_b
    o_ref[...] = x_ref[...] + jnp.dot(z.astype(BF16), wo_ref[...], preferred_element_type=F32)


def _mix(ocmp, oslc, owin, gates, glu, merge, x2, wexp, wnsa, wdw, bdw, gln, bln, wconv, wo, s, tm):
    t, d = x2.shape
    row = lambda n: pl.BlockSpec((tm, n), lambda i: (i, 0))
    full = lambda a: pl.BlockSpec(a.shape, lambda i: tuple(0 for _ in a.shape))
    halo_spec = pl.BlockSpec((HALO, glu.shape[1]), lambda i: (jnp.maximum(i * (tm // HALO) - 1, 0), 0))
    weights = [wexp, wnsa, wdw, bdw, gln, bln, wconv, wo]
    return pl.pallas_call(
        functools.partial(_mix_kernel, tm=tm, tiles_per_seq=s // tm),
        grid=(t // tm,),
        in_specs=[row(NSA_WIDTH), row(NSA_WIDTH), row(NSA_WIDTH), row(GATE_PAD), row(glu.shape[1]),
                  halo_spec, row(merge.shape[1]), row(d)] + [full(a) for a in weights],
        out_specs=row(d),
        out_shape=jax.ShapeDtypeStruct((t, d), F32),
        scratch_shapes=[pltpu.VMEM((CONV_SHIFTS, HALO + tm, CONV_CH), F32)],
        compiler_params=_cparams("parallel"),
        name="mixer_merge",
    )(ocmp, oslc, owin, gates, glu, glu, merge, x2, *weights)


def _extract_topk(jobs, k_top):
    def body(k, carry):
        for cur_ref, rank_ref, val_ref in jobs:
            cur = cur_ref[...]
            n_rows = cur.shape[0]
            rid = lax.broadcasted_iota(jnp.int32, cur.shape, 0).astype(F32)
            v = jnp.max(cur, axis=0, keepdims=True)
            idx = jnp.min(jnp.where(cur == v, rid, float(n_rows)), axis=0, keepdims=True)
            hit = rid == idx
            rank_ref[...] = jnp.where(hit, jnp.asarray(k, F32), rank_ref[...])
            cur_ref[...] = jnp.where(hit, -jnp.inf, cur)
            val_ref[pl.ds(k, 1), :] = v
        return carry

    lax.fori_loop(0, k_top, body, 0)


def _peer_stats_kernel(x_ref, g_ref, wq_ref, keys_ref, xn_ref, e1_ref, n1_ref, e2_ref, r2_ref,
                       c1_scr, c2_scr, r1_scr, r2_scr, v1_scr, v2_scr, cand_scr, csel_scr, cval_scr, *, tm):
    nk = PEER_NKEYS
    kt = PEER_TOPK
    x = x_ref[...]
    xn = (x * lax.rsqrt(jnp.mean(x * x, axis=-1, keepdims=True) + EPS) * g_ref[...]).astype(BF16)
    xn_ref[...] = xn
    big = float(nk)
    n_cand = cand_scr.shape[0]

    def head(h, carry):
        qp = jnp.dot(xn, wq_ref[h], preferred_element_type=F32)
        s1 = lax.dot_general(keys_ref[h, 0], qp[:, :PEER_HALF].astype(BF16), _NT,
                             preferred_element_type=F32)
        s2 = lax.dot_general(keys_ref[h, 1], qp[:, PEER_HALF:].astype(BF16), _NT,
                             preferred_element_type=F32)
        c1_scr[...] = s1
        c2_scr[...] = s2
        r1_scr[...] = jnp.full((nk, tm), big, F32)
        r2_scr[...] = jnp.full((nk, tm), big, F32)
        _extract_topk([(c1_scr, r1_scr, v1_scr), (c2_scr, r2_scr, v2_scr)], kt)
        v1 = v1_scr[...]
        v2 = v2_scr[...]
        pieces = [v1[a:a + 1, :] + v2[0:kt // (a + 1), :] for a in range(kt)]
        pad = n_cand - sum(kt // (a + 1) for a in range(kt))
        pieces.append(jnp.full((pad, tm), -jnp.inf, F32))
        cand = jnp.concatenate(pieces, axis=0)
        cand_scr[...] = cand
        csel_scr[...] = jnp.full((n_cand, tm), big, F32)
        _extract_topk([(cand_scr, csel_scr, cval_scr)], kt)
        sel_f = jnp.where(csel_scr[...] < big, 1.0, 0.0)
        top = v1[0:1, :] + v2[0:1, :]
        z = jnp.sum(sel_f * jnp.exp(cand - top), axis=0, keepdims=True)
        r1 = r1_scr[...]
        n1 = jnp.zeros((nk, tm), F32)
        off = 0
        for a in range(kt):
            cnt = kt // (a + 1)
            n_a = jnp.sum(sel_f[off:off + cnt, :], axis=0, keepdims=True)
            off += cnt
            n1 = jnp.where(r1 == float(a), n_a, n1)
        e1_ref[h] = jnp.exp(s1 - v1[0:1, :])
        n1_ref[h] = n1
        e2_ref[h] = (0.5 * jnp.exp(s2 - v2[0:1, :])) / z
        r2_ref[h] = r2_scr[...]
        return carry

    lax.fori_loop(0, PEER_HEADS, head, 0)


def _peer_stats(x1, g, wq_h, keys_bf16, tm):
    t, d = x1.shape
    nk, kt = PEER_NKEYS, PEER_TOPK
    stat_spec = pl.BlockSpec((PEER_HEADS, nk, tm), lambda i: (0, 0, i))
    stat_shape = jax.ShapeDtypeStruct((PEER_HEADS, nk, t), F32)
    full = lambda a: pl.BlockSpec(a.shape, lambda i: tuple(0 for _ in a.shape))
    n_cells = sum(kt // (a + 1) for a in range(kt))
    n_cand = -(-n_cells // 8) * 8
    return pl.pallas_call(
        functools.partial(_peer_stats_kernel, tm=tm),
        grid=(t // tm,),
        in_specs=[pl.BlockSpec((tm, d), lambda i: (i, 0)), pl.BlockSpec((1, d), lambda i: (0, 0)),
                  full(wq_h), full(keys_bf16)],
        out_specs=[pl.BlockSpec((tm, d), lambda i: (i, 0))] + [stat_spec] * 4,
        out_shape=[jax.ShapeDtypeStruct((t, d), BF16)] + [stat_shape] * 4,
        scratch_shapes=[pltpu.VMEM((nk, tm), F32), pltpu.VMEM((nk, tm), F32),
                        pltpu.VMEM((nk, tm), F32), pltpu.VMEM((nk, tm), F32),
                        pltpu.VMEM((kt, tm), F32), pltpu.VMEM((kt, tm), F32),
                        pltpu.VMEM((n_cand, tm), F32), pltpu.VMEM((n_cand, tm), F32),
                        pltpu.VMEM((kt, tm), F32)],
        compiler_params=_cparams("parallel"),
        name="peer_stats",
    )(x1, g.reshape(1, d), wq_h, keys_bf16)


PEER_KEY_GROUP = 8


PEER_STAGES = 3


def _peer_dense_kernel(xn_ref, u_ref, vt_ref, e1_ref, n1_ref, e2_ref, r2_ref, o_ref, ht_scr, at_scr,
                       *, tm, eb, nb):
    nk = PEER_NKEYS
    s = pl.program_id(0)
    n_keys = eb // nk

    @pl.when(s == 0)
    def _():
        ht_scr[...] = jnp.zeros(ht_scr.shape, F32)
        at_scr[...] = jnp.zeros(at_scr.shape, BF16)

    @pl.when((s < PEER_STAGES) | ((s - (PEER_STAGES - 1)) % nb == 0))
    def _():
        o_ref[...] = jnp.zeros(o_ref.shape, F32)

    def step(slot):
        o_ref[...] += jnp.dot(vt_ref[...], at_scr[slot], preferred_element_type=F32)

        jb = jnp.maximum(s - 1, 0) % nb
        n1_rows = [[n1_ref[h, pl.ds(jb * n_keys + c, 1), :] for c in range(n_keys)]
                   for h in range(PEER_HEADS)]
        e1_rows = [[e1_ref[h, pl.ds(jb * n_keys + c, 1), :] for c in range(n_keys)]
                   for h in range(PEER_HEADS)]
        ht = ht_scr[1 - slot]
        at_scr[1 - slot] = (ht * (1.0 + lax.erf(ht * (1.0 / math.sqrt(2.0))))).astype(BF16)
        pk = BF16_ROWS
        for lt in range(tm // LANE):
            lanes = slice(lt * LANE, (lt + 1) * LANE)
            gates = []
            for c0 in range(0, n_keys, PEER_KEY_GROUP):
                grp = [jnp.zeros((nk // pk, pk, LANE), BF16) for _ in range(PEER_KEY_GROUP)]
                for h in range(PEER_HEADS):
                    r2 = r2_ref[h, :, lanes].astype(BF16).reshape(nk // pk, pk, LANE)
                    e2 = e2_ref[h, :, lanes].astype(BF16).reshape(nk // pk, pk, LANE)
                    for c in range(PEER_KEY_GROUP):
                        n1 = jnp.broadcast_to(n1_rows[h][c0 + c][:, lanes], (pk, LANE)).astype(BF16)
                        e1 = jnp.broadcast_to(e1_rows[h][c0 + c][:, lanes], (pk, LANE)).astype(BF16)
                        grp[c] = grp[c] + jnp.where(r2 < n1, e2, jnp.zeros_like(e2)) * e1
                gates.extend(g.reshape(nk, LANE) for g in grp)
            gate = jnp.concatenate(gates, axis=0)
            at_scr[1 - slot, :, lanes] = at_scr[1 - slot, :, lanes] * gate

        ht_scr[slot] = lax.dot_general(u_ref[...], xn_ref[...], _NT, preferred_element_type=F32)

    step(s % 2)


def _peer_dense(xn, u_bf16, vt_bf16, e1, n1, e2, r2, tm, eb):
    t, d = xn.shape
    n_exp = u_bf16.shape[0]
    nb = n_exp // eb
    n_items = (t // tm) * nb
    item = lambda s, lag: jnp.clip(s - lag, 0, n_items - 1)
    stat_spec = pl.BlockSpec((PEER_HEADS, PEER_NKEYS, tm), lambda s: (0, 0, item(s, 1) // nb),
                             pipeline_mode=pl.Buffered(1))
    return pl.pallas_call(
        functools.partial(_peer_dense_kernel, tm=tm, eb=eb, nb=nb),
        grid=(n_items + PEER_STAGES - 1,),
        in_specs=[pl.BlockSpec((tm, d), lambda s: (item(s, 0) // nb, 0)),
                  pl.BlockSpec((eb, d), lambda s: (item(s, 0) % nb, 0)),
                  pl.BlockSpec((None, d, eb), lambda s: (item(s, 2) % nb, 0, 0)),
                  stat_spec, stat_spec, stat_spec, stat_spec],
        out_specs=pl.BlockSpec((d, tm), lambda s: (0, item(s, 2) // nb)),
        out_shape=jax.ShapeDtypeStruct((d, t), F32),
        scratch_shapes=[pltpu.VMEM((2, eb, tm), F32), pltpu.VMEM((2, eb, tm), BF16)],
        compiler_params=_cparams("arbitrary"),
        name="peer_dense",
    )(xn, u_bf16, vt_bf16, e1, n1, e2, r2)


def _final_kernel(x1_ref, yt_ref, g_ref, o_ref):
    y = x1_ref[...] + yt_ref[...].T
    o_ref[...] = y * lax.rsqrt(jnp.mean(y * y, axis=-1, keepdims=True) + EPS) * g_ref[...]


def _final_norm(x1, yt, g_final, tm):
    t, d = x1.shape
    return pl.pallas_call(
        _final_kernel,
        grid=(t // tm,),
        in_specs=[pl.BlockSpec((tm, d), lambda i: (i, 0)), pl.BlockSpec((d, tm), lambda i: (0, i)),
                  pl.BlockSpec((1, d), lambda i: (0, 0))],
        out_specs=pl.BlockSpec((tm, d), lambda i: (i, 0)),
        out_shape=jax.ShapeDtypeStruct((t, d), F32),
        compiler_params=_cparams("parallel"),
        name="peer_residual_final_norm",
    )(x1, yt, g_final.reshape(1, d))


def _gate_expand_matrix():
    r = jnp.arange(GATE_PAD)[:, None]
    c = jnp.arange(3 * NSA_WIDTH)[None, :]
    return ((r < GATE_COLS) & (r == (c // NSA_WIDTH) * NSA_HEADS + (c % NSA_WIDTH) // NSA_HEAD_DIM)).astype(F32)


def kernel(x, g_mix, w_in, pe_cmp_k, pe_cmp_v, w_cmp_k1, w_cmp_k2, w_cmp_v1, w_cmp_v2, w_nsa_out, w_dw, b_dw,
           g_conv_ln, b_conv_ln, w_conv_out, w_o, g_ffn, w_peer_q, peer_sub_keys, peer_u, peer_v, g_final):
    b, s, d = x.shape
    t = b * s
    depth = w_in.shape[0]
    assert depth == 1, "the fused final norm assumes a single layer"
    assert s % 512 == 0 and d % LANE == 0
    x2 = x.reshape(t, d)
    wexp = _gate_expand_matrix().astype(BF16)
    l = 0
    gate_end = QKV_COLS + GATE_COLS
    w_pad = jnp.concatenate([w_in[l][:, :gate_end], jnp.zeros((d, GATE_PAD - GATE_COLS), F32),
                             w_in[l][:, gate_end:]], axis=1).astype(BF16)
    splits = (QKV_COLS, GATE_PAD, 2 * CONV_CH, 2 * d)
    qkv, gates, glu, merge = _norm_proj(x2, g_mix[l], w_pad, splits, tm=256)

    kcmp, vcmp = _compress(qkv, b, s, pe_cmp_k[l], pe_cmp_v[l], w_cmp_k1[l], w_cmp_k2[l],
                           w_cmp_v1[l], w_cmp_v2[l])
    o_cmp, sel = _cmp_attn(qkv, kcmp, vcmp, b, s, tq=256)
    kaug_slc, kaug_win = _attn_prep(qkv, b, s, ts=512)
    o_slc = _flash(qkv, kaug_slc, sel, b, s, "slc")
    o_win = _flash(qkv, kaug_win, None, b, s, "win")

    x1 = _mix(o_cmp, o_slc, o_win, gates, glu, merge, x2, wexp,
              w_nsa_out[l].astype(BF16), w_dw[l].reshape(CONV_WIDTH, CONV_CH), b_dw[l].reshape(1, -1),
              g_conv_ln[l].reshape(1, -1), b_conv_ln[l].reshape(1, -1),
              w_conv_out[l].astype(BF16), w_o[l].astype(BF16), s, tm=256)

    wq_h = w_peer_q[l].reshape(d, PEER_HEADS, PEER_QDIM).transpose(1, 0, 2).astype(BF16)
    xn, e1, n1, e2, r2 = _peer_stats(x1, g_ffn[l], wq_h, peer_sub_keys[l].astype(BF16), tm=512)
    eb = 1024
    n_exp = peer_v.shape[1]
    vt_blocks = peer_v[l].astype(BF16).reshape(n_exp // eb, eb, d).transpose(0, 2, 1)
    yt = _peer_dense(xn, peer_u[l].astype(BF16), vt_blocks, e1, n1, e2, r2, tm=1024, eb=eb)
    out = _final_norm(x1, yt, g_final, tm=512)
    return out.reshape(b, s, d)
```

```python
import functools
import math

import jax
import jax.numpy as jnp
from jax import lax
from jax.experimental import pallas as pl
from jax.experimental.pallas import tpu as pltpu

F32 = jnp.float32
BF16 = jnp.bfloat16

NSA_HEADS = 8
NSA_KV_GROUPS = 2
NSA_HPG = NSA_HEADS // NSA_KV_GROUPS
NSA_HEAD_DIM = 64
NSA_WIDTH = NSA_HEADS * NSA_HEAD_DIM
NSA_KV_WIDTH = NSA_KV_GROUPS * NSA_HEAD_DIM
CMP_BLOCK = 32
CMP_STRIDE = 16
SLC_BLOCK = 64
SLC_TOPN = 16
WINDOW = 512
CONV_CH = 512
CONV_WIDTH = 31
PEER_HEADS = 8
PEER_NKEYS = 128
PEER_QDIM = 256
PEER_HALF = PEER_QDIM // 2
PEER_TOPK = 16
EPS = 1e-6
NEG = -1e30
FORCED = 1e9
SLOPES = tuple(2.0 ** (-8.0 * (h + 1) / NSA_HEADS) for h in range(NSA_HEADS))
LOG2E = math.log2(math.e)

LANE = 128
BF16_ROWS = 16
GATE_COLS = 3 * NSA_HEADS
GATE_PAD = LANE
QKV_COLS = NSA_WIDTH + 6 * NSA_KV_WIDTH
VMEM_LIMIT = 56 * 1024 * 1024

_NT = (((1,), (1,)), ((), ()))
_TN = (((0,), (0,)), ((), ()))


def _cparams(*sem):
    return pltpu.CompilerParams(dimension_semantics=sem, vmem_limit_bytes=VMEM_LIMIT)


def _sigmoid(x):
    return 1.0 / (1.0 + jnp.exp(-x))


def _slope_pair(slope, lane, first):
    s = jnp.full(lane.shape, slope, F32)
    hi = s.astype(BF16).astype(F32)
    return jnp.where(lane == first, hi, jnp.where(lane == first + 1, s - hi, 0.0))


def _gelu(x):
    return 0.5 * x * (1.0 + lax.erf(x * (1.0 / math.sqrt(2.0))))


def _norm_proj_kernel(x_ref, g_ref, w_ref, *out_refs, splits):
    x = x_ref[...]
    xn = x * lax.rsqrt(jnp.mean(x * x, axis=-1, keepdims=True) + EPS) * g_ref[...]
    xb = xn.astype(BF16)
    off = 0
    for o_ref, n in zip(out_refs, splits):
        o_ref[...] = jnp.dot(xb, w_ref[:, off:off + n], preferred_element_type=F32)
        off += n


def _norm_proj(x2, g, w_bf16, splits, tm):
    t, d = x2.shape
    n = w_bf16.shape[1]
    return pl.pallas_call(
        functools.partial(_norm_proj_kernel, splits=splits),
        grid=(t // tm,),
        in_specs=[pl.BlockSpec((tm, d), lambda i: (i, 0)),
                  pl.BlockSpec((1, d), lambda i: (0, 0)),
                  pl.BlockSpec((d, n), lambda i: (0, 0))],
        out_specs=[pl.BlockSpec((tm, s), lambda i: (i, 0)) for s in splits],
        out_shape=[jax.ShapeDtypeStruct((t, s), F32) for s in splits],
        compiler_params=_cparams("parallel"),
        name="norm_in_proj",
    )(x2, g.reshape(1, d), w_bf16)


def _compress_kernel(k_ref, v_ref, pek_ref, pev_ref, wk1_ref, wk2_ref, wv1_ref, wv2_ref,
                     kc_ref, vc_ref, *, n_blk):
    hp = lax.Precision.HIGHEST
    dh = NSA_HEAD_DIM
    per = CMP_BLOCK // CMP_STRIDE
    assert per == 2

    def one(src_ref, pe_ref, w1_ref, w2_ref, dst_ref):
        pe8 = jnp.broadcast_to(pe_ref[...], (8, CMP_BLOCK * dh))
        const = jnp.dot(pe8, w1_ref[...], precision=hp, preferred_element_type=F32)[0:1, :]
        acc = [[jnp.zeros((n_blk, dh), F32) for _ in range(per)] for _ in range(NSA_KV_GROUPS)]
        for lo in range(CMP_STRIDE):
            rows = src_ref[pl.ds(lo, n_blk, stride=CMP_STRIDE), :]
            for g in range(NSA_KV_GROUPS):
                rg = rows[:, g * dh:(g + 1) * dh]
                for hi in range(per):
                    l = hi * CMP_STRIDE + lo
                    acc[g][hi] = acc[g][hi] + jnp.dot(rg, w1_ref[l * dh:(l + 1) * dh, :], precision=hp,
                                                     preferred_element_type=F32)
        for g in range(NSA_KV_GROUPS):
            pre = acc[g][0] + pltpu.roll(acc[g][1], n_blk - 1, 0) + const
            out = jnp.dot(_gelu(pre), w2_ref[...], precision=hp, preferred_element_type=F32)
            rid = lax.broadcasted_iota(jnp.int32, (n_blk, dh), 0)
            dst_ref[0, g] = jnp.where(rid < n_blk - 1, out, 0.0)

    one(k_ref, pek_ref, wk1_ref, wk2_ref, kc_ref)
    one(v_ref, pev_ref, wv1_ref, wv2_ref, vc_ref)


def _compress(qkv, b, s, pe_k, pe_v, wk1, wk2, wv1, wv2):
    n_blk = s // CMP_STRIDE
    dh = NSA_HEAD_DIM
    kcol = NSA_WIDTH // LANE
    full = lambda shape: pl.BlockSpec(shape, lambda i: tuple(0 for _ in shape))
    out_spec = pl.BlockSpec((1, NSA_KV_GROUPS, n_blk, dh), lambda i: (i, 0, 0, 0))
    out_shape = jax.ShapeDtypeStruct((b, NSA_KV_GROUPS, n_blk, dh), F32)
    return pl.pallas_call(
        functools.partial(_compress_kernel, n_blk=n_blk),
        grid=(b,),
        in_specs=[pl.BlockSpec((s, LANE), lambda i: (i, kcol)),
                  pl.BlockSpec((s, LANE), lambda i: (i, kcol + 1)),
                  full((1, CMP_BLOCK * dh)), full((1, CMP_BLOCK * dh)),
                  full((CMP_BLOCK * dh, dh)), full((dh, dh)),
                  full((CMP_BLOCK * dh, dh)), full((dh, dh))],
        out_specs=[out_spec, out_spec],
        out_shape=[out_shape, out_shape],
        compiler_params=_cparams("parallel"),
        name="nsa_compress",
    )(qkv, qkv, pe_k.reshape(1, -1), pe_v.reshape(1, -1), wk1, wk2, wv1, wv2)


def _cmp_attn_kernel(q_ref, kc_ref, vc_ref, o_ref, sel_ref, ot_scr, *, tq, n_blk, n_slc):
    hp = lax.Precision.HIGHEST
    dh = NSA_HEAD_DIM
    sub = 8
    rows = NSA_HPG * tq
    q0 = pl.program_id(1) * tq
    n_id = lax.broadcasted_iota(jnp.int32, (n_blk, rows), 0)
    t_id = q0 + lax.broadcasted_iota(jnp.int32, (n_blk, rows), 1) % tq
    mask = n_id * CMP_STRIDE + (CMP_BLOCK - 1) <= t_id
    jj = lax.broadcasted_iota(jnp.int32, (n_slc, n_blk), 0) * SLC_BLOCK
    nn = lax.broadcasted_iota(jnp.int32, (n_slc, n_blk), 1) * CMP_STRIDE
    overlap_t = jnp.where((nn <= jj + SLC_BLOCK - 1) & (nn + CMP_BLOCK - 1 >= jj), 1.0, 0.0)
    jb = lax.broadcasted_iota(jnp.int32, (n_slc, tq), 0)
    t_blk = (q0 + lax.broadcasted_iota(jnp.int32, (n_slc, tq), 1)) // SLC_BLOCK
    future = jb > t_blk
    forced = (jb == 0) | (jb == t_blk) | (jb == t_blk - 1)
    n_sel = min(SLC_TOPN, n_slc)
    lk = lax.broadcasted_iota(jnp.int32, (n_blk, dh), 1)
    nk_f = (lax.broadcasted_iota(jnp.int32, (n_blk, dh), 0) * CMP_STRIDE).astype(F32)
    cols_k = jnp.where(lk <= 1, nk_f, 0.0)
    lq = lax.broadcasted_iota(jnp.int32, (tq, dh), 1)

    for g in range(NSA_KV_GROUPS):
        kca = jnp.concatenate([kc_ref[0, g], cols_k], axis=1).astype(BF16)
        vc = vc_ref[0, g].astype(BF16)
        blocks = []
        for h in range(NSA_HPG):
            hh = g * NSA_HPG + h
            qh = q_ref[:, hh * dh:(hh + 1) * dh] * (LOG2E / math.sqrt(dh))
            blocks.append(jnp.concatenate([qh, _slope_pair(LOG2E * SLOPES[hh], lq, 0)], axis=1))
        qa = jnp.concatenate(blocks, axis=0).astype(BF16)
        st = lax.dot_general(kca, qa, _NT, preferred_element_type=F32)
        st = jnp.where(mask, st, NEG)
        m = jnp.max(st, axis=0, keepdims=True)
        e = jnp.where(mask, jnp.exp2(st - m), 0.0)
        l = jnp.sum(e, axis=0, keepdims=True)
        p = e / jnp.where(l > 0.0, l, 1.0)
        ot_scr[g * dh:(g + 1) * dh, :] = lax.dot_general(vc, p.astype(BF16), _TN, preferred_element_type=F32)
        psum = p[:, 0:tq]
        for h in range(1, NSA_HPG):
            psum = psum + p[:, h * tq:(h + 1) * tq]
        imp = jnp.dot(overlap_t, psum, precision=hp, preferred_element_type=F32)
        imp = jnp.where(forced, FORCED, jnp.where(future, NEG, imp))
        slabs = [imp[v * sub:(v + 1) * sub, :] for v in range(n_slc // sub)]
        cnts = [jnp.zeros((sub, tq), F32) for _ in slabs]
        jrow = lax.broadcasted_iota(jnp.int32, (sub, tq), 0)
        for j2 in range(n_slc):
            row = imp[j2:j2 + 1, :]
            for v, slab in enumerate(slabs):
                if v * sub > j2:
                    before = row >= slab
                elif v * sub + sub - 1 <= j2:
                    before = row > slab
                else:
                    before = jnp.where(jrow + v * sub > j2, jnp.where(row >= slab, 1.0, 0.0),
                                       jnp.where(row > slab, 1.0, 0.0)) > 0.5
                cnts[v] = cnts[v] + jnp.where(before, 1.0, 0.0)
        sel_t = jnp.where(jnp.concatenate(cnts, axis=0) < float(n_sel), 1.0, 0.0)
        if n_slc < LANE:
            sel_t = jnp.concatenate([sel_t, jnp.zeros((LANE - n_slc, tq), F32)], axis=0)
        sel_ref[g] = sel_t.T[:, :n_slc]

    out = ot_scr[...].T
    for g in range(NSA_KV_GROUPS):
        for h in range(NSA_HPG):
            hh = g * NSA_HPG + h
            o_ref[:, hh * dh:(hh + 1) * dh] = out[h * tq:(h + 1) * tq, g * dh:(g + 1) * dh]


def _cmp_attn(qkv, kcmp, vcmp, b, s, tq):
    t = b * s
    nq = s // tq
    n_blk = kcmp.shape[2]
    n_slc = s // SLC_BLOCK
    cmp_spec = pl.BlockSpec((1, NSA_KV_GROUPS, n_blk, NSA_HEAD_DIM), lambda bi, i: (bi, 0, 0, 0))
    return pl.pallas_call(
        functools.partial(_cmp_attn_kernel, tq=tq, n_blk=n_blk, n_slc=n_slc),
        grid=(b, nq),
        in_specs=[pl.BlockSpec((tq, NSA_WIDTH), lambda bi, i: (bi * nq + i, 0)), cmp_spec, cmp_spec],
        out_specs=[pl.BlockSpec((tq, NSA_WIDTH), lambda bi, i: (bi * nq + i, 0)),
                   pl.BlockSpec((NSA_KV_GROUPS, tq, n_slc), lambda bi, i: (0, bi * nq + i, 0))],
        out_shape=[jax.ShapeDtypeStruct((t, NSA_WIDTH), F32),
                   jax.ShapeDtypeStruct((NSA_KV_GROUPS, t, n_slc), F32)],
        scratch_shapes=[pltpu.VMEM((NSA_KV_GROUPS * NSA_HEAD_DIM, NSA_HPG * tq), F32)],
        compiler_params=_cparams("parallel", "parallel"),
        name="nsa_cmp_attn_select",
    )(qkv, kcmp, vcmp)


ATT_TILE = 256
AUG_SLC = 2 * LANE
AUG_WIN = LANE


def _attn_prep_kernel(ks_ref, kw_ref, kas_ref, kaw_ref, *, ts):
    dh = NSA_HEAD_DIM
    s0 = pl.program_id(1) * ts
    pos = s0 + lax.broadcasted_iota(jnp.int32, (ts, dh), 0)
    lane = lax.broadcasted_iota(jnp.int32, (ts, dh), 1)
    onehot = jnp.where(lane == pos // SLC_BLOCK, 1.0, 0.0)
    off = (pos % ATT_TILE).astype(F32)
    cols = jnp.where(lane == 0, 1.0, jnp.where(lane <= 2, off, 0.0))
    zeros = jnp.zeros((ts, dh), F32)
    for g in range(NSA_KV_GROUPS):
        ks = ks_ref[:, g * dh:(g + 1) * dh]
        kas_ref[0, g] = jnp.concatenate([ks, onehot, cols, zeros], axis=1).astype(BF16)
        kw = kw_ref[:, g * dh:(g + 1) * dh]
        kaw_ref[0, g] = jnp.concatenate([kw, cols], axis=1).astype(BF16)


def _attn_prep(qkv, b, s, ts):
    kcol = NSA_WIDTH // LANE + 2
    return pl.pallas_call(
        functools.partial(_attn_prep_kernel, ts=ts),
        grid=(b, s // ts),
        in_specs=[pl.BlockSpec((ts, LANE), lambda bi, i: (bi * (s // ts) + i, kcol)),
                  pl.BlockSpec((ts, LANE), lambda bi, i: (bi * (s // ts) + i, kcol + 2))],
        out_specs=[pl.BlockSpec((1, NSA_KV_GROUPS, ts, AUG_SLC), lambda bi, i: (bi, 0, i, 0)),
                   pl.BlockSpec((1, NSA_KV_GROUPS, ts, AUG_WIN), lambda bi, i: (bi, 0, i, 0))],
        out_shape=[jax.ShapeDtypeStruct((b, NSA_KV_GROUPS, s, AUG_SLC), BF16),
                   jax.ShapeDtypeStruct((b, NSA_KV_GROUPS, s, AUG_WIN), BF16)],
        compiler_params=_cparams("parallel", "parallel"),
        name="nsa_key_augment",
    )(qkv, qkv)


def _flash_kernel(*refs, mode, tq):
    if mode == "slc":
        q_ref, ka_ref, v_ref, sel_ref, o_ref, qa_scr, m_scr, l_scr, acc_scr = refs
    else:
        q_ref, ka_ref, v_ref, o_ref, qa_scr, m_scr, l_scr, acc_scr = refs
        sel_ref = None
    dh = NSA_HEAD_DIM
    tk = tq
    rows = NSA_HPG * tq
    i = pl.program_id(1)
    q0 = i * tq
    qid = lax.broadcasted_iota(jnp.int32, (tk, rows), 1) % tq
    kid = lax.broadcasted_iota(jnp.int32, (tk, rows), 0)
    head_of_row = lax.broadcasted_iota(jnp.int32, (1, rows), 1) // tq
    rq = lax.broadcasted_iota(jnp.int32, (tq, dh), 0).astype(F32)
    lq = lax.broadcasted_iota(jnp.int32, (tq, dh), 1)

    slope_rows = []
    for g in range(NSA_KV_GROUPS):
        slopes = [LOG2E * SLOPES[g * NSA_HPG + h] for h in range(NSA_HPG)]
        slope_row = jnp.full((1, rows), slopes[-1], F32)
        for h in range(NSA_HPG - 2, -1, -1):
            slope_row = jnp.where(head_of_row == h, slopes[h], slope_row)
        slope_rows.append(slope_row)
        blocks = []
        for h in range(NSA_HPG):
            hh = g * NSA_HPG + h
            qh = q_ref[:, hh * dh:(hh + 1) * dh] * (LOG2E / math.sqrt(dh))
            cols = jnp.where(lq == 0, -slopes[h] * rq, _slope_pair(slopes[h], lq, 1))
            if mode == "slc":
                n_slc = sel_ref.shape[-1]
                notsel = (sel_ref[g] - 1.0) * (-NEG)
                if n_slc < dh:
                    notsel = jnp.concatenate([notsel, jnp.zeros((tq, dh - n_slc), F32)], axis=1)
                parts = [qh, notsel, cols, jnp.zeros((tq, dh), F32)]
            else:
                parts = [qh, cols]
            blocks.append(jnp.concatenate(parts, axis=1))
        qa_scr[g] = jnp.concatenate(blocks, axis=0).astype(BF16)
    m_scr[...] = jnp.full(m_scr.shape, NEG, F32)
    l_scr[...] = jnp.zeros(l_scr.shape, F32)
    acc_scr[...] = jnp.zeros(acc_scr.shape, F32)

    def tiles(specs):
        for g in range(NSA_KV_GROUPS):
            sts, shifts, vgs = [], [], []
            for kt, mask in specs:
                k0 = pl.multiple_of(kt * tk, tk)
                ka = ka_ref[0, g, pl.ds(k0, tk), :]
                vgs.append(v_ref[pl.ds(k0, tk), g * dh:(g + 1) * dh].astype(BF16))
                st = lax.dot_general(ka, qa_scr[g], _NT, preferred_element_type=F32)
                sts.append(st if mask is None else jnp.where(mask, st, NEG))
                shifts.append(slope_rows[g] * jnp.asarray(k0 - q0, F32))
            m_old = m_scr[g]
            m_new = m_old
            for st, shift in zip(sts, shifts):
                m_new = jnp.maximum(m_new, jnp.max(st, axis=0, keepdims=True) + shift)
            alpha = jnp.exp2(m_old - m_new)
            l_new = alpha * l_scr[g]
            acc = alpha * acc_scr[g * dh:(g + 1) * dh, :]
            for st, shift, vg in zip(sts, shifts, vgs):
                p = jnp.exp2(st - (m_new - shift))
                l_new = l_new + jnp.sum(p, axis=0, keepdims=True)
                acc = acc + lax.dot_general(vg, p.astype(BF16), _TN, preferred_element_type=F32)
            l_scr[g] = l_new
            acc_scr[g * dh:(g + 1) * dh, :] = acc
            m_scr[g] = m_new

    diag = kid <= qid
    if mode == "slc":
        def body(k2, carry):
            tiles([(2 * k2, None), (2 * k2 + 1, None)])
            return carry
        lax.fori_loop(0, i // 2, body, 0)

        @pl.when(i % 2 == 1)
        def _():
            tiles([(i - 1, None), (i, diag)])

        @pl.when(i % 2 == 0)
        def _():
            tiles([(i, diag)])
    else:
        @pl.when(i == 0)
        def _():
            tiles([(i, diag)])

        @pl.when(i == 1)
        def _():
            tiles([(i, diag), (i - 1, None)])

        @pl.when(i >= 2)
        def _():
            tiles([(i, diag), (i - 1, None), (i - 2, kid > qid)])

    for g in range(NSA_KV_GROUPS):
        acc_scr[g * dh:(g + 1) * dh, :] = acc_scr[g * dh:(g + 1) * dh, :] / l_scr[g]

    out = acc_scr[...].T
    for g in range(NSA_KV_GROUPS):
        for h in range(NSA_HPG):
            hh = g * NSA_HPG + h
            o_ref[:, hh * dh:(hh + 1) * dh] = out[h * tq:(h + 1) * tq, g * dh:(g + 1) * dh]


def _flash(qkv, kaug, sel, b, s, mode):
    tq = ATT_TILE
    assert WINDOW == 2 * tq and s % tq == 0
    t = b * s
    nq = s // tq
    aug = kaug.shape[-1]
    vcol = NSA_WIDTH // LANE + {"slc": 3, "win": 5}[mode]
    in_specs = [pl.BlockSpec((tq, NSA_WIDTH), lambda bi, i: (bi * nq + i, 0)),
                pl.BlockSpec((1, NSA_KV_GROUPS, s, aug), lambda bi, i: (bi, 0, 0, 0)),
                pl.BlockSpec((s, LANE), lambda bi, i: (bi, vcol))]
    args = [qkv, kaug, qkv]
    if mode == "slc":
        assert sel.shape[-1] <= NSA_HEAD_DIM
        in_specs.append(pl.BlockSpec((NSA_KV_GROUPS, tq, sel.shape[-1]), lambda bi, i: (0, bi * nq + i, 0)))
        args.append(sel)
    rows = NSA_HPG * tq
    return pl.pallas_call(
        functools.partial(_flash_kernel, mode=mode, tq=tq),
        grid=(b, nq),
        in_specs=in_specs,
        out_specs=pl.BlockSpec((tq, NSA_WIDTH), lambda bi, i: (bi * nq + i, 0)),
        out_shape=jax.ShapeDtypeStruct((t, NSA_WIDTH), F32),
        scratch_shapes=[pltpu.VMEM((NSA_KV_GROUPS, rows, aug), BF16),
                        pltpu.VMEM((NSA_KV_GROUPS, 1, rows), F32),
                        pltpu.VMEM((NSA_KV_GROUPS, 1, rows), F32),
                        pltpu.VMEM((NSA_KV_GROUPS * NSA_HEAD_DIM, rows), F32)],
        compiler_params=_cparams("parallel", "parallel"),
        name="nsa_flash_" + mode,
    )(*args)


HALO = 32
CONV_SHIFTS = 8


def _mix_kernel(ocmp_ref, oslc_ref, owin_ref, gates_ref, glu_ref, halo_ref, merge_ref, x_ref,
                wexp_ref, wnsa_ref, wdw_ref, bdw_ref, gln_ref, bln_ref, wconv_ref, wo_ref,
                o_ref, uext_scr, *, tm, tiles_per_seq):
    i = pl.program_id(0)
    gts = _sigmoid(gates_ref[...])
    g_hi = gts.astype(BF16)
    r1 = gts - g_hi.astype(F32)
    g_mid = r1.astype(BF16)
    g_lo = (r1 - g_mid.astype(F32)).astype(BF16)
    wexp = wexp_ref[...]
    gexp = (jnp.dot(g_hi, wexp, preferred_element_type=F32) + jnp.dot(g_mid, wexp, preferred_element_type=F32)
            + jnp.dot(g_lo, wexp, preferred_element_type=F32))
    w = NSA_WIDTH
    o_nsa = gexp[:, :w] * ocmp_ref[...] + gexp[:, w:2 * w] * oslc_ref[...] + gexp[:, 2 * w:] * owin_ref[...]
    y_a = jnp.dot(o_nsa.astype(BF16), wnsa_ref[...], preferred_element_type=F32)

    c = CONV_CH
    gl = glu_ref[...]
    u = gl[:, :c] * _sigmoid(gl[:, c:])
    hl = halo_ref[...]
    uh = hl[:, :c] * _sigmoid(hl[:, c:])
    uh = jnp.where(i % tiles_per_seq == 0, 0.0, uh)
    uext_scr[0, 0:HALO, :] = uh
    uext_scr[0, HALO:HALO + tm, :] = u
    span = HALO + tm - CONV_SHIFTS
    for j in range(1, CONV_SHIFTS):
        uext_scr[j, 0:span, :] = uext_scr[0, pl.ds(j, span), :]
    acc = jnp.zeros((tm, c), F32)
    for k in range(CONV_WIDTH):
        first = HALO - (CONV_WIDTH - 1) + k
        acc = acc + uext_scr[first % CONV_SHIFTS, pl.ds(first - first % CONV_SHIFTS, tm), :] * wdw_ref[k:k + 1, :]
    cv = acc + bdw_ref[...]
    mu = jnp.mean(cv, axis=-1, keepdims=True)
    var = jnp.mean(jnp.square(cv - mu), axis=-1, keepdims=True)
    un = (cv - mu) * lax.rsqrt(var + EPS) * gln_ref[...] + bln_ref[...]
    act = un * _sigmoid(un)
    y_b = jnp.dot(act.astype(BF16), wconv_ref[...], preferred_element_type=F32)

    d = x_ref.shape[-1]
    mg = merge_ref[...]
    z = _sigmoid(mg[:, :d]) * y_a + _sigmoid(mg[:, d:]) * y_b
    o_ref[...] = x_ref[...] + jnp.dot(z.astype(BF16), wo_ref[...], preferred_element_type=F32)


def _mix(ocmp, oslc, owin, gates, glu, merge, x2, wexp, wnsa, wdw, bdw, gln, bln, wconv, wo, s, tm):
    t, d = x2.shape
    row = lambda n: pl.BlockSpec((tm, n), lambda i: (i, 0))
    full = lambda a: pl.BlockSpec(a.shape, lambda i: tuple(0 for _ in a.shape))
    halo_spec = pl.BlockSpec((HALO, glu.shape[1]), lambda i: (jnp.maximum(i * (tm // HALO) - 1, 0), 0))
    weights = [wexp, wnsa, wdw, bdw, gln, bln, wconv, wo]
    return pl.pallas_call(
        functools.partial(_mix_kernel, tm=tm, tiles_per_seq=s // tm),
        grid=(t // tm,),
        in_specs=[row(NSA_WIDTH), row(NSA_WIDTH), row(NSA_WIDTH), row(GATE_PAD), row(glu.shape[1]),
                  halo_spec, row(merge.shape[1]), row(d)] + [full(a) for a in weights],
        out_specs=row(d),
        out_shape=jax.ShapeDtypeStruct((t, d), F32),
        scratch_shapes=[pltpu.VMEM((CONV_SHIFTS, HALO + tm, CONV_CH), F32)],
        compiler_params=_cparams("parallel"),
        name="mixer_merge",
    )(ocmp, oslc, owin, gates, glu, glu, merge, x2, *weights)


def _extract_topk(jobs, k_top):
    def body(k, carry):
        for cur_ref, rank_ref, val_ref in jobs:
            cur = cur_ref[...]
            n_rows = cur.shape[0]
            rid = lax.broadcasted_iota(jnp.int32, cur.shape, 0).astype(F32)
            v = jnp.max(cur, axis=0, keepdims=True)
            idx = jnp.min(jnp.where(cur == v, rid, float(n_rows)), axis=0, keepdims=True)
            hit = rid == idx
            rank_ref[...] = jnp.where(hit, jnp.asarray(k, F32), rank_ref[...])
            cur_ref[...] = jnp.where(hit, -jnp.inf, cur)
            val_ref[pl.ds(k, 1), :] = v
        return carry

    lax.fori_loop(0, k_top, body, 0)


def _peer_stats_kernel(x_ref, g_ref, wq_ref, keys_ref, xn_ref, e1_ref, n1_ref, e2_ref, r2_ref,
                       c1_scr, c2_scr, r1_scr, r2_scr, v1_scr, v2_scr, cand_scr, csel_scr, cval_scr, *, tm):
    nk = PEER_NKEYS
    kt = PEER_TOPK
    x = x_ref[...]
    xn = (x * lax.rsqrt(jnp.mean(x * x, axis=-1, keepdims=True) + EPS) * g_ref[...]).astype(BF16)
    xn_ref[...] = xn
    big = float(nk)
    n_cand = cand_scr.shape[0]

    def head(h, carry):
        qp = jnp.dot(xn, wq_ref[h], preferred_element_type=F32)
        s1 = lax.dot_general(keys_ref[h, 0], qp[:, :PEER_HALF].astype(BF16), _NT,
                             preferred_element_type=F32)
        s2 = lax.dot_general(keys_ref[h, 1], qp[:, PEER_HALF:].astype(BF16), _NT,
                             preferred_element_type=F32)
        c1_scr[...] = s1
        c2_scr[...] = s2
        r1_scr[...] = jnp.full((nk, tm), big, F32)
        r2_scr[...] = jnp.full((nk, tm), big, F32)
        _extract_topk([(c1_scr, r1_scr, v1_scr), (c2_scr, r2_scr, v2_scr)], kt)
        v1 = v1_scr[...]
        v2 = v2_scr[...]
        pieces = [v1[a:a + 1, :] + v2[0:kt // (a + 1), :] for a in range(kt)]
        pad = n_cand - sum(kt // (a + 1) for a in range(kt))
        pieces.append(jnp.full((pad, tm), -jnp.inf, F32))
        cand = jnp.concatenate(pieces, axis=0)
        cand_scr[...] = cand
        csel_scr[...] = jnp.full((n_cand, tm), big, F32)
        _extract_topk([(cand_scr, csel_scr, cval_scr)], kt)
        sel_f = jnp.where(csel_scr[...] < big, 1.0, 0.0)
        top = v1[0:1, :] + v2[0:1, :]
        z = jnp.sum(sel_f * jnp.exp(cand - top), axis=0, keepdims=True)
        r1 = r1_scr[...]
        n1 = jnp.zeros((nk, tm), F32)
        off = 0
        for a in range(kt):
            cnt = kt // (a + 1)
            n_a = jnp.sum(sel_f[off:off + cnt, :], axis=0, keepdims=True)
            off += cnt
            n1 = jnp.where(r1 == float(a), n_a, n1)
        e1_ref[h] = jnp.exp(s1 - v1[0:1, :])
        n1_ref[h] = n1
        e2_ref[h] = (0.5 * jnp.exp(s2 - v2[0:1, :])) / z
        r2_ref[h] = r2_scr[...]
        return carry

    lax.fori_loop(0, PEER_HEADS, head, 0)


def _peer_stats(x1, g, wq_h, keys_bf16, tm):
    t, d = x1.shape
    nk, kt = PEER_NKEYS, PEER_TOPK
    stat_spec = pl.BlockSpec((PEER_HEADS, nk, tm), lambda i: (0, 0, i))
    stat_shape = jax.ShapeDtypeStruct((PEER_HEADS, nk, t), F32)
    full = lambda a: pl.BlockSpec(a.shape, lambda i: tuple(0 for _ in a.shape))
    n_cells = sum(kt // (a + 1) for a in range(kt))
    n_cand = -(-n_cells // 8) * 8
    return pl.pallas_call(
        functools.partial(_peer_stats_kernel, tm=tm),
        grid=(t // tm,),
        in_specs=[pl.BlockSpec((tm, d), lambda i: (i, 0)), pl.BlockSpec((1, d), lambda i: (0, 0)),
                  full(wq_h), full(keys_bf16)],
        out_specs=[pl.BlockSpec((tm, d), lambda i: (i, 0))] + [stat_spec] * 4,
        out_shape=[jax.ShapeDtypeStruct((t, d), BF16)] + [stat_shape] * 4,
        scratch_shapes=[pltpu.VMEM((nk, tm), F32), pltpu.VMEM((nk, tm), F32),
                        pltpu.VMEM((nk, tm), F32), pltpu.VMEM((nk, tm), F32),
                        pltpu.VMEM((kt, tm), F32), pltpu.VMEM((kt, tm), F32),
                        pltpu.VMEM((n_cand, tm), F32), pltpu.VMEM((n_cand, tm), F32),
                        pltpu.VMEM((kt, tm), F32)],
        compiler_params=_cparams("parallel"),
        name="peer_stats",
    )(x1, g.reshape(1, d), wq_h, keys_bf16)


PEER_KEY_GROUP = 8


PEER_STAGES = 3


def _peer_dense_kernel(xn_ref, u_ref, vt_ref, e1_ref, n1_ref, e2_ref, r2_ref, o_ref, ht_scr, at_scr,
                       *, tm, eb, nb):
    nk = PEER_NKEYS
    s = pl.program_id(0)
    n_keys = eb // nk

    @pl.when(s == 0)
    def _():
        ht_scr[...] = jnp.zeros(ht_scr.shape, F32)
        at_scr[...] = jnp.zeros(at_scr.shape, BF16)

    @pl.when((s < PEER_STAGES) | ((s - (PEER_STAGES - 1)) % nb == 0))
    def _():
        o_ref[...] = jnp.zeros(o_ref.shape, F32)

    def step(slot):
        o_ref[...] += jnp.dot(vt_ref[...], at_scr[slot], preferred_element_type=F32)

        jb = jnp.maximum(s - 1, 0) % nb
        n1_rows = [[n1_ref[h, pl.ds(jb * n_keys + c, 1), :] for c in range(n_keys)]
                   for h in range(PEER_HEADS)]
        e1_rows = [[e1_ref[h, pl.ds(jb * n_keys + c, 1), :] for c in range(n_keys)]
                   for h in range(PEER_HEADS)]
        ht = ht_scr[1 - slot].astype(BF16)
        at_scr[1 - slot] = ht * (1.0 + lax.erf(ht * (1.0 / math.sqrt(2.0))))
        pk = BF16_ROWS
        for lt in range(tm // LANE):
            lanes = slice(lt * LANE, (lt + 1) * LANE)
            gates = []
            for c0 in range(0, n_keys, PEER_KEY_GROUP):
                grp = [jnp.zeros((nk // pk, pk, LANE), BF16) for _ in range(PEER_KEY_GROUP)]
                for h in range(PEER_HEADS):
                    r2 = r2_ref[h, :, lanes].astype(BF16).reshape(nk // pk, pk, LANE)
                    e2 = e2_ref[h, :, lanes].astype(BF16).reshape(nk // pk, pk, LANE)
                    for c in range(PEER_KEY_GROUP):
                        n1 = jnp.broadcast_to(n1_rows[h][c0 + c][:, lanes], (pk, LANE)).astype(BF16)
                        e1 = jnp.broadcast_to(e1_rows[h][c0 + c][:, lanes], (pk, LANE)).astype(BF16)
                        grp[c] = grp[c] + jnp.where(r2 < n1, e2, jnp.zeros_like(e2)) * e1
                gates.extend(g.reshape(nk, LANE) for g in grp)
            gate = jnp.concatenate(gates, axis=0)
            at_scr[1 - slot, :, lanes] = at_scr[1 - slot, :, lanes] * gate

        ht_scr[slot] = lax.dot_general(u_ref[...], xn_ref[...], _NT, preferred_element_type=F32)

    step(s % 2)


def _peer_dense(xn, u_bf16, vt_bf16, e1, n1, e2, r2, tm, eb):
    t, d = xn.shape
    n_exp = u_bf16.shape[0]
    nb = n_exp // eb
    n_items = (t // tm) * nb
    item = lambda s, lag: jnp.clip(s - lag, 0, n_items - 1)
    stat_spec = pl.BlockSpec((PEER_HEADS, PEER_NKEYS, tm), lambda s: (0, 0, item(s, 1) // nb),
                             pipeline_mode=pl.Buffered(1))
    return pl.pallas_call(
        functools.partial(_peer_dense_kernel, tm=tm, eb=eb, nb=nb),
        grid=(n_items + PEER_STAGES - 1,),
        in_specs=[pl.BlockSpec((tm, d), lambda s: (item(s, 0) // nb, 0)),
                  pl.BlockSpec((eb, d), lambda s: (item(s, 0) % nb, 0)),
                  pl.BlockSpec((None, d, eb), lambda s: (item(s, 2) % nb, 0, 0)),
                  stat_spec, stat_spec, stat_spec, stat_spec],
        out_specs=pl.BlockSpec((d, tm), lambda s: (0, item(s, 2) // nb)),
        out_shape=jax.ShapeDtypeStruct((d, t), F32),
        scratch_shapes=[pltpu.VMEM((2, eb, tm), F32), pltpu.VMEM((2, eb, tm), BF16)],
        compiler_params=_cparams("arbitrary"),
        name="peer_dense",
    )(xn, u_bf16, vt_bf16, e1, n1, e2, r2)


def _final_kernel(x1_ref, yt_ref, g_ref, o_ref):
    y = x1_ref[...] + yt_ref[...].T
    o_ref[...] = y * lax.rsqrt(jnp.mean(y * y, axis=-1, keepdims=True) + EPS) * g_ref[...]


def _final_norm(x1, yt, g_final, tm):
    t, d = x1.shape
    return pl.pallas_call(
        _final_kernel,
        grid=(t // tm,),
        in_specs=[pl.BlockSpec((tm, d), lambda i: (i, 0)), pl.BlockSpec((d, tm), lambda i: (0, i)),
                  pl.BlockSpec((1, d), lambda i: (0, 0))],
        out_specs=pl.BlockSpec((tm, d), lambda i: (i, 0)),
        out_shape=jax.ShapeDtypeStruct((t, d), F32),
        compiler_params=_cparams("parallel"),
        name="peer_residual_final_norm",
    )(x1, yt, g_final.reshape(1, d))


def _gate_expand_matrix():
    r = jnp.arange(GATE_PAD)[:, None]
    c = jnp.arange(3 * NSA_WIDTH)[None, :]
    return ((r < GATE_COLS) & (r == (c // NSA_WIDTH) * NSA_HEADS + (c % NSA_WIDTH) // NSA_HEAD_DIM)).astype(F32)


def kernel(x, g_mix, w_in, pe_cmp_k, pe_cmp_v, w_cmp_k1, w_cmp_k2, w_cmp_v1, w_cmp_v2, w_nsa_out, w_dw, b_dw,
           g_conv_ln, b_conv_ln, w_conv_out, w_o, g_ffn, w_peer_q, peer_sub_keys, peer_u, peer_v, g_final):
    b, s, d = x.shape
    t = b * s
    depth = w_in.shape[0]
    assert depth == 1, "the fused final norm assumes a single layer"
    assert s % 512 == 0 and d % LANE == 0
    x2 = x.reshape(t, d)
    wexp = _gate_expand_matrix().astype(BF16)
    l = 0
    gate_end = QKV_COLS + GATE_COLS
    w_pad = jnp.concatenate([w_in[l][:, :gate_end], jnp.zeros((d, GATE_PAD - GATE_COLS), F32),
                             w_in[l][:, gate_end:]], axis=1).astype(BF16)
    splits = (QKV_COLS, GATE_PAD, 2 * CONV_CH, 2 * d)
    qkv, gates, glu, merge = _norm_proj(x2, g_mix[l], w_pad, splits, tm=256)

    kcmp, vcmp = _compress(qkv, b, s, pe_cmp_k[l], pe_cmp_v[l], w_cmp_k1[l], w_cmp_k2[l],
                           w_cmp_v1[l], w_cmp_v2[l])
    o_cmp, sel = _cmp_attn(qkv, kcmp, vcmp, b, s, tq=256)
    kaug_slc, kaug_win = _attn_prep(qkv, b, s, ts=512)
    o_slc = _flash(qkv, kaug_slc, sel, b, s, "slc")
    o_win = _flash(qkv, kaug_win, None, b, s, "win")

    x1 = _mix(o_cmp, o_slc, o_win, gates, glu, merge, x2, wexp,
              w_nsa_out[l].astype(BF16), w_dw[l].reshape(CONV_WIDTH, CONV_CH), b_dw[l].reshape(1, -1),
              g_conv_ln[l].reshape(1, -1), b_conv_ln[l].reshape(1, -1),
              w_conv_out[l].astype(BF16), w_o[l].astype(BF16), s, tm=256)

    wq_h = w_peer_q[l].reshape(d, PEER_HEADS, PEER_QDIM).transpose(1, 0, 2).astype(BF16)
    xn, e1, n1, e2, r2 = _peer_stats(x1, g_ffn[l], wq_h, peer_sub_keys[l].astype(BF16), tm=512)
    eb = 1024
    n_exp = peer_v.shape[1]
    vt_blocks = peer_v[l].astype(BF16).reshape(n_exp // eb, eb, d).transpose(0, 2, 1)
    yt = _peer_dense(xn, peer_u[l].astype(BF16), vt_blocks, e1, n1, e2, r2, tm=1024, eb=eb)
    out = _final_norm(x1, yt, g_final, tm=512)
    return out.reshape(b, s, d)
```

```python
import functools
import math

import jax
import jax.numpy as jnp
from jax import lax
from jax.experimental import pallas as pl
from jax.experimental.pallas import tpu as pltpu

F32 = jnp.float32
BF16 = jnp.bfloat16

NSA_HEADS = 8
NSA_KV_GROUPS = 2
NSA_HPG = NSA_HEADS // NSA_KV_GROUPS
NSA_HEAD_DIM = 64
NSA_WIDTH = NSA_HEADS * NSA_HEAD_DIM
NSA_KV_WIDTH = NSA_KV_GROUPS * NSA_HEAD_DIM
CMP_BLOCK = 32
CMP_STRIDE = 16
SLC_BLOCK = 64
SLC_TOPN = 16
WINDOW = 512
CONV_CH = 512
CONV_WIDTH = 31
PEER_HEADS = 8
PEER_NKEYS = 128
PEER_QDIM = 256
PEER_HALF = PEER_QDIM // 2
PEER_TOPK = 16
EPS = 1e-6
NEG = -1e30
FORCED = 1e9
SLOPES = tuple(2.0 ** (-8.0 * (h + 1) / NSA_HEADS) for h in range(NSA_HEADS))
LOG2E = math.log2(math.e)

LANE = 128
BF16_ROWS = 16
GATE_COLS = 3 * NSA_HEADS
GATE_PAD = LANE
QKV_COLS = NSA_WIDTH + 6 * NSA_KV_WIDTH
VMEM_LIMIT = 56 * 1024 * 1024

_NT = (((1,), (1,)), ((), ()))
_TN = (((0,), (0,)), ((), ()))


def _cparams(*sem):
    return pltpu.CompilerParams(dimension_semantics=sem, vmem_limit_bytes=VMEM_LIMIT)


def _sigmoid(x):
    return 1.0 / (1.0 + jnp.exp(-x))


def _slope_pair(slope, lane, first):
    s = jnp.full(lane.shape, slope, F32)
    hi = s.astype(BF16).astype(F32)
    return jnp.where(lane == first, hi, jnp.where(lane == first + 1, s - hi, 0.0))


def _gelu(x):
    return 0.5 * x * (1.0 + lax.erf(x * (1.0 / math.sqrt(2.0))))


def _norm_proj_kernel(x_ref, g_ref, w_ref, *out_refs, splits):
    x = x_ref[...]
    xn = x * lax.rsqrt(jnp.mean(x * x, axis=-1, keepdims=True) + EPS) * g_ref[...]
    xb = xn.astype(BF16)
    off = 0
    for o_ref, n in zip(out_refs, splits):
        o_ref[...] = jnp.dot(xb, w_ref[:, off:off + n], preferred_element_type=F32)
        off += n


def _norm_proj(x2, g, w_bf16, splits, tm):
    t, d = x2.shape
    n = w_bf16.shape[1]
    return pl.pallas_call(
        functools.partial(_norm_proj_kernel, splits=splits),
        grid=(t // tm,),
        in_specs=[pl.BlockSpec((tm, d), lambda i: (i, 0)),
                  pl.BlockSpec((1, d), lambda i: (0, 0)),
                  pl.BlockSpec((d, n), lambda i: (0, 0))],
        out_specs=[pl.BlockSpec((tm, s), lambda i: (i, 0)) for s in splits],
        out_shape=[jax.ShapeDtypeStruct((t, s), F32) for s in splits],
        compiler_params=_cparams("parallel"),
        name="norm_in_proj",
    )(x2, g.reshape(1, d), w_bf16)


def _compress_kernel(k_ref, v_ref, pek_ref, pev_ref, wk1_ref, wk2_ref, wv1_ref, wv2_ref,
                     kc_ref, vc_ref, *, n_blk):
    hp = lax.Precision.HIGHEST
    dh = NSA_HEAD_DIM
    per = CMP_BLOCK // CMP_STRIDE
    assert per == 2

    def one(src_ref, pe_ref, w1_ref, w2_ref, dst_ref):
        pe8 = jnp.broadcast_to(pe_ref[...], (8, CMP_BLOCK * dh))
        const = jnp.dot(pe8, w1_ref[...], precision=hp, preferred_element_type=F32)[0:1, :]
        acc = [[jnp.zeros((n_blk, dh), F32) for _ in range(per)] for _ in range(NSA_KV_GROUPS)]
        for lo in range(CMP_STRIDE):
            rows = src_ref[pl.ds(lo, n_blk, stride=CMP_STRIDE), :]
            for g in range(NSA_KV_GROUPS):
                rg = rows[:, g * dh:(g + 1) * dh]
                for hi in range(per):
                    l = hi * CMP_STRIDE + lo
                    acc[g][hi] = acc[g][hi] + jnp.dot(rg, w1_ref[l * dh:(l + 1) * dh, :], precision=hp,
                                                     preferred_element_type=F32)
        for g in range(NSA_KV_GROUPS):
            pre = acc[g][0] + pltpu.roll(acc[g][1], n_blk - 1, 0) + const
            out = jnp.dot(_gelu(pre), w2_ref[...], precision=hp, preferred_element_type=F32)
            rid = lax.broadcasted_iota(jnp.int32, (n_blk, dh), 0)
            dst_ref[0, g] = jnp.where(rid < n_blk - 1, out, 0.0)

    one(k_ref, pek_ref, wk1_ref, wk2_ref, kc_ref)
    one(v_ref, pev_ref, wv1_ref, wv2_ref, vc_ref)


def _compress(qkv, b, s, pe_k, pe_v, wk1, wk2, wv1, wv2):
    n_blk = s // CMP_STRIDE
    dh = NSA_HEAD_DIM
    kcol = NSA_WIDTH // LANE
    full = lambda shape: pl.BlockSpec(shape, lambda i: tuple(0 for _ in shape))
    out_spec = pl.BlockSpec((1, NSA_KV_GROUPS, n_blk, dh), lambda i: (i, 0, 0, 0))
    out_shape = jax.ShapeDtypeStruct((b, NSA_KV_GROUPS, n_blk, dh), F32)
    return pl.pallas_call(
        functools.partial(_compress_kernel, n_blk=n_blk),
        grid=(b,),
        in_specs=[pl.BlockSpec((s, LANE), lambda i: (i, kcol)),
                  pl.BlockSpec((s, LANE), lambda i: (i, kcol + 1)),
                  full((1, CMP_BLOCK * dh)), full((1, CMP_BLOCK * dh)),
                  full((CMP_BLOCK * dh, dh)), full((dh, dh)),
                  full((CMP_BLOCK * dh, dh)), full((dh, dh))],
        out_specs=[out_spec, out_spec],
        out_shape=[out_shape, out_shape],
        compiler_params=_cparams("parallel"),
        name="nsa_compress",
    )(qkv, qkv, pe_k.reshape(1, -1), pe_v.reshape(1, -1), wk1, wk2, wv1, wv2)


def _cmp_attn_kernel(q_ref, kc_ref, vc_ref, o_ref, sel_ref, ot_scr, *, tq, n_blk, n_slc):
    hp = lax.Precision.HIGHEST
    dh = NSA_HEAD_DIM
    sub = 8
    rows = NSA_HPG * tq
    q0 = pl.program_id(1) * tq
    n_id = lax.broadcasted_iota(jnp.int32, (n_blk, rows), 0)
    t_id = q0 + lax.broadcasted_iota(jnp.int32, (n_blk, rows), 1) % tq
    mask = n_id * CMP_STRIDE + (CMP_BLOCK - 1) <= t_id
    jj = lax.broadcasted_iota(jnp.int32, (n_slc, n_blk), 0) * SLC_BLOCK
    nn = lax.broadcasted_iota(jnp.int32, (n_slc, n_blk), 1) * CMP_STRIDE
    overlap_t = jnp.where((nn <= jj + SLC_BLOCK - 1) & (nn + CMP_BLOCK - 1 >= jj), 1.0, 0.0)
    jb = lax.broadcasted_iota(jnp.int32, (n_slc, tq), 0)
    t_blk = (q0 + lax.broadcasted_iota(jnp.int32, (n_slc, tq), 1)) // SLC_BLOCK
    future = jb > t_blk
    forced = (jb == 0) | (jb == t_blk) | (jb == t_blk - 1)
    n_sel = min(SLC_TOPN, n_slc)
    lk = lax.broadcasted_iota(jnp.int32, (n_blk, dh), 1)
    nk_f = (lax.broadcasted_iota(jnp.int32, (n_blk, dh), 0) * CMP_STRIDE).astype(F32)
    cols_k = jnp.where(lk <= 1, nk_f, 0.0)
    lq = lax.broadcasted_iota(jnp.int32, (tq, dh), 1)

    for g in range(NSA_KV_GROUPS):
        kca = jnp.concatenate([kc_ref[0, g], cols_k], axis=1).astype(BF16)
        vc = vc_ref[0, g].astype(BF16)
        blocks = []
        for h in range(NSA_HPG):
            hh = g * NSA_HPG + h
            qh = q_ref[:, hh * dh:(hh + 1) * dh] * (LOG2E / math.sqrt(dh))
            blocks.append(jnp.concatenate([qh, _slope_pair(LOG2E * SLOPES[hh], lq, 0)], axis=1))
        qa = jnp.concatenate(blocks, axis=0).astype(BF16)
        st = lax.dot_general(kca, qa, _NT, preferred_element_type=F32)
        st = jnp.where(mask, st, NEG)
        m = jnp.max(st, axis=0, keepdims=True)
        e = jnp.where(mask, jnp.exp2(st - m), 0.0)
        l = jnp.sum(e, axis=0, keepdims=True)
        p = e / jnp.where(l > 0.0, l, 1.0)
        ot_scr[g * dh:(g + 1) * dh, :] = lax.dot_general(vc, p.astype(BF16), _TN, preferred_element_type=F32)
        psum = p[:, 0:tq]
        for h in range(1, NSA_HPG):
            psum = psum + p[:, h * tq:(h + 1) * tq]
        imp = jnp.dot(overlap_t, psum, precision=hp, preferred_element_type=F32)
        imp = jnp.where(forced, FORCED, jnp.where(future, NEG, imp))
        slabs = [imp[v * sub:(v + 1) * sub, :] for v in range(n_slc // sub)]
        cnts = [jnp.zeros((sub, tq), F32) for _ in slabs]
        jrow = lax.broadcasted_iota(jnp.int32, (sub, tq), 0)
        for j2 in range(n_slc):
            row = imp[j2:j2 + 1, :]
            for v, slab in enumerate(slabs):
                if v * sub > j2:
                    before = row >= slab
                elif v * sub + sub - 1 <= j2:
                    before = row > slab
                else:
                    before = jnp.where(jrow + v * sub > j2, jnp.where(row >= slab, 1.0, 0.0),
                                       jnp.where(row > slab, 1.0, 0.0)) > 0.5
                cnts[v] = cnts[v] + jnp.where(before, 1.0, 0.0)
        sel_t = jnp.where(jnp.concatenate(cnts, axis=0) < float(n_sel), 1.0, 0.0)
        if n_slc < LANE:
            sel_t = jnp.concatenate([sel_t, jnp.zeros((LANE - n_slc, tq), F32)], axis=0)
        sel_ref[g] = sel_t.T[:, :n_slc]

    out = ot_scr[...].T
    for g in range(NSA_KV_GROUPS):
        for h in range(NSA_HPG):
            hh = g * NSA_HPG + h
            o_ref[:, hh * dh:(hh + 1) * dh] = out[h * tq:(h + 1) * tq, g * dh:(g + 1) * dh]


def _cmp_attn(qkv, kcmp, vcmp, b, s, tq):
    t = b * s
    nq = s // tq
    n_blk = kcmp.shape[2]
    n_slc = s // SLC_BLOCK
    cmp_spec = pl.BlockSpec((1, NSA_KV_GROUPS, n_blk, NSA_HEAD_DIM), lambda bi, i: (bi, 0, 0, 0))
    return pl.pallas_call(
        functools.partial(_cmp_attn_kernel, tq=tq, n_blk=n_blk, n_slc=n_slc),
        grid=(b, nq),
        in_specs=[pl.BlockSpec((tq, NSA_WIDTH), lambda bi, i: (bi * nq + i, 0)), cmp_spec, cmp_spec],
        out_specs=[pl.BlockSpec((tq, NSA_WIDTH), lambda bi, i: (bi * nq + i, 0)),
                   pl.BlockSpec((NSA_KV_GROUPS, tq, n_slc), lambda bi, i: (0, bi * nq + i, 0))],
        out_shape=[jax.ShapeDtypeStruct((t, NSA_WIDTH), F32),
                   jax.ShapeDtypeStruct((NSA_KV_GROUPS, t, n_slc), F32)],
        scratch_shapes=[pltpu.VMEM((NSA_KV_GROUPS * NSA_HEAD_DIM, NSA_HPG * tq), F32)],
        compiler_params=_cparams("parallel", "parallel"),
        name="nsa_cmp_attn_select",
    )(qkv, kcmp, vcmp)


ATT_TILE = 256
AUG_SLC = 2 * LANE
AUG_WIN = LANE


def _attn_prep_kernel(ks_ref, kw_ref, kas_ref, kaw_ref, *, ts):
    dh = NSA_HEAD_DIM
    s0 = pl.program_id(1) * ts
    pos = s0 + lax.broadcasted_iota(jnp.int32, (ts, dh), 0)
    lane = lax.broadcasted_iota(jnp.int32, (ts, dh), 1)
    onehot = jnp.where(lane == pos // SLC_BLOCK, 1.0, 0.0)
    off = (pos % ATT_TILE).astype(F32)
    cols = jnp.where(lane == 0, 1.0, jnp.where(lane <= 2, off, 0.0))
    zeros = jnp.zeros((ts, dh), F32)
    for g in range(NSA_KV_GROUPS):
        ks = ks_ref[:, g * dh:(g + 1) * dh]
        kas_ref[0, g] = jnp.concatenate([ks, onehot, cols, zeros], axis=1).astype(BF16)
        kw = kw_ref[:, g * dh:(g + 1) * dh]
        kaw_ref[0, g] = jnp.concatenate([kw, cols], axis=1).astype(BF16)


def _attn_prep(qkv, b, s, ts):
    kcol = NSA_WIDTH // LANE + 2
    return pl.pallas_call(
        functools.partial(_attn_prep_kernel, ts=ts),
        grid=(b, s // ts),
        in_specs=[pl.BlockSpec((ts, LANE), lambda bi, i: (bi * (s // ts) + i, kcol)),
                  pl.BlockSpec((ts, LANE), lambda bi, i: (bi * (s // ts) + i, kcol + 2))],
        out_specs=[pl.BlockSpec((1, NSA_KV_GROUPS, ts, AUG_SLC), lambda bi, i: (bi, 0, i, 0)),
                   pl.BlockSpec((1, NSA_KV_GROUPS, ts, AUG_WIN), lambda bi, i: (bi, 0, i, 0))],
        out_shape=[jax.ShapeDtypeStruct((b, NSA_KV_GROUPS, s, AUG_SLC), BF16),
                   jax.ShapeDtypeStruct((b, NSA_KV_GROUPS, s, AUG_WIN), BF16)],
        compiler_params=_cparams("parallel", "parallel"),
        name="nsa_key_augment",
    )(qkv, qkv)


def _flash_kernel(*refs, mode, tq):
    if mode == "slc":
        q_ref, ka_ref, v_ref, sel_ref, o_ref, qa_scr, m_scr, l_scr, acc_scr = refs
    else:
        q_ref, ka_ref, v_ref, o_ref, qa_scr, m_scr, l_scr, acc_scr = refs
        sel_ref = None
    dh = NSA_HEAD_DIM
    tk = tq
    rows = NSA_HPG * tq
    i = pl.program_id(1)
    q0 = i * tq
    qid = lax.broadcasted_iota(jnp.int32, (tk, rows), 1) % tq
    kid = lax.broadcasted_iota(jnp.int32, (tk, rows), 0)
    head_of_row = lax.broadcasted_iota(jnp.int32, (1, rows), 1) // tq
    rq = lax.broadcasted_iota(jnp.int32, (tq, dh), 0).astype(F32)
    lq = lax.broadcasted_iota(jnp.int32, (tq, dh), 1)

    slope_rows = []
    for g in range(NSA_KV_GROUPS):
        slopes = [LOG2E * SLOPES[g * NSA_HPG + h] for h in range(NSA_HPG)]
        slope_row = jnp.full((1, rows), slopes[-1], F32)
        for h in range(NSA_HPG - 2, -1, -1):
            slope_row = jnp.where(head_of_row == h, slopes[h], slope_row)
        slope_rows.append(slope_row)
        blocks = []
        for h in range(NSA_HPG):
            hh = g * NSA_HPG + h
            qh = q_ref[:, hh * dh:(hh + 1) * dh] * (LOG2E / math.sqrt(dh))
            cols = jnp.where(lq == 0, -slopes[h] * rq, _slope_pair(slopes[h], lq, 1))
            if mode == "slc":
                n_slc = sel_ref.shape[-1]
                notsel = (sel_ref[g] - 1.0) * (-NEG)
                if n_slc < dh:
                    notsel = jnp.concatenate([notsel, jnp.zeros((tq, dh - n_slc), F32)], axis=1)
                parts = [qh, notsel, cols, jnp.zeros((tq, dh), F32)]
            else:
                parts = [qh, cols]
            blocks.append(jnp.concatenate(parts, axis=1))
        qa_scr[g] = jnp.concatenate(blocks, axis=0).astype(BF16)
    m_scr[...] = jnp.full(m_scr.shape, NEG, F32)
    l_scr[...] = jnp.zeros(l_scr.shape, F32)
    acc_scr[...] = jnp.zeros(acc_scr.shape, F32)

    def tiles(specs):
        for g in range(NSA_KV_GROUPS):
            sts, shifts, vgs = [], [], []
            for kt, mask in specs:
                k0 = pl.multiple_of(kt * tk, tk)
                ka = ka_ref[0, g, pl.ds(k0, tk), :]
                vgs.append(v_ref[pl.ds(k0, tk), g * dh:(g + 1) * dh].astype(BF16))
                st = lax.dot_general(ka, qa_scr[g], _NT, preferred_element_type=F32)
                sts.append(st if mask is None else jnp.where(mask, st, NEG))
                shifts.append(slope_rows[g] * jnp.asarray(k0 - q0, F32))
            m_old = m_scr[g]
            m_new = m_old
            for st, shift in zip(sts, shifts):
                m_new = jnp.maximum(m_new, jnp.max(st, axis=0, keepdims=True) + shift)
            alpha = jnp.exp2(m_old - m_new)
            l_new = alpha * l_scr[g]
            acc = alpha * acc_scr[g * dh:(g + 1) * dh, :]
            for st, shift, vg in zip(sts, shifts, vgs):
                p = jnp.exp2(st - (m_new - shift))
                l_new = l_new + jnp.sum(p, axis=0, keepdims=True)
                acc = acc + lax.dot_general(vg, p.astype(BF16), _TN, preferred_element_type=F32)
            l_scr[g] = l_new
            acc_scr[g * dh:(g + 1) * dh, :] = acc
            m_scr[g] = m_new

    diag = kid <= qid
    if mode == "slc":
        def body(k2, carry):
            tiles([(2 * k2, None), (2 * k2 + 1, None)])
            return carry
        lax.fori_loop(0, i // 2, body, 0)

        @pl.when(i % 2 == 1)
        def _():
            tiles([(i - 1, None), (i, diag)])

        @pl.when(i % 2 == 0)
        def _():
            tiles([(i, diag)])
    else:
        @pl.when(i == 0)
        def _():
            tiles([(i, diag)])

        @pl.when(i == 1)
        def _():
            tiles([(i, diag), (i - 1, None)])

        @pl.when(i >= 2)
        def _():
            tiles([(i, diag), (i - 1, None), (i - 2, kid > qid)])

    for g in range(NSA_KV_GROUPS):
        acc_scr[g * dh:(g + 1) * dh, :] = acc_scr[g * dh:(g + 1) * dh, :] / l_scr[g]

    out = acc_scr[...].T
    for g in range(NSA_KV_GROUPS):
        for h in range(NSA_HPG):
            hh = g * NSA_HPG + h
            o_ref[:, hh * dh:(hh + 1) * dh] = out[h * tq:(h + 1) * tq, g * dh:(g + 1) * dh]


def _flash(qkv, kaug, sel, b, s, mode):
    tq = ATT_TILE
    assert WINDOW == 2 * tq and s % tq == 0
    t = b * s
    nq = s // tq
    aug = kaug.shape[-1]
    vcol = NSA_WIDTH // LANE + {"slc": 3, "win": 5}[mode]
    in_specs = [pl.BlockSpec((tq, NSA_WIDTH), lambda bi, i: (bi * nq + i, 0)),
                pl.BlockSpec((1, NSA_KV_GROUPS, s, aug), lambda bi, i: (bi, 0, 0, 0)),
                pl.BlockSpec((s, LANE), lambda bi, i: (bi, vcol))]
    args = [qkv, kaug, qkv]
    if mode == "slc":
        assert sel.shape[-1] <= NSA_HEAD_DIM
        in_specs.append(pl.BlockSpec((NSA_KV_GROUPS, tq, sel.shape[-1]), lambda bi, i: (0, bi * nq + i, 0)))
        args.append(sel)
    rows = NSA_HPG * tq
    return pl.pallas_call(
        functools.partial(_flash_kernel, mode=mode, tq=tq),
        grid=(b, nq),
        in_specs=in_specs,
        out_specs=pl.BlockSpec((tq, NSA_WIDTH), lambda bi, i: (bi * nq + i, 0)),
        out_shape=jax.ShapeDtypeStruct((t, NSA_WIDTH), F32),
        scratch_shapes=[pltpu.VMEM((NSA_KV_GROUPS, rows, aug), BF16),
                        pltpu.VMEM((NSA_KV_GROUPS, 1, rows), F32),
                        pltpu.VMEM((NSA_KV_GROUPS, 1, rows), F32),
                        pltpu.VMEM((NSA_KV_GROUPS * NSA_HEAD_DIM, rows), F32)],
        compiler_params=_cparams("parallel", "parallel"),
        name="nsa_flash_" + mode,
    )(*args)


HALO = 32
CONV_SHIFTS = 8


def _mix_kernel(ocmp_ref, oslc_ref, owin_ref, gates_ref, glu_ref, halo_ref, merge_ref, x_ref,
                wexp_ref, wnsa_ref, wdw_ref, bdw_ref, gln_ref, bln_ref, wconv_ref, wo_ref,
                o_ref, uext_scr, *, tm, tiles_per_seq):
    i = pl.program_id(0)
    gts = _sigmoid(gates_ref[...])
    g_hi = gts.astype(BF16)
    r1 = gts - g_hi.astype(F32)
    g_mid = r1.astype(BF16)
    g_lo = (r1 - g_mid.astype(F32)).astype(BF16)
    wexp = wexp_ref[...]
    gexp = (jnp.dot(g_hi, wexp, preferred_element_type=F32) + jnp.dot(g_mid, wexp, preferred_element_type=F32)
            + jnp.dot(g_lo, wexp, preferred_element_type=F32))
    w = NSA_WIDTH
    o_nsa = gexp[:, :w] * ocmp_ref[...] + gexp[:, w:2 * w] * oslc_ref[...] + gexp[:, 2 * w:] * owin_ref[...]
    y_a = jnp.dot(o_nsa.astype(BF16), wnsa_ref[...], preferred_element_type=F32)

    c = CONV_CH
    gl = glu_ref[...]
    u = gl[:, :c] * _sigmoid(gl[:, c:])
    hl = halo_ref[...]
    uh = hl[:, :c] * _sigmoid(hl[:, c:])
    uh = jnp.where(i % tiles_per_seq == 0, 0.0, uh)
    uext_scr[0, 0:HALO, :] = uh
    uext_scr[0, HALO:HALO + tm, :] = u
    span = HALO + tm - CONV_SHIFTS
    for j in range(1, CONV_SHIFTS):
        uext_scr[j, 0:span, :] = uext_scr[0, pl.ds(j, span), :]
    acc = jnp.zeros((tm, c), F32)
    for k in range(CONV_WIDTH):
        first = HALO - (CONV_WIDTH - 1) + k
        acc = acc + uext_scr[first % CONV_SHIFTS, pl.ds(first - first % CONV_SHIFTS, tm), :] * wdw_ref[k:k + 1, :]
    cv = acc + bdw_ref[...]
    mu = jnp.mean(cv, axis=-1, keepdims=True)
    var = jnp.mean(jnp.square(cv - mu), axis=-1, keepdims=True)
    un = (cv - mu) * lax.rsqrt(var + EPS) * gln_ref[...] + bln_ref[...]
    act = un * _sigmoid(un)
    y_b = jnp.dot(act.astype(BF16), wconv_ref[...], preferred_element_type=F32)

    d = x_ref.shape[-1]
    mg = merge_ref[...]
    z = _sigmoid(mg[:, :d]) * y_a + _sigmoid(mg[:, d:]) * y_b
    o_ref[...] = x_ref[...] + jnp.dot(z.astype(BF16), wo_ref[...], preferred_element_type=F32)


def _mix(ocmp, oslc, owin, gates, glu, merge, x2, wexp, wnsa, wdw, bdw, gln, bln, wconv, wo, s, tm):
    t, d = x2.shape
    row = lambda n: pl.BlockSpec((tm, n), lambda i: (i, 0))
    full = lambda a: pl.BlockSpec(a.shape, lambda i: tuple(0 for _ in a.shape))
    halo_spec = pl.BlockSpec((HALO, glu.shape[1]), lambda i: (jnp.maximum(i * (tm // HALO) - 1, 0), 0))
    weights = [wexp, wnsa, wdw, bdw, gln, bln, wconv, wo]
    return pl.pallas_call(
        functools.partial(_mix_kernel, tm=tm, tiles_per_seq=s // tm),
        grid=(t // tm,),
        in_specs=[row(NSA_WIDTH), row(NSA_WIDTH), row(NSA_WIDTH), row(GATE_PAD), row(glu.shape[1]),
                  halo_spec, row(merge.shape[1]), row(d)] + [full(a) for a in weights],
        out_specs=row(d),
        out_shape=jax.ShapeDtypeStruct((t, d), F32),
        scratch_shapes=[pltpu.VMEM((CONV_SHIFTS, HALO + tm, CONV_CH), F32)],
        compiler_params=_cparams("parallel"),
        name="mixer_merge",
    )(ocmp, oslc, owin, gates, glu, glu, merge, x2, *weights)


def _extract_topk(jobs, k_top):
    def body(k, carry):
        for cur_ref, rank_ref, val_ref in jobs:
            cur = cur_ref[...]
            n_rows = cur.shape[0]
            rid = lax.broadcasted_iota(jnp.int32, cur.shape, 0).astype(F32)
            v = jnp.max(cur, axis=0, keepdims=True)
            idx = jnp.min(jnp.where(cur == v, rid, float(n_rows)), axis=0, keepdims=True)
            hit = rid == idx
            rank_ref[...] = jnp.where(hit, jnp.asarray(k, F32), rank_ref[...])
            cur_ref[...] = jnp.where(hit, -jnp.inf, cur)
            val_ref[pl.ds(k, 1), :] = v
        return carry

    lax.fori_loop(0, k_top, body, 0)


def _peer_stats_kernel(x_ref, g_ref, wq_ref, keys_ref, xn_ref, e1_ref, n1_ref, e2_ref, r2_ref,
                       c1_scr, c2_scr, r1_scr, r2_scr, v1_scr, v2_scr, cand_scr, csel_scr, cval_scr, *, tm):
    nk = PEER_NKEYS
    kt = PEER_TOPK
    x = x_ref[...]
    xn = (x * lax.rsqrt(jnp.mean(x * x, axis=-1, keepdims=True) + EPS) * g_ref[...]).astype(BF16)
    xn_ref[...] = xn
    big = float(nk)
    n_cand = cand_scr.shape[0]

    def head(h, carry):
        qp = jnp.dot(xn, wq_ref[h], preferred_element_type=F32)
        s1 = lax.dot_general(keys_ref[h, 0], qp[:, :PEER_HALF].astype(BF16), _NT,
                             preferred_element_type=F32)
        s2 = lax.dot_general(keys_ref[h, 1], qp[:, PEER_HALF:].astype(BF16), _NT,
                             preferred_element_type=F32)
        c1_scr[...] = s1
        c2_scr[...] = s2
        r1_scr[...] = jnp.full((nk, tm), big, F32)
        r2_scr[...] = jnp.full((nk, tm), big, F32)
        _extract_topk([(c1_scr, r1_scr, v1_scr), (c2_scr, r2_scr, v2_scr)], kt)
        v1 = v1_scr[...]
        v2 = v2_scr[...]
        pieces = [v1[a:a + 1, :] + v2[0:kt // (a + 1), :] for a in range(kt)]
        pad = n_cand - sum(kt // (a + 1) for a in range(kt))
        pieces.append(jnp.full((pad, tm), -jnp.inf, F32))
        cand = jnp.concatenate(pieces, axis=0)
        cand_scr[...] = cand
        csel_scr[...] = jnp.full((n_cand, tm), big, F32)
        _extract_topk([(cand_scr, csel_scr, cval_scr)], kt)
        sel_f = jnp.where(csel_scr[...] < big, 1.0, 0.0)
        top = v1[0:1, :] + v2[0:1, :]
        z = jnp.sum(sel_f * jnp.exp(cand - top), axis=0, keepdims=True)
        r1 = r1_scr[...]
        n1 = jnp.zeros((nk, tm), F32)
        off = 0
        for a in range(kt):
            cnt = kt // (a + 1)
            n_a = jnp.sum(sel_f[off:off + cnt, :], axis=0, keepdims=True)
            off += cnt
            n1 = jnp.where(r1 == float(a), n_a, n1)
        e1_ref[h] = jnp.exp(s1 - v1[0:1, :])
        n1_ref[h] = n1
        e2_ref[h] = (0.5 * jnp.exp(s2 - v2[0:1, :])) / z
        r2_ref[h] = r2_scr[...]
        return carry

    lax.fori_loop(0, PEER_HEADS, head, 0)


def _peer_stats(x1, g, wq_h, keys_bf16, tm):
    t, d = x1.shape
    nk, kt = PEER_NKEYS, PEER_TOPK
    stat_spec = pl.BlockSpec((PEER_HEADS, nk, tm), lambda i: (0, 0, i))
    stat_shape = jax.ShapeDtypeStruct((PEER_HEADS, nk, t), F32)
    full = lambda a: pl.BlockSpec(a.shape, lambda i: tuple(0 for _ in a.shape))
    n_cells = sum(kt // (a + 1) for a in range(kt))
    n_cand = -(-n_cells // 8) * 8
    return pl.pallas_call(
        functools.partial(_peer_stats_kernel, tm=tm),
        grid=(t // tm,),
        in_specs=[pl.BlockSpec((tm, d), lambda i: (i, 0)), pl.BlockSpec((1, d), lambda i: (0, 0)),
                  full(wq_h), full(keys_bf16)],
        out_specs=[pl.BlockSpec((tm, d), lambda i: (i, 0))] + [stat_spec] * 4,
        out_shape=[jax.ShapeDtypeStruct((t, d), BF16)] + [stat_shape] * 4,
        scratch_shapes=[pltpu.VMEM((nk, tm), F32), pltpu.VMEM((nk, tm), F32),
                        pltpu.VMEM((nk, tm), F32), pltpu.VMEM((nk, tm), F32),
                        pltpu.VMEM((kt, tm), F32), pltpu.VMEM((kt, tm), F32),
                        pltpu.VMEM((n_cand, tm), F32), pltpu.VMEM((n_cand, tm), F32),
                        pltpu.VMEM((kt, tm), F32)],
        compiler_params=_cparams("parallel"),
        name="peer_stats",
    )(x1, g.reshape(1, d), wq_h, keys_bf16)


PEER_KEY_GROUP = 8


PEER_STAGES = 3


def _peer_dense_kernel(xn_ref, u_ref, vt_ref, e1_ref, n1_ref, e2_ref, r2_ref, o_ref, ht_scr, at_scr,
                       *, tm, eb, nb):
    nk = PEER_NKEYS
    s = pl.program_id(0)
    n_keys = eb // nk

    @pl.when(s == 0)
    def _():
        ht_scr[...] = jnp.zeros(ht_scr.shape, F32)
        at_scr[...] = jnp.zeros(at_scr.shape, BF16)

    @pl.when((s < PEER_STAGES) | ((s - (PEER_STAGES - 1)) % nb == 0))
    def _():
        o_ref[...] = jnp.zeros(o_ref.shape, F32)

    def step(slot):
        o_ref[...] += jnp.dot(vt_ref[...], at_scr[slot], preferred_element_type=F32)

        jb = jnp.maximum(s - 1, 0) % nb
        n1_rows = [[n1_ref[h, pl.ds(jb * n_keys + c, 1), :] for c in range(n_keys)]
                   for h in range(PEER_HEADS)]
        e1_rows = [[e1_ref[h, pl.ds(jb * n_keys + c, 1), :] for c in range(n_keys)]
                   for h in range(PEER_HEADS)]
        ht = ht_scr[1 - slot].astype(BF16)
        at_scr[1 - slot] = ht * (1.0 + lax.erf(ht * (1.0 / math.sqrt(2.0))))
        pk = BF16_ROWS
        for lt in range(tm // LANE):
            lanes = slice(lt * LANE, (lt + 1) * LANE)
            gates = []
            for c0 in range(0, n_keys, PEER_KEY_GROUP):
                grp = [jnp.zeros((nk // pk, pk, LANE), BF16) for _ in range(PEER_KEY_GROUP)]
                for h in range(PEER_HEADS):
                    r2 = r2_ref[h, :, lanes].astype(BF16).reshape(nk // pk, pk, LANE)
                    e2 = e2_ref[h, :, lanes].astype(BF16).reshape(nk // pk, pk, LANE)
                    for c in range(PEER_KEY_GROUP):
                        n1 = jnp.broadcast_to(n1_rows[h][c0 + c][:, lanes], (pk, LANE)).astype(BF16)
                        e1 = jnp.broadcast_to(e1_rows[h][c0 + c][:, lanes], (pk, LANE)).astype(BF16)
                        grp[c] = grp[c] + jnp.where(r2 < n1, e2, jnp.zeros_like(e2)) * e1
                gates.extend(g.reshape(nk, LANE) for g in grp)
            gate = jnp.concatenate(gates, axis=0)
            at_scr[1 - slot, :, lanes] = at_scr[1 - slot, :, lanes] * gate

        ht_scr[slot] = lax.dot_general(u_ref[...], xn_ref[...], _NT, preferred_element_type=F32)

    step(s % 2)


def _peer_dense(xn, u_bf16, vt_bf16, e1, n1, e2, r2, tm, eb):
    t, d = xn.shape
    n_exp = u_bf16.shape[0]
    nb = n_exp // eb
    n_items = (t // tm) * nb
    item = lambda s, lag: jnp.clip(s - lag, 0, n_items - 1)
    stat_spec = pl.BlockSpec((PEER_HEADS, PEER_NKEYS, tm), lambda s: (0, 0, item(s, 1) // nb),
                             pipeline_mode=pl.Buffered(1))
    return pl.pallas_call(
        functools.partial(_peer_dense_kernel, tm=tm, eb=eb, nb=nb),
        grid=(n_items + PEER_STAGES - 1,),
        in_specs=[pl.BlockSpec((tm, d), lambda s: (item(s, 0) // nb, 0)),
                  pl.BlockSpec((eb, d), lambda s: (item(s, 0) % nb, 0)),
                  pl.BlockSpec((None, d, eb), lambda s: (item(s, 2) % nb, 0, 0)),
                  stat_spec, stat_spec, stat_spec, stat_spec],
        out_specs=pl.BlockSpec((d, tm), lambda s: (0, item(s, 2) // nb)),
        out_shape=jax.ShapeDtypeStruct((d, t), F32),
        scratch_shapes=[pltpu.VMEM((2, eb, tm), F32), pltpu.VMEM((2, eb, tm), BF16)],
        compiler_params=_cparams("arbitrary"),
        name="peer_dense",
    )(xn, u_bf16, vt_bf16, e1, n1, e2, r2)


PEER_CHUNK = 256


def _peer_dense2_kernel(xn_ref, u_ref, vt_ref, e1_ref, n1_ref, e2_ref, r2_ref, o_ref, ht_scr, act_scr,
                        *, tm, eb, nb):
    nk = PEER_NKEYS
    s = pl.program_id(0)
    n_keys = eb // nk
    slot = s % 2
    other = 1 - slot

    @pl.when(s == 0)
    def _():
        ht_scr[...] = jnp.zeros(ht_scr.shape, F32)

    @pl.when((s < 2) | ((s - 1) % nb == 0))
    def _():
        o_ref[...] = jnp.zeros(o_ref.shape, F32)

    jb = jnp.maximum(s - 1, 0) % nb
    n1_rows = [[n1_ref[h, pl.ds(jb * n_keys + c, 1), :] for c in range(n_keys)] for h in range(PEER_HEADS)]
    e1_rows = [[e1_ref[h, pl.ds(jb * n_keys + c, 1), :] for c in range(n_keys)] for h in range(PEER_HEADS)]
    ht = ht_scr[other].astype(BF16)
    act_scr[...] = ht * (1.0 + lax.erf(ht * (1.0 / math.sqrt(2.0))))
    pk = BF16_ROWS
    for ch in range(tm // PEER_CHUNK):
        cols = slice(ch * PEER_CHUNK, (ch + 1) * PEER_CHUNK)
        ht_scr[slot, :, cols] = lax.dot_general(u_ref[...], xn_ref[cols, :], _NT, preferred_element_type=F32)
        parts = []
        for lt in range(PEER_CHUNK // LANE):
            lanes = slice(ch * PEER_CHUNK + lt * LANE, ch * PEER_CHUNK + (lt + 1) * LANE)
            grp = [jnp.zeros((nk // pk, pk, LANE), BF16) for _ in range(n_keys)]
            for h in range(PEER_HEADS):
                r2 = r2_ref[h, :, lanes].astype(BF16).reshape(nk // pk, pk, LANE)
                e2 = e2_ref[h, :, lanes].astype(BF16).reshape(nk // pk, pk, LANE)
                for c in range(n_keys):
                    n1 = jnp.broadcast_to(n1_rows[h][c][:, lanes], (pk, LANE)).astype(BF16)
                    e1 = jnp.broadcast_to(e1_rows[h][c][:, lanes], (pk, LANE)).astype(BF16)
                    grp[c] = grp[c] + jnp.where(r2 < n1, e2, jnp.zeros_like(e2)) * e1
            gate = jnp.concatenate([g.reshape(nk, LANE) for g in grp], axis=0)
            parts.append(act_scr[:, lanes] * gate)
        at = jnp.concatenate(parts, axis=1)
        o_ref[:, cols] += jnp.dot(vt_ref[...], at, preferred_element_type=F32)


def _peer_dense2(xn, u_bf16, vt_bf16, e1, n1, e2, r2, tm, eb):
    t, d = xn.shape
    n_exp = u_bf16.shape[0]
    nb = n_exp // eb
    n_items = (t // tm) * nb
    item = lambda s, lag: jnp.clip(s - lag, 0, n_items - 1)
    stat_spec = pl.BlockSpec((PEER_HEADS, PEER_NKEYS, tm), lambda s: (0, 0, item(s, 1) // nb),
                             pipeline_mode=pl.Buffered(1))
    return pl.pallas_call(
        functools.partial(_peer_dense2_kernel, tm=tm, eb=eb, nb=nb),
        grid=(n_items + 1,),
        in_specs=[pl.BlockSpec((tm, d), lambda s: (item(s, 0) // nb, 0)),
                  pl.BlockSpec((eb, d), lambda s: (item(s, 0) % nb, 0)),
                  pl.BlockSpec((None, d, eb), lambda s: (item(s, 1) % nb, 0, 0)),
                  stat_spec, stat_spec, stat_spec, stat_spec],
        out_specs=pl.BlockSpec((d, tm), lambda s: (0, item(s, 1) // nb)),
        out_shape=jax.ShapeDtypeStruct((d, t), F32),
        scratch_shapes=[pltpu.VMEM((2, eb, tm), F32), pltpu.VMEM((eb, tm), BF16)],
        compiler_params=_cparams("arbitrary"),
        name="peer_dense",
    )(xn, u_bf16, vt_bf16, e1, n1, e2, r2)


def _final_kernel(x1_ref, yt_ref, g_ref, o_ref):
    y = x1_ref[...] + yt_ref[...].T
    o_ref[...] = y * lax.rsqrt(jnp.mean(y * y, axis=-1, keepdims=True) + EPS) * g_ref[...]


def _final_norm(x1, yt, g_final, tm):
    t, d = x1.shape
    return pl.pallas_call(
        _final_kernel,
        grid=(t // tm,),
        in_specs=[pl.BlockSpec((tm, d), lambda i: (i, 0)), pl.BlockSpec((d, tm), lambda i: (0, i)),
                  pl.BlockSpec((1, d), lambda i: (0, 0))],
        out_specs=pl.BlockSpec((tm, d), lambda i: (i, 0)),
        out_shape=jax.ShapeDtypeStruct((t, d), F32),
        compiler_params=_cparams("parallel"),
        name="peer_residual_final_norm",
    )(x1, yt, g_final.reshape(1, d))


def _gate_expand_matrix():
    r = jnp.arange(GATE_PAD)[:, None]
    c = jnp.arange(3 * NSA_WIDTH)[None, :]
    return ((r < GATE_COLS) & (r == (c // NSA_WIDTH) * NSA_HEADS + (c % NSA_WIDTH) // NSA_HEAD_DIM)).astype(F32)


def kernel(x, g_mix, w_in, pe_cmp_k, pe_cmp_v, w_cmp_k1, w_cmp_k2, w_cmp_v1, w_cmp_v2, w_nsa_out, w_dw, b_dw,
           g_conv_ln, b_conv_ln, w_conv_out, w_o, g_ffn, w_peer_q, peer_sub_keys, peer_u, peer_v, g_final):
    b, s, d = x.shape
    t = b * s
    depth = w_in.shape[0]
    assert depth == 1, "the fused final norm assumes a single layer"
    assert s % 512 == 0 and d % LANE == 0
    x2 = x.reshape(t, d)
    wexp = _gate_expand_matrix().astype(BF16)
    l = 0
    gate_end = QKV_COLS + GATE_COLS
    w_pad = jnp.concatenate([w_in[l][:, :gate_end], jnp.zeros((d, GATE_PAD - GATE_COLS), F32),
                             w_in[l][:, gate_end:]], axis=1).astype(BF16)
    splits = (QKV_COLS, GATE_PAD, 2 * CONV_CH, 2 * d)
    qkv, gates, glu, merge = _norm_proj(x2, g_mix[l], w_pad, splits, tm=256)

    kcmp, vcmp = _compress(qkv, b, s, pe_cmp_k[l], pe_cmp_v[l], w_cmp_k1[l], w_cmp_k2[l],
                           w_cmp_v1[l], w_cmp_v2[l])
    o_cmp, sel = _cmp_attn(qkv, kcmp, vcmp, b, s, tq=256)
    kaug_slc, kaug_win = _attn_prep(qkv, b, s, ts=512)
    o_slc = _flash(qkv, kaug_slc, sel, b, s, "slc")
    o_win = _flash(qkv, kaug_win, None, b, s, "win")

    x1 = _mix(o_cmp, o_slc, o_win, gates, glu, merge, x2, wexp,
              w_nsa_out[l].astype(BF16), w_dw[l].reshape(CONV_WIDTH, CONV_CH), b_dw[l].reshape(1, -1),
              g_conv_ln[l].reshape(1, -1), b_conv_ln[l].reshape(1, -1),
              w_conv_out[l].astype(BF16), w_o[l].astype(BF16), s, tm=256)

    wq_h = w_peer_q[l].reshape(d, PEER_HEADS, PEER_QDIM).transpose(1, 0, 2).astype(BF16)
    xn, e1, n1, e2, r2 = _peer_stats(x1, g_ffn[l], wq_h, peer_sub_keys[l].astype(BF16), tm=512)
    eb = 1024
    n_exp = peer_v.shape[1]
    vt_blocks = peer_v[l].astype(BF16).reshape(n_exp // eb, eb, d).transpose(0, 2, 1)
    yt = _peer_dense2(xn, peer_u[l].astype(BF16), vt_blocks, e1, n1, e2, r2, tm=1024, eb=eb)
    out = _final_norm(x1, yt, g_final, tm=512)
    return out.reshape(b, s, d)
```

```python
import functools
import math

import jax
import jax.numpy as jnp
from jax import lax
from jax.experimental import pallas as pl
from jax.experimental.pallas import tpu as pltpu

F32 = jnp.float32
BF16 = jnp.bfloat16

NSA_HEADS = 8
NSA_KV_GROUPS = 2
NSA_HPG = NSA_HEADS // NSA_KV_GROUPS
NSA_HEAD_DIM = 64
NSA_WIDTH = NSA_HEADS * NSA_HEAD_DIM
NSA_KV_WIDTH = NSA_KV_GROUPS * NSA_HEAD_DIM
CMP_BLOCK = 32
CMP_STRIDE = 16
SLC_BLOCK = 64
SLC_TOPN = 16
WINDOW = 512
CONV_CH = 512
CONV_WIDTH = 31
PEER_HEADS = 8
PEER_NKEYS = 128
PEER_QDIM = 256
PEER_HALF = PEER_QDIM // 2
PEER_TOPK = 16
EPS = 1e-6
NEG = -1e30
FORCED = 1e9
SLOPES = tuple(2.0 ** (-8.0 * (h + 1) / NSA_HEADS) for h in range(NSA_HEADS))
LOG2E = math.log2(math.e)

LANE = 128
BF16_ROWS = 16
GATE_COLS = 3 * NSA_HEADS
GATE_PAD = LANE
QKV_COLS = NSA_WIDTH + 6 * NSA_KV_WIDTH
VMEM_LIMIT = 56 * 1024 * 1024

_NT = (((1,), (1,)), ((), ()))
_TN = (((0,), (0,)), ((), ()))


def _cparams(*sem):
    return pltpu.CompilerParams(dimension_semantics=sem, vmem_limit_bytes=VMEM_LIMIT)


def _sigmoid(x):
    return 1.0 / (1.0 + jnp.exp(-x))


def _slope_pair(slope, lane, first):
    s = jnp.full(lane.shape, slope, F32)
    hi = s.astype(BF16).astype(F32)
    return jnp.where(lane == first, hi, jnp.where(lane == first + 1, s - hi, 0.0))


def _gelu(x):
    return 0.5 * x * (1.0 + lax.erf(x * (1.0 / math.sqrt(2.0))))


def _norm_proj_kernel(x_ref, g_ref, w_ref, *out_refs, splits):
    x = x_ref[...]
    xn = x * lax.rsqrt(jnp.mean(x * x, axis=-1, keepdims=True) + EPS) * g_ref[...]
    xb = xn.astype(BF16)
    off = 0
    for o_ref, n in zip(out_refs, splits):
        o_ref[...] = jnp.dot(xb, w_ref[:, off:off + n], preferred_element_type=F32)
        off += n


def _norm_proj(x2, g, w_bf16, splits, tm):
    t, d = x2.shape
    n = w_bf16.shape[1]
    return pl.pallas_call(
        functools.partial(_norm_proj_kernel, splits=splits),
        grid=(t // tm,),
        in_specs=[pl.BlockSpec((tm, d), lambda i: (i, 0)),
                  pl.BlockSpec((1, d), lambda i: (0, 0)),
                  pl.BlockSpec((d, n), lambda i: (0, 0))],
        out_specs=[pl.BlockSpec((tm, s), lambda i: (i, 0)) for s in splits],
        out_shape=[jax.ShapeDtypeStruct((t, s), F32) for s in splits],
        compiler_params=_cparams("parallel"),
        name="norm_in_proj",
    )(x2, g.reshape(1, d), w_bf16)


def _compress_kernel(k_ref, v_ref, pek_ref, pev_ref, wk1_ref, wk2_ref, wv1_ref, wv2_ref,
                     kc_ref, vc_ref, *, n_blk):
    hp = lax.Precision.HIGHEST
    dh = NSA_HEAD_DIM
    per = CMP_BLOCK // CMP_STRIDE
    assert per == 2

    def one(src_ref, pe_ref, w1_ref, w2_ref, dst_ref):
        pe8 = jnp.broadcast_to(pe_ref[...], (8, CMP_BLOCK * dh))
        const = jnp.dot(pe8, w1_ref[...], precision=hp, preferred_element_type=F32)[0:1, :]
        acc = [[jnp.zeros((n_blk, dh), F32) for _ in range(per)] for _ in range(NSA_KV_GROUPS)]
        for lo in range(CMP_STRIDE):
            rows = src_ref[pl.ds(lo, n_blk, stride=CMP_STRIDE), :]
            for g in range(NSA_KV_GROUPS):
                rg = rows[:, g * dh:(g + 1) * dh]
                for hi in range(per):
                    l = hi * CMP_STRIDE + lo
                    acc[g][hi] = acc[g][hi] + jnp.dot(rg, w1_ref[l * dh:(l + 1) * dh, :], precision=hp,
                                                     preferred_element_type=F32)
        for g in range(NSA_KV_GROUPS):
            pre = acc[g][0] + pltpu.roll(acc[g][1], n_blk - 1, 0) + const
            out = jnp.dot(_gelu(pre), w2_ref[...], precision=hp, preferred_element_type=F32)
            rid = lax.broadcasted_iota(jnp.int32, (n_blk, dh), 0)
            dst_ref[0, g] = jnp.where(rid < n_blk - 1, out, 0.0)

    one(k_ref, pek_ref, wk1_ref, wk2_ref, kc_ref)
    one(v_ref, pev_ref, wv1_ref, wv2_ref, vc_ref)


def _compress(qkv, b, s, pe_k, pe_v, wk1, wk2, wv1, wv2):
    n_blk = s // CMP_STRIDE
    dh = NSA_HEAD_DIM
    kcol = NSA_WIDTH // LANE
    full = lambda shape: pl.BlockSpec(shape, lambda i: tuple(0 for _ in shape))
    out_spec = pl.BlockSpec((1, NSA_KV_GROUPS, n_blk, dh), lambda i: (i, 0, 0, 0))
    out_shape = jax.ShapeDtypeStruct((b, NSA_KV_GROUPS, n_blk, dh), F32)
    return pl.pallas_call(
        functools.partial(_compress_kernel, n_blk=n_blk),
        grid=(b,),
        in_specs=[pl.BlockSpec((s, LANE), lambda i: (i, kcol)),
                  pl.BlockSpec((s, LANE), lambda i: (i, kcol + 1)),
                  full((1, CMP_BLOCK * dh)), full((1, CMP_BLOCK * dh)),
                  full((CMP_BLOCK * dh, dh)), full((dh, dh)),
                  full((CMP_BLOCK * dh, dh)), full((dh, dh))],
        out_specs=[out_spec, out_spec],
        out_shape=[out_shape, out_shape],
        compiler_params=_cparams("parallel"),
        name="nsa_compress",
    )(qkv, qkv, pe_k.reshape(1, -1), pe_v.reshape(1, -1), wk1, wk2, wv1, wv2)


def _cmp_attn_kernel(q_ref, kc_ref, vc_ref, o_ref, sel_ref, ot_scr, *, tq, n_blk, n_slc):
    hp = lax.Precision.HIGHEST
    dh = NSA_HEAD_DIM
    sub = 8
    rows = NSA_HPG * tq
    q0 = pl.program_id(1) * tq
    n_id = lax.broadcasted_iota(jnp.int32, (n_blk, rows), 0)
    t_id = q0 + lax.broadcasted_iota(jnp.int32, (n_blk, rows), 1) % tq
    mask = n_id * CMP_STRIDE + (CMP_BLOCK - 1) <= t_id
    jj = lax.broadcasted_iota(jnp.int32, (n_slc, n_blk), 0) * SLC_BLOCK
    nn = lax.broadcasted_iota(jnp.int32, (n_slc, n_blk), 1) * CMP_STRIDE
    overlap_t = jnp.where((nn <= jj + SLC_BLOCK - 1) & (nn + CMP_BLOCK - 1 >= jj), 1.0, 0.0)
    jb = lax.broadcasted_iota(jnp.int32, (n_slc, tq), 0)
    t_blk = (q0 + lax.broadcasted_iota(jnp.int32, (n_slc, tq), 1)) // SLC_BLOCK
    future = jb > t_blk
    forced = (jb == 0) | (jb == t_blk) | (jb == t_blk - 1)
    n_sel = min(SLC_TOPN, n_slc)
    lk = lax.broadcasted_iota(jnp.int32, (n_blk, dh), 1)
    nk_f = (lax.broadcasted_iota(jnp.int32, (n_blk, dh), 0) * CMP_STRIDE).astype(F32)
    cols_k = jnp.where(lk <= 1, nk_f, 0.0)
    lq = lax.broadcasted_iota(jnp.int32, (tq, dh), 1)

    for g in range(NSA_KV_GROUPS):
        kca = jnp.concatenate([kc_ref[0, g], cols_k], axis=1).astype(BF16)
        vc = vc_ref[0, g].astype(BF16)
        blocks = []
        for h in range(NSA_HPG):
            hh = g * NSA_HPG + h
            qh = q_ref[:, hh * dh:(hh + 1) * dh] * (LOG2E / math.sqrt(dh))
            blocks.append(jnp.concatenate([qh, _slope_pair(LOG2E * SLOPES[hh], lq, 0)], axis=1))
        qa = jnp.concatenate(blocks, axis=0).astype(BF16)
        st = lax.dot_general(kca, qa, _NT, preferred_element_type=F32)
        st = jnp.where(mask, st, NEG)
        m = jnp.max(st, axis=0, keepdims=True)
        e = jnp.where(mask, jnp.exp2(st - m), 0.0)
        l = jnp.sum(e, axis=0, keepdims=True)
        p = e / jnp.where(l > 0.0, l, 1.0)
        ot_scr[g * dh:(g + 1) * dh, :] = lax.dot_general(vc, p.astype(BF16), _TN, preferred_element_type=F32)
        psum = p[:, 0:tq]
        for h in range(1, NSA_HPG):
            psum = psum + p[:, h * tq:(h + 1) * tq]
        imp = jnp.dot(overlap_t, psum, precision=hp, preferred_element_type=F32)
        imp = jnp.where(forced, FORCED, jnp.where(future, NEG, imp))
        slabs = [imp[v * sub:(v + 1) * sub, :] for v in range(n_slc // sub)]
        cnts = [jnp.zeros((sub, tq), F32) for _ in slabs]
        jrow = lax.broadcasted_iota(jnp.int32, (sub, tq), 0)
        for j2 in range(n_slc):
            row = imp[j2:j2 + 1, :]
            for v, slab in enumerate(slabs):
                if v * sub > j2:
                    before = row >= slab
                elif v * sub + sub - 1 <= j2:
                    before = row > slab
                else:
                    before = jnp.where(jrow + v * sub > j2, jnp.where(row >= slab, 1.0, 0.0),
                                       jnp.where(row > slab, 1.0, 0.0)) > 0.5
                cnts[v] = cnts[v] + jnp.where(before, 1.0, 0.0)
        sel_t = jnp.where(jnp.concatenate(cnts, axis=0) < float(n_sel), 1.0, 0.0)
        if n_slc < LANE:
            sel_t = jnp.concatenate([sel_t, jnp.zeros((LANE - n_slc, tq), F32)], axis=0)
        sel_ref[g] = sel_t.T[:, :n_slc]

    out = ot_scr[...].T
    for g in range(NSA_KV_GROUPS):
        for h in range(NSA_HPG):
            hh = g * NSA_HPG + h
            o_ref[:, hh * dh:(hh + 1) * dh] = out[h * tq:(h + 1) * tq, g * dh:(g + 1) * dh]


def _cmp_attn(qkv, kcmp, vcmp, b, s, tq):
    t = b * s
    nq = s // tq
    n_blk = kcmp.shape[2]
    n_slc = s // SLC_BLOCK
    cmp_spec = pl.BlockSpec((1, NSA_KV_GROUPS, n_blk, NSA_HEAD_DIM), lambda bi, i: (bi, 0, 0, 0))
    return pl.pallas_call(
        functools.partial(_cmp_attn_kernel, tq=tq, n_blk=n_blk, n_slc=n_slc),
        grid=(b, nq),
        in_specs=[pl.BlockSpec((tq, NSA_WIDTH), lambda bi, i: (bi * nq + i, 0)), cmp_spec, cmp_spec],
        out_specs=[pl.BlockSpec((tq, NSA_WIDTH), lambda bi, i: (bi * nq + i, 0)),
                   pl.BlockSpec((NSA_KV_GROUPS, tq, n_slc), lambda bi, i: (0, bi * nq + i, 0))],
        out_shape=[jax.ShapeDtypeStruct((t, NSA_WIDTH), F32),
                   jax.ShapeDtypeStruct((NSA_KV_GROUPS, t, n_slc), F32)],
        scratch_shapes=[pltpu.VMEM((NSA_KV_GROUPS * NSA_HEAD_DIM, NSA_HPG * tq), F32)],
        compiler_params=_cparams("parallel", "parallel"),
        name="nsa_cmp_attn_select",
    )(qkv, kcmp, vcmp)


ATT_TILE = 256
AUG_SLC = 2 * LANE
AUG_WIN = LANE


def _attn_prep_kernel(ks_ref, kw_ref, kas_ref, kaw_ref, *, ts):
    dh = NSA_HEAD_DIM
    s0 = pl.program_id(1) * ts
    pos = s0 + lax.broadcasted_iota(jnp.int32, (ts, dh), 0)
    lane = lax.broadcasted_iota(jnp.int32, (ts, dh), 1)
    onehot = jnp.where(lane == pos // SLC_BLOCK, 1.0, 0.0)
    off = (pos % ATT_TILE).astype(F32)
    cols = jnp.where(lane == 0, 1.0, jnp.where(lane <= 2, off, 0.0))
    zeros = jnp.zeros((ts, dh), F32)
    for g in range(NSA_KV_GROUPS):
        ks = ks_ref[:, g * dh:(g + 1) * dh]
        kas_ref[0, g] = jnp.concatenate([ks, onehot, cols, zeros], axis=1).astype(BF16)
        kw = kw_ref[:, g * dh:(g + 1) * dh]
        kaw_ref[0, g] = jnp.concatenate([kw, cols], axis=1).astype(BF16)


def _attn_prep(qkv, b, s, ts):
    kcol = NSA_WIDTH // LANE + 2
    return pl.pallas_call(
        functools.partial(_attn_prep_kernel, ts=ts),
        grid=(b, s // ts),
        in_specs=[pl.BlockSpec((ts, LANE), lambda bi, i: (bi * (s // ts) + i, kcol)),
                  pl.BlockSpec((ts, LANE), lambda bi, i: (bi * (s // ts) + i, kcol + 2))],
        out_specs=[pl.BlockSpec((1, NSA_KV_GROUPS, ts, AUG_SLC), lambda bi, i: (bi, 0, i, 0)),
                   pl.BlockSpec((1, NSA_KV_GROUPS, ts, AUG_WIN), lambda bi, i: (bi, 0, i, 0))],
        out_shape=[jax.ShapeDtypeStruct((b, NSA_KV_GROUPS, s, AUG_SLC), BF16),
                   jax.ShapeDtypeStruct((b, NSA_KV_GROUPS, s, AUG_WIN), BF16)],
        compiler_params=_cparams("parallel", "parallel"),
        name="nsa_key_augment",
    )(qkv, qkv)


def _flash_kernel(*refs, mode, tq):
    if mode == "slc":
        q_ref, ka_ref, v_ref, sel_ref, o_ref, qa_scr, m_scr, l_scr, acc_scr = refs
    else:
        q_ref, ka_ref, v_ref, o_ref, qa_scr, m_scr, l_scr, acc_scr = refs
        sel_ref = None
    dh = NSA_HEAD_DIM
    tk = tq
    rows = NSA_HPG * tq
    i = pl.program_id(1)
    q0 = i * tq
    qid = lax.broadcasted_iota(jnp.int32, (tk, rows), 1) % tq
    kid = lax.broadcasted_iota(jnp.int32, (tk, rows), 0)
    head_of_row = lax.broadcasted_iota(jnp.int32, (1, rows), 1) // tq
    rq = lax.broadcasted_iota(jnp.int32, (tq, dh), 0).astype(F32)
    lq = lax.broadcasted_iota(jnp.int32, (tq, dh), 1)

    slope_rows = []
    for g in range(NSA_KV_GROUPS):
        slopes = [LOG2E * SLOPES[g * NSA_HPG + h] for h in range(NSA_HPG)]
        slope_row = jnp.full((1, rows), slopes[-1], F32)
        for h in range(NSA_HPG - 2, -1, -1):
            slope_row = jnp.where(head_of_row == h, slopes[h], slope_row)
        slope_rows.append(slope_row)
        blocks = []
        for h in range(NSA_HPG):
            hh = g * NSA_HPG + h
            qh = q_ref[:, hh * dh:(hh + 1) * dh] * (LOG2E / math.sqrt(dh))
            cols = jnp.where(lq == 0, -slopes[h] * rq, _slope_pair(slopes[h], lq, 1))
            if mode == "slc":
                n_slc = sel_ref.shape[-1]
                notsel = (sel_ref[g] - 1.0) * (-NEG)
                if n_slc < dh:
                    notsel = jnp.concatenate([notsel, jnp.zeros((tq, dh - n_slc), F32)], axis=1)
                parts = [qh, notsel, cols, jnp.zeros((tq, dh), F32)]
            else:
                parts = [qh, cols]
            blocks.append(jnp.concatenate(parts, axis=1))
        qa_scr[g] = jnp.concatenate(blocks, axis=0).astype(BF16)
    m_scr[...] = jnp.full(m_scr.shape, NEG, F32)
    l_scr[...] = jnp.zeros(l_scr.shape, F32)
    acc_scr[...] = jnp.zeros(acc_scr.shape, F32)

    def tiles(specs):
        for g in range(NSA_KV_GROUPS):
            sts, shifts, vgs = [], [], []
            for kt, mask in specs:
                k0 = pl.multiple_of(kt * tk, tk)
                ka = ka_ref[0, g, pl.ds(k0, tk), :]
                vgs.append(v_ref[pl.ds(k0, tk), g * dh:(g + 1) * dh].astype(BF16))
                st = lax.dot_general(ka, qa_scr[g], _NT, preferred_element_type=F32)
                sts.append(st if mask is None else jnp.where(mask, st, NEG))
                shifts.append(slope_rows[g] * jnp.asarray(k0 - q0, F32))
            m_old = m_scr[g]
            m_new = m_old
            for st, shift in zip(sts, shifts):
                m_new = jnp.maximum(m_new, jnp.max(st, axis=0, keepdims=True) + shift)
            alpha = jnp.exp2(m_old - m_new)
            l_new = alpha * l_scr[g]
            acc = alpha * acc_scr[g * dh:(g + 1) * dh, :]
            for st, shift, vg in zip(sts, shifts, vgs):
                p = jnp.exp2(st - (m_new - shift))
                l_new = l_new + jnp.sum(p, axis=0, keepdims=True)
                acc = acc + lax.dot_general(vg, p.astype(BF16), _TN, preferred_element_type=F32)
            l_scr[g] = l_new
            acc_scr[g * dh:(g + 1) * dh, :] = acc
            m_scr[g] = m_new

    diag = kid <= qid
    if mode == "slc":
        def body(k2, carry):
            tiles([(2 * k2, None), (2 * k2 + 1, None)])
            return carry
        lax.fori_loop(0, i // 2, body, 0)

        @pl.when(i % 2 == 1)
        def _():
            tiles([(i - 1, None), (i, diag)])

        @pl.when(i % 2 == 0)
        def _():
            tiles([(i, diag)])
    else:
        @pl.when(i == 0)
        def _():
            tiles([(i, diag)])

        @pl.when(i == 1)
        def _():
            tiles([(i, diag), (i - 1, None)])

        @pl.when(i >= 2)
        def _():
            tiles([(i, diag), (i - 1, None), (i - 2, kid > qid)])

    for g in range(NSA_KV_GROUPS):
        acc_scr[g * dh:(g + 1) * dh, :] = acc_scr[g * dh:(g + 1) * dh, :] / l_scr[g]

    out = acc_scr[...].T
    for g in range(NSA_KV_GROUPS):
        for h in range(NSA_HPG):
            hh = g * NSA_HPG + h
            o_ref[:, hh * dh:(hh + 1) * dh] = out[h * tq:(h + 1) * tq, g * dh:(g + 1) * dh]


def _flash(qkv, kaug, sel, b, s, mode):
    tq = ATT_TILE
    assert WINDOW == 2 * tq and s % tq == 0
    t = b * s
    nq = s // tq
    aug = kaug.shape[-1]
    vcol = NSA_WIDTH // LANE + {"slc": 3, "win": 5}[mode]
    in_specs = [pl.BlockSpec((tq, NSA_WIDTH), lambda bi, i: (bi * nq + i, 0)),
                pl.BlockSpec((1, NSA_KV_GROUPS, s, aug), lambda bi, i: (bi, 0, 0, 0)),
                pl.BlockSpec((s, LANE), lambda bi, i: (bi, vcol))]
    args = [qkv, kaug, qkv]
    if mode == "slc":
        assert sel.shape[-1] <= NSA_HEAD_DIM
        in_specs.append(pl.BlockSpec((NSA_KV_GROUPS, tq, sel.shape[-1]), lambda bi, i: (0, bi * nq + i, 0)))
        args.append(sel)
    rows = NSA_HPG * tq
    return pl.pallas_call(
        functools.partial(_flash_kernel, mode=mode, tq=tq),
        grid=(b, nq),
        in_specs=in_specs,
        out_specs=pl.BlockSpec((tq, NSA_WIDTH), lambda bi, i: (bi * nq + i, 0)),
        out_shape=jax.ShapeDtypeStruct((t, NSA_WIDTH), F32),
        scratch_shapes=[pltpu.VMEM((NSA_KV_GROUPS, rows, aug), BF16),
                        pltpu.VMEM((NSA_KV_GROUPS, 1, rows), F32),
                        pltpu.VMEM((NSA_KV_GROUPS, 1, rows), F32),
                        pltpu.VMEM((NSA_KV_GROUPS * NSA_HEAD_DIM, rows), F32)],
        compiler_params=_cparams("parallel", "parallel"),
        name="nsa_flash_" + mode,
    )(*args)


HALO = 32
CONV_SHIFTS = 8


def _mix_kernel(ocmp_ref, oslc_ref, owin_ref, gates_ref, glu_ref, halo_ref, merge_ref, x_ref,
                wexp_ref, wnsa_ref, wdw_ref, bdw_ref, gln_ref, bln_ref, wconv_ref, wo_ref,
                o_ref, uext_scr, *, tm, tiles_per_seq):
    i = pl.program_id(0)
    gts = _sigmoid(gates_ref[...])
    g_hi = gts.astype(BF16)
    r1 = gts - g_hi.astype(F32)
    g_mid = r1.astype(BF16)
    g_lo = (r1 - g_mid.astype(F32)).astype(BF16)
    wexp = wexp_ref[...]
    gexp = (jnp.dot(g_hi, wexp, preferred_element_type=F32) + jnp.dot(g_mid, wexp, preferred_element_type=F32)
            + jnp.dot(g_lo, wexp, preferred_element_type=F32))
    w = NSA_WIDTH
    o_nsa = gexp[:, :w] * ocmp_ref[...] + gexp[:, w:2 * w] * oslc_ref[...] + gexp[:, 2 * w:] * owin_ref[...]
    y_a = jnp.dot(o_nsa.astype(BF16), wnsa_ref[...], preferred_element_type=F32)

    c = CONV_CH
    gl = glu_ref[...]
    u = gl[:, :c] * _sigmoid(gl[:, c:])
    hl = halo_ref[...]
    uh = hl[:, :c] * _sigmoid(hl[:, c:])
    uh = jnp.where(i % tiles_per_seq == 0, 0.0, uh)
    uext_scr[0, 0:HALO, :] = uh
    uext_scr[0, HALO:HALO + tm, :] = u
    span = HALO + tm - CONV_SHIFTS
    for j in range(1, CONV_SHIFTS):
        uext_scr[j, 0:span, :] = uext_scr[0, pl.ds(j, span), :]
    acc = jnp.zeros((tm, c), F32)
    for k in range(CONV_WIDTH):
        first = HALO - (CONV_WIDTH - 1) + k
        acc = acc + uext_scr[first % CONV_SHIFTS, pl.ds(first - first % CONV_SHIFTS, tm), :] * wdw_ref[k:k + 1, :]
    cv = acc + bdw_ref[...]
    mu = jnp.mean(cv, axis=-1, keepdims=True)
    var = jnp.mean(jnp.square(cv - mu), axis=-1, keepdims=True)
    un = (cv - mu) * lax.rsqrt(var + EPS) * gln_ref[...] + bln_ref[...]
    act = un * _sigmoid(un)
    y_b = jnp.dot(act.astype(BF16), wconv_ref[...], preferred_element_type=F32)

    d = x_ref.shape[-1]
    mg = merge_ref[...]
    z = _sigmoid(mg[:, :d]) * y_a + _sigmoid(mg[:, d:]) * y_b
    o_ref[...] = x_ref[...] + jnp.dot(z.astype(BF16), wo_ref[...], preferred_element_type=F32)


def _mix(ocmp, oslc, owin, gates, glu, merge, x2, wexp, wnsa, wdw, bdw, gln, bln, wconv, wo, s, tm):
    t, d = x2.shape
    row = lambda n: pl.BlockSpec((tm, n), lambda i: (i, 0))
    full = lambda a: pl.BlockSpec(a.shape, lambda i: tuple(0 for _ in a.shape))
    halo_spec = pl.BlockSpec((HALO, glu.shape[1]), lambda i: (jnp.maximum(i * (tm // HALO) - 1, 0), 0))
    weights = [wexp, wnsa, wdw, bdw, gln, bln, wconv, wo]
    return pl.pallas_call(
        functools.partial(_mix_kernel, tm=tm, tiles_per_seq=s // tm),
        grid=(t // tm,),
        in_specs=[row(NSA_WIDTH), row(NSA_WIDTH), row(NSA_WIDTH), row(GATE_PAD), row(glu.shape[1]),
                  halo_spec, row(merge.shape[1]), row(d)] + [full(a) for a in weights],
        out_specs=row(d),
        out_shape=jax.ShapeDtypeStruct((t, d), F32),
        scratch_shapes=[pltpu.VMEM((CONV_SHIFTS, HALO + tm, CONV_CH), F32)],
        compiler_params=_cparams("parallel"),
        name="mixer_merge",
    )(ocmp, oslc, owin, gates, glu, glu, merge, x2, *weights)


def _extract_topk(jobs, k_top):
    def body(k, carry):
        for cur_ref, rank_ref, val_ref in jobs:
            cur = cur_ref[...]
            n_rows = cur.shape[0]
            rid = lax.broadcasted_iota(jnp.int32, cur.shape, 0).astype(F32)
            v = jnp.max(cur, axis=0, keepdims=True)
            idx = jnp.min(jnp.where(cur == v, rid, float(n_rows)), axis=0, keepdims=True)
            hit = rid == idx
            rank_ref[...] = jnp.where(hit, jnp.asarray(k, F32), rank_ref[...])
            cur_ref[...] = jnp.where(hit, -jnp.inf, cur)
            val_ref[pl.ds(k, 1), :] = v
        return carry

    lax.fori_loop(0, k_top, body, 0)


def _peer_stats_kernel(x_ref, g_ref, wq_ref, keys_ref, xn_ref, e1_ref, n1_ref, e2_ref, r2_ref,
                       c1_scr, c2_scr, r1_scr, r2_scr, v1_scr, v2_scr, cand_scr, csel_scr, cval_scr, *, tm):
    nk = PEER_NKEYS
    kt = PEER_TOPK
    x = x_ref[...]
    xn = (x * lax.rsqrt(jnp.mean(x * x, axis=-1, keepdims=True) + EPS) * g_ref[...]).astype(BF16)
    xn_ref[...] = xn
    big = float(nk)
    n_cand = cand_scr.shape[0]

    def head(h, carry):
        qp = jnp.dot(xn, wq_ref[h], preferred_element_type=F32)
        s1 = lax.dot_general(keys_ref[h, 0], qp[:, :PEER_HALF].astype(BF16), _NT,
                             preferred_element_type=F32)
        s2 = lax.dot_general(keys_ref[h, 1], qp[:, PEER_HALF:].astype(BF16), _NT,
                             preferred_element_type=F32)
        c1_scr[...] = s1
        c2_scr[...] = s2
        r1_scr[...] = jnp.full((nk, tm), big, F32)
        r2_scr[...] = jnp.full((nk, tm), big, F32)
        _extract_topk([(c1_scr, r1_scr, v1_scr), (c2_scr, r2_scr, v2_scr)], kt)
        v1 = v1_scr[...]
        v2 = v2_scr[...]
        pieces = [v1[a:a + 1, :] + v2[0:kt // (a + 1), :] for a in range(kt)]
        pad = n_cand - sum(kt // (a + 1) for a in range(kt))
        pieces.append(jnp.full((pad, tm), -jnp.inf, F32))
        cand = jnp.concatenate(pieces, axis=0)
        cand_scr[...] = cand
        csel_scr[...] = jnp.full((n_cand, tm), big, F32)
        _extract_topk([(cand_scr, csel_scr, cval_scr)], kt)
        sel_f = jnp.where(csel_scr[...] < big, 1.0, 0.0)
        top = v1[0:1, :] + v2[0:1, :]
        z = jnp.sum(sel_f * jnp.exp(cand - top), axis=0, keepdims=True)
        r1 = r1_scr[...]
        n1 = jnp.zeros((nk, tm), F32)
        off = 0
        for a in range(kt):
            cnt = kt // (a + 1)
            n_a = jnp.sum(sel_f[off:off + cnt, :], axis=0, keepdims=True)
            off += cnt
            n1 = jnp.where(r1 == float(a), n_a, n1)
        e1_ref[h] = jnp.exp(s1 - v1[0:1, :])
        n1_ref[h] = n1
        e2_ref[h] = (0.5 * jnp.exp(s2 - v2[0:1, :])) / z
        r2_ref[h] = r2_scr[...]
        return carry

    lax.fori_loop(0, PEER_HEADS, head, 0)


def _peer_stats(x1, g, wq_h, keys_bf16, tm):
    t, d = x1.shape
    nk, kt = PEER_NKEYS, PEER_TOPK
    stat_spec = pl.BlockSpec((PEER_HEADS, nk, tm), lambda i: (0, 0, i))
    stat_shape = jax.ShapeDtypeStruct((PEER_HEADS, nk, t), F32)
    full = lambda a: pl.BlockSpec(a.shape, lambda i: tuple(0 for _ in a.shape))
    n_cells = sum(kt // (a + 1) for a in range(kt))
    n_cand = -(-n_cells // 8) * 8
    return pl.pallas_call(
        functools.partial(_peer_stats_kernel, tm=tm),
        grid=(t // tm,),
        in_specs=[pl.BlockSpec((tm, d), lambda i: (i, 0)), pl.BlockSpec((1, d), lambda i: (0, 0)),
                  full(wq_h), full(keys_bf16)],
        out_specs=[pl.BlockSpec((tm, d), lambda i: (i, 0))] + [stat_spec] * 4,
        out_shape=[jax.ShapeDtypeStruct((t, d), BF16)] + [stat_shape] * 4,
        scratch_shapes=[pltpu.VMEM((nk, tm), F32), pltpu.VMEM((nk, tm), F32),
                        pltpu.VMEM((nk, tm), F32), pltpu.VMEM((nk, tm), F32),
                        pltpu.VMEM((kt, tm), F32), pltpu.VMEM((kt, tm), F32),
                        pltpu.VMEM((n_cand, tm), F32), pltpu.VMEM((n_cand, tm), F32),
                        pltpu.VMEM((kt, tm), F32)],
        compiler_params=_cparams("parallel"),
        name="peer_stats",
    )(x1, g.reshape(1, d), wq_h, keys_bf16)


PEER_CHUNK = 256


def _peer_dense_kernel(xn_ref, u_ref, vt_ref, e1_ref, n1_ref, e2_ref, r2_ref, o_ref, ht_scr, act_scr,
                       *, tm, eb, nb):
    nk = PEER_NKEYS
    s = pl.program_id(0)
    n_keys = eb // nk
    slot = s % 2
    other = 1 - slot

    @pl.when(s == 0)
    def _():
        ht_scr[...] = jnp.zeros(ht_scr.shape, F32)

    @pl.when((s < 2) | ((s - 1) % nb == 0))
    def _():
        o_ref[...] = jnp.zeros(o_ref.shape, F32)

    jb = jnp.maximum(s - 1, 0) % nb
    n1_rows = [[n1_ref[h, pl.ds(jb * n_keys + c, 1), :] for c in range(n_keys)] for h in range(PEER_HEADS)]
    e1_rows = [[e1_ref[h, pl.ds(jb * n_keys + c, 1), :] for c in range(n_keys)] for h in range(PEER_HEADS)]
    ht = ht_scr[other].astype(BF16)
    act_scr[...] = ht * (1.0 + lax.erf(ht * (1.0 / math.sqrt(2.0))))
    pk = BF16_ROWS
    for ch in range(tm // PEER_CHUNK):
        cols = slice(ch * PEER_CHUNK, (ch + 1) * PEER_CHUNK)
        ht_scr[slot, :, cols] = lax.dot_general(u_ref[...], xn_ref[cols, :], _NT, preferred_element_type=F32)
        parts = []
        for lt in range(PEER_CHUNK // LANE):
            lanes = slice(ch * PEER_CHUNK + lt * LANE, ch * PEER_CHUNK + (lt + 1) * LANE)
            grp = [jnp.zeros((nk // pk, pk, LANE), BF16) for _ in range(n_keys)]
            for h in range(PEER_HEADS):
                r2 = r2_ref[h, :, lanes].astype(BF16).reshape(nk // pk, pk, LANE)
                e2 = e2_ref[h, :, lanes].astype(BF16).reshape(nk // pk, pk, LANE)
                for c in range(n_keys):
                    n1 = jnp.broadcast_to(n1_rows[h][c][:, lanes], (pk, LANE)).astype(BF16)
                    e1 = jnp.broadcast_to(e1_rows[h][c][:, lanes], (pk, LANE)).astype(BF16)
                    grp[c] = grp[c] + jnp.where(r2 < n1, e2, jnp.zeros_like(e2)) * e1
            gate = jnp.concatenate([g.reshape(nk, LANE) for g in grp], axis=0)
            parts.append(act_scr[:, lanes] * gate)
        at = jnp.concatenate(parts, axis=1)
        o_ref[:, cols] += jnp.dot(vt_ref[...], at, preferred_element_type=F32)


def _peer_dense(xn, u_bf16, vt_bf16, e1, n1, e2, r2, tm, eb):
    t, d = xn.shape
    n_exp = u_bf16.shape[0]
    nb = n_exp // eb
    n_items = (t // tm) * nb
    item = lambda s, lag: jnp.clip(s - lag, 0, n_items - 1)
    stat_spec = pl.BlockSpec((PEER_HEADS, PEER_NKEYS, tm), lambda s: (0, 0, item(s, 1) // nb),
                             pipeline_mode=pl.Buffered(1))
    return pl.pallas_call(
        functools.partial(_peer_dense_kernel, tm=tm, eb=eb, nb=nb),
        grid=(n_items + 1,),
        in_specs=[pl.BlockSpec((tm, d), lambda s: (item(s, 0) // nb, 0)),
                  pl.BlockSpec((eb, d), lambda s: (item(s, 0) % nb, 0)),
                  pl.BlockSpec((None, d, eb), lambda s: (item(s, 1) % nb, 0, 0)),
                  stat_spec, stat_spec, stat_spec, stat_spec],
        out_specs=pl.BlockSpec((d, tm), lambda s: (0, item(s, 1) // nb)),
        out_shape=jax.ShapeDtypeStruct((d, t), F32),
        scratch_shapes=[pltpu.VMEM((2, eb, tm), F32), pltpu.VMEM((eb, tm), BF16)],
        compiler_params=_cparams("arbitrary"),
        name="peer_dense",
    )(xn, u_bf16, vt_bf16, e1, n1, e2, r2)


def _final_kernel(x1_ref, yt_ref, g_ref, o_ref):
    y = x1_ref[...] + yt_ref[...].T
    o_ref[...] = y * lax.rsqrt(jnp.mean(y * y, axis=-1, keepdims=True) + EPS) * g_ref[...]


def _final_norm(x1, yt, g_final, tm):
    t, d = x1.shape
    return pl.pallas_call(
        _final_kernel,
        grid=(t // tm,),
        in_specs=[pl.BlockSpec((tm, d), lambda i: (i, 0)), pl.BlockSpec((d, tm), lambda i: (0, i)),
                  pl.BlockSpec((1, d), lambda i: (0, 0))],
        out_specs=pl.BlockSpec((tm, d), lambda i: (i, 0)),
        out_shape=jax.ShapeDtypeStruct((t, d), F32),
        compiler_params=_cparams("parallel"),
        name="peer_residual_final_norm",
    )(x1, yt, g_final.reshape(1, d))


def _gate_expand_matrix():
    r = jnp.arange(GATE_PAD)[:, None]
    c = jnp.arange(3 * NSA_WIDTH)[None, :]
    return ((r < GATE_COLS) & (r == (c // NSA_WIDTH) * NSA_HEADS + (c % NSA_WIDTH) // NSA_HEAD_DIM)).astype(F32)


class _Tiles:
    proj = 256
    attn = ATT_TILE
    key_prep = 512
    mix = 256
    stats = 512
    dense = 1024
    experts = 1024
    final = 512


def kernel(x, g_mix, w_in, pe_cmp_k, pe_cmp_v, w_cmp_k1, w_cmp_k2, w_cmp_v1, w_cmp_v2, w_nsa_out, w_dw, b_dw,
           g_conv_ln, b_conv_ln, w_conv_out, w_o, g_ffn, w_peer_q, peer_sub_keys, peer_u, peer_v, g_final):
    b, s, d = x.shape
    t = b * s
    depth = w_in.shape[0]
    assert depth == 1, "the fused final norm assumes a single layer"
    assert s % 512 == 0 and d % LANE == 0 and t % _Tiles.dense == 0
    x2 = x.reshape(t, d)
    wexp = _gate_expand_matrix().astype(BF16)
    l = 0
    gate_end = QKV_COLS + GATE_COLS
    w_pad = jnp.concatenate([w_in[l][:, :gate_end], jnp.zeros((d, GATE_PAD - GATE_COLS), F32),
                             w_in[l][:, gate_end:]], axis=1).astype(BF16)
    splits = (QKV_COLS, GATE_PAD, 2 * CONV_CH, 2 * d)
    qkv, gates, glu, merge = _norm_proj(x2, g_mix[l], w_pad, splits, tm=_Tiles.proj)

    kcmp, vcmp = _compress(qkv, b, s, pe_cmp_k[l], pe_cmp_v[l], w_cmp_k1[l], w_cmp_k2[l],
                           w_cmp_v1[l], w_cmp_v2[l])
    o_cmp, sel = _cmp_attn(qkv, kcmp, vcmp, b, s, tq=_Tiles.attn)
    kaug_slc, kaug_win = _attn_prep(qkv, b, s, ts=_Tiles.key_prep)
    o_slc = _flash(qkv, kaug_slc, sel, b, s, "slc")
    o_win = _flash(qkv, kaug_win, None, b, s, "win")

    x1 = _mix(o_cmp, o_slc, o_win, gates, glu, merge, x2, wexp,
              w_nsa_out[l].astype(BF16), w_dw[l].reshape(CONV_WIDTH, CONV_CH), b_dw[l].reshape(1, -1),
              g_conv_ln[l].reshape(1, -1), b_conv_ln[l].reshape(1, -1),
              w_conv_out[l].astype(BF16), w_o[l].astype(BF16), s, tm=_Tiles.mix)

    wq_h = w_peer_q[l].reshape(d, PEER_HEADS, PEER_QDIM).transpose(1, 0, 2).astype(BF16)
    xn, e1, n1, e2, r2 = _peer_stats(x1, g_ffn[l], wq_h, peer_sub_keys[l].astype(BF16), tm=_Tiles.stats)
    eb = _Tiles.experts
    n_exp = peer_v.shape[1]
    vt_blocks = peer_v[l].astype(BF16).reshape(n_exp // eb, eb, d).transpose(0, 2, 1)
    yt = _peer_dense(xn, peer_u[l].astype(BF16), vt_blocks, e1, n1, e2, r2, tm=_Tiles.dense, eb=eb)
    out = _final_norm(x1, yt, g_final, tm=_Tiles.final)
    return out.reshape(b, s, d)
```

```python
import functools
import math

import jax
import jax.numpy as jnp
from jax import lax
from jax.experimental import pallas as pl
from jax.experimental.pallas import tpu as pltpu

F32 = jnp.float32
BF16 = jnp.bfloat16

NSA_HEADS = 8
NSA_KV_GROUPS = 2
NSA_HPG = NSA_HEADS // NSA_KV_GROUPS
NSA_HEAD_DIM = 64
NSA_WIDTH = NSA_HEADS * NSA_HEAD_DIM
NSA_KV_WIDTH = NSA_KV_GROUPS * NSA_HEAD_DIM
CMP_BLOCK = 32
CMP_STRIDE = 16
SLC_BLOCK = 64
SLC_TOPN = 16
WINDOW = 512
CONV_CH = 512
CONV_WIDTH = 31
PEER_HEADS = 8
PEER_NKEYS = 128
PEER_QDIM = 256
PEER_HALF = PEER_QDIM // 2
PEER_TOPK = 16
EPS = 1e-6
NEG = -1e30
FORCED = 1e9
SLOPES = tuple(2.0 ** (-8.0 * (h + 1) / NSA_HEADS) for h in range(NSA_HEADS))
LOG2E = math.log2(math.e)

LANE = 128
BF16_ROWS = 16
GATE_COLS = 3 * NSA_HEADS
GATE_PAD = LANE
QKV_COLS = NSA_WIDTH + 6 * NSA_KV_WIDTH
VMEM_LIMIT = 56 * 1024 * 1024

_NT = (((1,), (1,)), ((), ()))
_TN = (((0,), (0,)), ((), ()))


def _cparams(*sem, fuse=None):
    return pltpu.CompilerParams(dimension_semantics=sem, vmem_limit_bytes=VMEM_LIMIT, allow_input_fusion=fuse)


def _sigmoid(x):
    return 1.0 / (1.0 + jnp.exp(-x))


def _slope_pair(slope, lane, first):
    s = jnp.full(lane.shape, slope, F32)
    hi = s.astype(BF16).astype(F32)
    return jnp.where(lane == first, hi, jnp.where(lane == first + 1, s - hi, 0.0))


def _gelu(x):
    return 0.5 * x * (1.0 + lax.erf(x * (1.0 / math.sqrt(2.0))))


def _norm_proj_kernel(x_ref, g_ref, w_ref, *out_refs, splits):
    x = x_ref[...]
    xn = x * lax.rsqrt(jnp.mean(x * x, axis=-1, keepdims=True) + EPS) * g_ref[...]
    xb = xn.astype(BF16)
    off = 0
    for o_ref, n in zip(out_refs, splits):
        o_ref[...] = jnp.dot(xb, w_ref[:, off:off + n], preferred_element_type=F32)
        off += n


def _norm_proj(x2, g, w_bf16, splits, tm):
    t, d = x2.shape
    n = w_bf16.shape[1]
    return pl.pallas_call(
        functools.partial(_norm_proj_kernel, splits=splits),
        grid=(t // tm,),
        in_specs=[pl.BlockSpec((tm, d), lambda i: (i, 0)),
                  pl.BlockSpec((1, d), lambda i: (0, 0)),
                  pl.BlockSpec((d, n), lambda i: (0, 0))],
        out_specs=[pl.BlockSpec((tm, s), lambda i: (i, 0)) for s in splits],
        out_shape=[jax.ShapeDtypeStruct((t, s), F32) for s in splits],
        compiler_params=_cparams("parallel", fuse=[False, False, True]),
        name="norm_in_proj",
    )(x2, g.reshape(1, d), w_bf16)


def _compress_kernel(k_ref, v_ref, pek_ref, pev_ref, wk1_ref, wk2_ref, wv1_ref, wv2_ref,
                     kc_ref, vc_ref, *, n_blk):
    hp = lax.Precision.HIGHEST
    dh = NSA_HEAD_DIM
    per = CMP_BLOCK // CMP_STRIDE
    assert per == 2

    def one(src_ref, pe_ref, w1_ref, w2_ref, dst_ref):
        pe8 = jnp.broadcast_to(pe_ref[...], (8, CMP_BLOCK * dh))
        const = jnp.dot(pe8, w1_ref[...], precision=hp, preferred_element_type=F32)[0:1, :]
        acc = [[jnp.zeros((n_blk, dh), F32) for _ in range(per)] for _ in range(NSA_KV_GROUPS)]
        for lo in range(CMP_STRIDE):
            rows = src_ref[pl.ds(lo, n_blk, stride=CMP_STRIDE), :]
            for g in range(NSA_KV_GROUPS):
                rg = rows[:, g * dh:(g + 1) * dh]
                for hi in range(per):
                    l = hi * CMP_STRIDE + lo
                    acc[g][hi] = acc[g][hi] + jnp.dot(rg, w1_ref[l * dh:(l + 1) * dh, :], precision=hp,
                                                     preferred_element_type=F32)
        for g in range(NSA_KV_GROUPS):
            pre = acc[g][0] + pltpu.roll(acc[g][1], n_blk - 1, 0) + const
            out = jnp.dot(_gelu(pre), w2_ref[...], precision=hp, preferred_element_type=F32)
            rid = lax.broadcasted_iota(jnp.int32, (n_blk, dh), 0)
            dst_ref[0, g] = jnp.where(rid < n_blk - 1, out, 0.0)

    one(k_ref, pek_ref, wk1_ref, wk2_ref, kc_ref)
    one(v_ref, pev_ref, wv1_ref, wv2_ref, vc_ref)


def _compress(qkv, b, s, pe_k, pe_v, wk1, wk2, wv1, wv2):
    n_blk = s // CMP_STRIDE
    dh = NSA_HEAD_DIM
    kcol = NSA_WIDTH // LANE
    full = lambda shape: pl.BlockSpec(shape, lambda i: tuple(0 for _ in shape))
    out_spec = pl.BlockSpec((1, NSA_KV_GROUPS, n_blk, dh), lambda i: (i, 0, 0, 0))
    out_shape = jax.ShapeDtypeStruct((b, NSA_KV_GROUPS, n_blk, dh), F32)
    return pl.pallas_call(
        functools.partial(_compress_kernel, n_blk=n_blk),
        grid=(b,),
        in_specs=[pl.BlockSpec((s, LANE), lambda i: (i, kcol)),
                  pl.BlockSpec((s, LANE), lambda i: (i, kcol + 1)),
                  full((1, CMP_BLOCK * dh)), full((1, CMP_BLOCK * dh)),
                  full((CMP_BLOCK * dh, dh)), full((dh, dh)),
                  full((CMP_BLOCK * dh, dh)), full((dh, dh))],
        out_specs=[out_spec, out_spec],
        out_shape=[out_shape, out_shape],
        compiler_params=_cparams("parallel"),
        name="nsa_compress",
    )(qkv, qkv, pe_k.reshape(1, -1), pe_v.reshape(1, -1), wk1, wk2, wv1, wv2)


def _cmp_attn_kernel(q_ref, kc_ref, vc_ref, o_ref, sel_ref, ot_scr, *, tq, n_blk, n_slc):
    hp = lax.Precision.HIGHEST
    dh = NSA_HEAD_DIM
    sub = 8
    rows = NSA_HPG * tq
    q0 = pl.program_id(1) * tq
    n_id = lax.broadcasted_iota(jnp.int32, (n_blk, rows), 0)
    t_id = q0 + lax.broadcasted_iota(jnp.int32, (n_blk, rows), 1) % tq
    mask = n_id * CMP_STRIDE + (CMP_BLOCK - 1) <= t_id
    jj = lax.broadcasted_iota(jnp.int32, (n_slc, n_blk), 0) * SLC_BLOCK
    nn = lax.broadcasted_iota(jnp.int32, (n_slc, n_blk), 1) * CMP_STRIDE
    overlap_t = jnp.where((nn <= jj + SLC_BLOCK - 1) & (nn + CMP_BLOCK - 1 >= jj), 1.0, 0.0)
    jb = lax.broadcasted_iota(jnp.int32, (n_slc, tq), 0)
    t_blk = (q0 + lax.broadcasted_iota(jnp.int32, (n_slc, tq), 1)) // SLC_BLOCK
    future = jb > t_blk
    forced = (jb == 0) | (jb == t_blk) | (jb == t_blk - 1)
    n_sel = min(SLC_TOPN, n_slc)
    lk = lax.broadcasted_iota(jnp.int32, (n_blk, dh), 1)
    nk_f = (lax.broadcasted_iota(jnp.int32, (n_blk, dh), 0) * CMP_STRIDE).astype(F32)
    cols_k = jnp.where(lk <= 1, nk_f, 0.0)
    lq = lax.broadcasted_iota(jnp.int32, (tq, dh), 1)

    for g in range(NSA_KV_GROUPS):
        kca = jnp.concatenate([kc_ref[0, g], cols_k], axis=1).astype(BF16)
        vc = vc_ref[0, g].astype(BF16)
        blocks = []
        for h in range(NSA_HPG):
            hh = g * NSA_HPG + h
            qh = q_ref[:, hh * dh:(hh + 1) * dh] * (LOG2E / math.sqrt(dh))
            blocks.append(jnp.concatenate([qh, _slope_pair(LOG2E * SLOPES[hh], lq, 0)], axis=1))
        qa = jnp.concatenate(blocks, axis=0).astype(BF16)
        st = lax.dot_general(kca, qa, _NT, preferred_element_type=F32)
        st = jnp.where(mask, st, NEG)
        m = jnp.max(st, axis=0, keepdims=True)
        e = jnp.where(mask, jnp.exp2(st - m), 0.0)
        l = jnp.sum(e, axis=0, keepdims=True)
        p = e / jnp.where(l > 0.0, l, 1.0)
        ot_scr[g * dh:(g + 1) * dh, :] = lax.dot_general(vc, p.astype(BF16), _TN, preferred_element_type=F32)
        psum = p[:, 0:tq]
        for h in range(1, NSA_HPG):
            psum = psum + p[:, h * tq:(h + 1) * tq]
        imp = jnp.dot(overlap_t, psum, precision=hp, preferred_element_type=F32)
        imp = jnp.where(forced, FORCED, jnp.where(future, NEG, imp))
        slabs = [imp[v * sub:(v + 1) * sub, :] for v in range(n_slc // sub)]
        cnts = [jnp.zeros((sub, tq), F32) for _ in slabs]
        jrow = lax.broadcasted_iota(jnp.int32, (sub, tq), 0)
        for j2 in range(n_slc):
            row = imp[j2:j2 + 1, :]
            for v, slab in enumerate(slabs):
                if v * sub > j2:
                    before = row >= slab
                elif v * sub + sub - 1 <= j2:
                    before = row > slab
                else:
                    before = jnp.where(jrow + v * sub > j2, jnp.where(row >= slab, 1.0, 0.0),
                                       jnp.where(row > slab, 1.0, 0.0)) > 0.5
                cnts[v] = cnts[v] + jnp.where(before, 1.0, 0.0)
        sel_t = jnp.where(jnp.concatenate(cnts, axis=0) < float(n_sel), 1.0, 0.0)
        if n_slc < LANE:
            sel_t = jnp.concatenate([sel_t, jnp.zeros((LANE - n_slc, tq), F32)], axis=0)
        sel_ref[g] = sel_t.T[:, :n_slc]

    out = ot_scr[...].T
    for g in range(NSA_KV_GROUPS):
        for h in range(NSA_HPG):
            hh = g * NSA_HPG + h
            o_ref[:, hh * dh:(hh + 1) * dh] = out[h * tq:(h + 1) * tq, g * dh:(g + 1) * dh]


def _cmp_attn(qkv, kcmp, vcmp, b, s, tq):
    t = b * s
    nq = s // tq
    n_blk = kcmp.shape[2]
    n_slc = s // SLC_BLOCK
    cmp_spec = pl.BlockSpec((1, NSA_KV_GROUPS, n_blk, NSA_HEAD_DIM), lambda bi, i: (bi, 0, 0, 0))
    return pl.pallas_call(
        functools.partial(_cmp_attn_kernel, tq=tq, n_blk=n_blk, n_slc=n_slc),
        grid=(b, nq),
        in_specs=[pl.BlockSpec((tq, NSA_WIDTH), lambda bi, i: (bi * nq + i, 0)), cmp_spec, cmp_spec],
        out_specs=[pl.BlockSpec((tq, NSA_WIDTH), lambda bi, i: (bi * nq + i, 0)),
                   pl.BlockSpec((NSA_KV_GROUPS, tq, n_slc), lambda bi, i: (0, bi * nq + i, 0))],
        out_shape=[jax.ShapeDtypeStruct((t, NSA_WIDTH), F32),
                   jax.ShapeDtypeStruct((NSA_KV_GROUPS, t, n_slc), F32)],
        scratch_shapes=[pltpu.VMEM((NSA_KV_GROUPS * NSA_HEAD_DIM, NSA_HPG * tq), F32)],
        compiler_params=_cparams("parallel", "parallel"),
        name="nsa_cmp_attn_select",
    )(qkv, kcmp, vcmp)


ATT_TILE = 256
AUG_SLC = 2 * LANE
AUG_WIN = LANE


def _attn_prep_kernel(ks_ref, kw_ref, kas_ref, kaw_ref, *, ts):
    dh = NSA_HEAD_DIM
    s0 = pl.program_id(1) * ts
    pos = s0 + lax.broadcasted_iota(jnp.int32, (ts, dh), 0)
    lane = lax.broadcasted_iota(jnp.int32, (ts, dh), 1)
    onehot = jnp.where(lane == pos // SLC_BLOCK, 1.0, 0.0)
    off = (pos % ATT_TILE).astype(F32)
    cols = jnp.where(lane == 0, 1.0, jnp.where(lane <= 2, off, 0.0))
    zeros = jnp.zeros((ts, dh), F32)
    for g in range(NSA_KV_GROUPS):
        ks = ks_ref[:, g * dh:(g + 1) * dh]
        kas_ref[0, g] = jnp.concatenate([ks, onehot, cols, zeros], axis=1).astype(BF16)
        kw = kw_ref[:, g * dh:(g + 1) * dh]
        kaw_ref[0, g] = jnp.concatenate([kw, cols], axis=1).astype(BF16)


def _attn_prep(qkv, b, s, ts):
    kcol = NSA_WIDTH // LANE + 2
    return pl.pallas_call(
        functools.partial(_attn_prep_kernel, ts=ts),
        grid=(b, s // ts),
        in_specs=[pl.BlockSpec((ts, LANE), lambda bi, i: (bi * (s // ts) + i, kcol)),
                  pl.BlockSpec((ts, LANE), lambda bi, i: (bi * (s // ts) + i, kcol + 2))],
        out_specs=[pl.BlockSpec((1, NSA_KV_GROUPS, ts, AUG_SLC), lambda bi, i: (bi, 0, i, 0)),
                   pl.BlockSpec((1, NSA_KV_GROUPS, ts, AUG_WIN), lambda bi, i: (bi, 0, i, 0))],
        out_shape=[jax.ShapeDtypeStruct((b, NSA_KV_GROUPS, s, AUG_SLC), BF16),
                   jax.ShapeDtypeStruct((b, NSA_KV_GROUPS, s, AUG_WIN), BF16)],
        compiler_params=_cparams("parallel", "parallel"),
        name="nsa_key_augment",
    )(qkv, qkv)


def _flash_kernel(*refs, mode, tq):
    if mode == "slc":
        q_ref, ka_ref, v_ref, sel_ref, o_ref, qa_scr, m_scr, l_scr, acc_scr = refs
    else:
        q_ref, ka_ref, v_ref, o_ref, qa_scr, m_scr, l_scr, acc_scr = refs
        sel_ref = None
    dh = NSA_HEAD_DIM
    tk = tq
    rows = NSA_HPG * tq
    i = pl.program_id(1)
    q0 = i * tq
    qid = lax.broadcasted_iota(jnp.int32, (tk, rows), 1) % tq
    kid = lax.broadcasted_iota(jnp.int32, (tk, rows), 0)
    head_of_row = lax.broadcasted_iota(jnp.int32, (1, rows), 1) // tq
    rq = lax.broadcasted_iota(jnp.int32, (tq, dh), 0).astype(F32)
    lq = lax.broadcasted_iota(jnp.int32, (tq, dh), 1)

    slope_rows = []
    for g in range(NSA_KV_GROUPS):
        slopes = [LOG2E * SLOPES[g * NSA_HPG + h] for h in range(NSA_HPG)]
        slope_row = jnp.full((1, rows), slopes[-1], F32)
        for h in range(NSA_HPG - 2, -1, -1):
            slope_row = jnp.where(head_of_row == h, slopes[h], slope_row)
        slope_rows.append(slope_row)
        blocks = []
        for h in range(NSA_HPG):
            hh = g * NSA_HPG + h
            qh = q_ref[:, hh * dh:(hh + 1) * dh] * (LOG2E / math.sqrt(dh))
            cols = jnp.where(lq == 0, -slopes[h] * rq, _slope_pair(slopes[h], lq, 1))
            if mode == "slc":
                n_slc = sel_ref.shape[-1]
                notsel = (sel_ref[g] - 1.0) * (-NEG)
                if n_slc < dh:
                    notsel = jnp.concatenate([notsel, jnp.zeros((tq, dh - n_slc), F32)], axis=1)
                parts = [qh, notsel, cols, jnp.zeros((tq, dh), F32)]
            else:
                parts = [qh, cols]
            blocks.append(jnp.concatenate(parts, axis=1))
        qa_scr[g] = jnp.concatenate(blocks, axis=0).astype(BF16)
    m_scr[...] = jnp.full(m_scr.shape, NEG, F32)
    l_scr[...] = jnp.zeros(l_scr.shape, F32)
    acc_scr[...] = jnp.zeros(acc_scr.shape, F32)

    def tiles(specs):
        for g in range(NSA_KV_GROUPS):
            sts, shifts, vgs = [], [], []
            for kt, mask in specs:
                k0 = pl.multiple_of(kt * tk, tk)
                ka = ka_ref[0, g, pl.ds(k0, tk), :]
                vgs.append(v_ref[pl.ds(k0, tk), g * dh:(g + 1) * dh].astype(BF16))
                st = lax.dot_general(ka, qa_scr[g], _NT, preferred_element_type=F32)
                sts.append(st if mask is None else jnp.where(mask, st, NEG))
                shifts.append(slope_rows[g] * jnp.asarray(k0 - q0, F32))
            m_old = m_scr[g]
            m_new = m_old
            for st, shift in zip(sts, shifts):
                m_new = jnp.maximum(m_new, jnp.max(st, axis=0, keepdims=True) + shift)
            alpha = jnp.exp2(m_old - m_new)
            l_new = alpha * l_scr[g]
            acc = alpha * acc_scr[g * dh:(g + 1) * dh, :]
            for st, shift, vg in zip(sts, shifts, vgs):
                p = jnp.exp2(st - (m_new - shift))
                l_new = l_new + jnp.sum(p, axis=0, keepdims=True)
                acc = acc + lax.dot_general(vg, p.astype(BF16), _TN, preferred_element_type=F32)
            l_scr[g] = l_new
            acc_scr[g * dh:(g + 1) * dh, :] = acc
            m_scr[g] = m_new

    diag = kid <= qid
    if mode == "slc":
        def body(k2, carry):
            tiles([(2 * k2, None), (2 * k2 + 1, None)])
            return carry
        lax.fori_loop(0, i // 2, body, 0)

        @pl.when(i % 2 == 1)
        def _():
            tiles([(i - 1, None), (i, diag)])

        @pl.when(i % 2 == 0)
        def _():
            tiles([(i, diag)])
    else:
        @pl.when(i == 0)
        def _():
            tiles([(i, diag)])

        @pl.when(i == 1)
        def _():
            tiles([(i, diag), (i - 1, None)])

        @pl.when(i >= 2)
        def _():
            tiles([(i, diag), (i - 1, None), (i - 2, kid > qid)])

    for g in range(NSA_KV_GROUPS):
        acc_scr[g * dh:(g + 1) * dh, :] = acc_scr[g * dh:(g + 1) * dh, :] / l_scr[g]

    out = acc_scr[...].T
    for g in range(NSA_KV_GROUPS):
        for h in range(NSA_HPG):
            hh = g * NSA_HPG + h
            o_ref[:, hh * dh:(hh + 1) * dh] = out[h * tq:(h + 1) * tq, g * dh:(g + 1) * dh]


def _flash(qkv, kaug, sel, b, s, mode):
    tq = ATT_TILE
    assert WINDOW == 2 * tq and s % tq == 0
    t = b * s
    nq = s // tq
    aug = kaug.shape[-1]
    vcol = NSA_WIDTH // LANE + {"slc": 3, "win": 5}[mode]
    in_specs = [pl.BlockSpec((tq, NSA_WIDTH), lambda bi, i: (bi * nq + i, 0)),
                pl.BlockSpec((1, NSA_KV_GROUPS, s, aug), lambda bi, i: (bi, 0, 0, 0)),
                pl.BlockSpec((s, LANE), lambda bi, i: (bi, vcol))]
    args = [qkv, kaug, qkv]
    if mode == "slc":
        assert sel.shape[-1] <= NSA_HEAD_DIM
        in_specs.append(pl.BlockSpec((NSA_KV_GROUPS, tq, sel.shape[-1]), lambda bi, i: (0, bi * nq + i, 0)))
        args.append(sel)
    rows = NSA_HPG * tq
    return pl.pallas_call(
        functools.partial(_flash_kernel, mode=mode, tq=tq),
        grid=(b, nq),
        in_specs=in_specs,
        out_specs=pl.BlockSpec((tq, NSA_WIDTH), lambda bi, i: (bi * nq + i, 0)),
        out_shape=jax.ShapeDtypeStruct((t, NSA_WIDTH), F32),
        scratch_shapes=[pltpu.VMEM((NSA_KV_GROUPS, rows, aug), BF16),
                        pltpu.VMEM((NSA_KV_GROUPS, 1, rows), F32),
                        pltpu.VMEM((NSA_KV_GROUPS, 1, rows), F32),
                        pltpu.VMEM((NSA_KV_GROUPS * NSA_HEAD_DIM, rows), F32)],
        compiler_params=_cparams("parallel", "parallel"),
        name="nsa_flash_" + mode,
    )(*args)


HALO = 32
CONV_SHIFTS = 8


def _mix_kernel(ocmp_ref, oslc_ref, owin_ref, gates_ref, glu_ref, halo_ref, merge_ref, x_ref,
                wexp_ref, wnsa_ref, wdw_ref, bdw_ref, gln_ref, bln_ref, wconv_ref, wo_ref,
                o_ref, uext_scr, *, tm, tiles_per_seq):
    i = pl.program_id(0)
    gts = _sigmoid(gates_ref[...])
    g_hi = gts.astype(BF16)
    r1 = gts - g_hi.astype(F32)
    g_mid = r1.astype(BF16)
    g_lo = (r1 - g_mid.astype(F32)).astype(BF16)
    wexp = wexp_ref[...]
    gexp = (jnp.dot(g_hi, wexp, preferred_element_type=F32) + jnp.dot(g_mid, wexp, preferred_element_type=F32)
            + jnp.dot(g_lo, wexp, preferred_element_type=F32))
    w = NSA_WIDTH
    o_nsa = gexp[:, :w] * ocmp_ref[...] + gexp[:, w:2 * w] * oslc_ref[...] + gexp[:, 2 * w:] * owin_ref[...]
    y_a = jnp.dot(o_nsa.astype(BF16), wnsa_ref[...], preferred_element_type=F32)

    c = CONV_CH
    gl = glu_ref[...]
    u = gl[:, :c] * _sigmoid(gl[:, c:])
    hl = halo_ref[...]
    uh = hl[:, :c] * _sigmoid(hl[:, c:])
    uh = jnp.where(i % tiles_per_seq == 0, 0.0, uh)
    uext_scr[0, 0:HALO, :] = uh
    uext_scr[0, HALO:HALO + tm, :] = u
    span = HALO + tm - CONV_SHIFTS
    for j in range(1, CONV_SHIFTS):
        uext_scr[j, 0:span, :] = uext_scr[0, pl.ds(j, span), :]
    acc = jnp.zeros((tm, c), F32)
    for k in range(CONV_WIDTH):
        first = HALO - (CONV_WIDTH - 1) + k
        acc = acc + uext_scr[first % CONV_SHIFTS, pl.ds(first - first % CONV_SHIFTS, tm), :] * wdw_ref[k:k + 1, :]
    cv = acc + bdw_ref[...]
    mu = jnp.mean(cv, axis=-1, keepdims=True)
    var = jnp.mean(jnp.square(cv - mu), axis=-1, keepdims=True)
    un = (cv - mu) * lax.rsqrt(var + EPS) * gln_ref[...] + bln_ref[...]
    act = un * _sigmoid(un)
    y_b = jnp.dot(act.astype(BF16), wconv_ref[...], preferred_element_type=F32)

    d = x_ref.shape[-1]
    mg = merge_ref[...]
    z = _sigmoid(mg[:, :d]) * y_a + _sigmoid(mg[:, d:]) * y_b
    o_ref[...] = x_ref[...] + jnp.dot(z.astype(BF16), wo_ref[...], preferred_element_type=F32)


def _mix(ocmp, oslc, owin, gates, glu, merge, x2, wexp, wnsa, wdw, bdw, gln, bln, wconv, wo, s, tm):
    t, d = x2.shape
    row = lambda n: pl.BlockSpec((tm, n), lambda i: (i, 0))
    full = lambda a: pl.BlockSpec(a.shape, lambda i: tuple(0 for _ in a.shape))
    halo_spec = pl.BlockSpec((HALO, glu.shape[1]), lambda i: (jnp.maximum(i * (tm // HALO) - 1, 0), 0))
    weights = [wexp, wnsa, wdw, bdw, gln, bln, wconv, wo]
    return pl.pallas_call(
        functools.partial(_mix_kernel, tm=tm, tiles_per_seq=s // tm),
        grid=(t // tm,),
        in_specs=[row(NSA_WIDTH), row(NSA_WIDTH), row(NSA_WIDTH), row(GATE_PAD), row(glu.shape[1]),
                  halo_spec, row(merge.shape[1]), row(d)] + [full(a) for a in weights],
        out_specs=row(d),
        out_shape=jax.ShapeDtypeStruct((t, d), F32),
        scratch_shapes=[pltpu.VMEM((CONV_SHIFTS, HALO + tm, CONV_CH), F32)],
        compiler_params=_cparams("parallel", fuse=[False] * 8 + [True] * len(weights)),
        name="mixer_merge",
    )(ocmp, oslc, owin, gates, glu, glu, merge, x2, *weights)


def _extract_topk(jobs, k_top):
    def body(k, carry):
        for cur_ref, rank_ref, val_ref in jobs:
            cur = cur_ref[...]
            n_rows = cur.shape[0]
            rid = lax.broadcasted_iota(jnp.int32, cur.shape, 0).astype(F32)
            v = jnp.max(cur, axis=0, keepdims=True)
            idx = jnp.min(jnp.where(cur == v, rid, float(n_rows)), axis=0, keepdims=True)
            hit = rid == idx
            rank_ref[...] = jnp.where(hit, jnp.asarray(k, F32), rank_ref[...])
            cur_ref[...] = jnp.where(hit, -jnp.inf, cur)
            val_ref[pl.ds(k, 1), :] = v
        return carry

    lax.fori_loop(0, k_top, body, 0)


def _peer_stats_kernel(x_ref, g_ref, wq_ref, keys_ref, xn_ref, e1_ref, n1_ref, e2_ref, r2_ref,
                       c1_scr, c2_scr, r1_scr, r2_scr, v1_scr, v2_scr, cand_scr, csel_scr, cval_scr, *, tm):
    nk = PEER_NKEYS
    kt = PEER_TOPK
    x = x_ref[...]
    xn = (x * lax.rsqrt(jnp.mean(x * x, axis=-1, keepdims=True) + EPS) * g_ref[...]).astype(BF16)
    xn_ref[...] = xn
    big = float(nk)
    n_cand = cand_scr.shape[0]

    def head(h, carry):
        qp = jnp.dot(xn, wq_ref[h], preferred_element_type=F32)
        s1 = lax.dot_general(keys_ref[h, 0], qp[:, :PEER_HALF].astype(BF16), _NT,
                             preferred_element_type=F32)
        s2 = lax.dot_general(keys_ref[h, 1], qp[:, PEER_HALF:].astype(BF16), _NT,
                             preferred_element_type=F32)
        c1_scr[...] = s1
        c2_scr[...] = s2
        r1_scr[...] = jnp.full((nk, tm), big, F32)
        r2_scr[...] = jnp.full((nk, tm), big, F32)
        _extract_topk([(c1_scr, r1_scr, v1_scr), (c2_scr, r2_scr, v2_scr)], kt)
        v1 = v1_scr[...]
        v2 = v2_scr[...]
        pieces = [v1[a:a + 1, :] + v2[0:kt // (a + 1), :] for a in range(kt)]
        pad = n_cand - sum(kt // (a + 1) for a in range(kt))
        pieces.append(jnp.full((pad, tm), -jnp.inf, F32))
        cand = jnp.concatenate(pieces, axis=0)
        cand_scr[...] = cand
        csel_scr[...] = jnp.full((n_cand, tm), big, F32)
        _extract_topk([(cand_scr, csel_scr, cval_scr)], kt)
        sel_f = jnp.where(csel_scr[...] < big, 1.0, 0.0)
        top = v1[0:1, :] + v2[0:1, :]
        z = jnp.sum(sel_f * jnp.exp(cand - top), axis=0, keepdims=True)
        r1 = r1_scr[...]
        n1 = jnp.zeros((nk, tm), F32)
        off = 0
        for a in range(kt):
            cnt = kt // (a + 1)
            n_a = jnp.sum(sel_f[off:off + cnt, :], axis=0, keepdims=True)
            off += cnt
            n1 = jnp.where(r1 == float(a), n_a, n1)
        e1_ref[h] = jnp.exp(s1 - v1[0:1, :])
        n1_ref[h] = n1
        e2_ref[h] = (0.5 * jnp.exp(s2 - v2[0:1, :])) / z
        r2_ref[h] = r2_scr[...]
        return carry

    lax.fori_loop(0, PEER_HEADS, head, 0)


def _peer_stats(x1, g, wq_h, keys_bf16, tm):
    t, d = x1.shape
    nk, kt = PEER_NKEYS, PEER_TOPK
    stat_spec = pl.BlockSpec((PEER_HEADS, nk, tm), lambda i: (0, 0, i))
    stat_shape = jax.ShapeDtypeStruct((PEER_HEADS, nk, t), F32)
    full = lambda a: pl.BlockSpec(a.shape, lambda i: tuple(0 for _ in a.shape))
    n_cells = sum(kt // (a + 1) for a in range(kt))
    n_cand = -(-n_cells // 8) * 8
    return pl.pallas_call(
        functools.partial(_peer_stats_kernel, tm=tm),
        grid=(t // tm,),
        in_specs=[pl.BlockSpec((tm, d), lambda i: (i, 0)), pl.BlockSpec((1, d), lambda i: (0, 0)),
                  full(wq_h), full(keys_bf16)],
        out_specs=[pl.BlockSpec((tm, d), lambda i: (i, 0))] + [stat_spec] * 4,
        out_shape=[jax.ShapeDtypeStruct((t, d), BF16)] + [stat_shape] * 4,
        scratch_shapes=[pltpu.VMEM((nk, tm), F32), pltpu.VMEM((nk, tm), F32),
                        pltpu.VMEM((nk, tm), F32), pltpu.VMEM((nk, tm), F32),
                        pltpu.VMEM((kt, tm), F32), pltpu.VMEM((kt, tm), F32),
                        pltpu.VMEM((n_cand, tm), F32), pltpu.VMEM((n_cand, tm), F32),
                        pltpu.VMEM((kt, tm), F32)],
        compiler_params=_cparams("parallel"),
        name="peer_stats",
    )(x1, g.reshape(1, d), wq_h, keys_bf16)


PEER_CHUNK = 256


def _peer_dense_kernel(xn_ref, u_ref, vt_ref, e1_ref, n1_ref, e2_ref, r2_ref, o_ref, ht_scr, act_scr,
                       *, tm, eb, nb):
    nk = PEER_NKEYS
    s = pl.program_id(0)
    n_keys = eb // nk
    slot = s % 2
    other = 1 - slot

    @pl.when(s == 0)
    def _():
        ht_scr[...] = jnp.zeros(ht_scr.shape, F32)

    @pl.when((s < 2) | ((s - 1) % nb == 0))
    def _():
        o_ref[...] = jnp.zeros(o_ref.shape, F32)

    jb = jnp.maximum(s - 1, 0) % nb
    n1_rows = [[n1_ref[h, pl.ds(jb * n_keys + c, 1), :] for c in range(n_keys)] for h in range(PEER_HEADS)]
    e1_rows = [[e1_ref[h, pl.ds(jb * n_keys + c, 1), :] for c in range(n_keys)] for h in range(PEER_HEADS)]
    ht = ht_scr[other].astype(BF16)
    act_scr[...] = ht * (1.0 + lax.erf(ht * (1.0 / math.sqrt(2.0))))
    pk = BF16_ROWS
    for ch in range(tm // PEER_CHUNK):
        cols = slice(ch * PEER_CHUNK, (ch + 1) * PEER_CHUNK)
        ht_scr[slot, :, cols] = lax.dot_general(u_ref[...], xn_ref[cols, :], _NT, preferred_element_type=F32)
        parts = []
        for lt in range(PEER_CHUNK // LANE):
            lanes = slice(ch * PEER_CHUNK + lt * LANE, ch * PEER_CHUNK + (lt + 1) * LANE)
            grp = [jnp.zeros((nk // pk, pk, LANE), BF16) for _ in range(n_keys)]
            for h in range(PEER_HEADS):
                r2 = r2_ref[h, :, lanes].astype(BF16).reshape(nk // pk, pk, LANE)
                e2 = e2_ref[h, :, lanes].astype(BF16).reshape(nk // pk, pk, LANE)
                for c in range(n_keys):
                    n1 = jnp.broadcast_to(n1_rows[h][c][:, lanes], (pk, LANE)).astype(BF16)
                    e1 = jnp.broadcast_to(e1_rows[h][c][:, lanes], (pk, LANE)).astype(BF16)
                    grp[c] = grp[c] + jnp.where(r2 < n1, e2, jnp.zeros_like(e2)) * e1
            gate = jnp.concatenate([g.reshape(nk, LANE) for g in grp], axis=0)
            parts.append(act_scr[:, lanes] * gate)
        at = jnp.concatenate(parts, axis=1)
        o_ref[:, cols] += jnp.dot(vt_ref[...], at, preferred_element_type=F32)


def _peer_dense(xn, u_bf16, vt_bf16, e1, n1, e2, r2, tm, eb):
    t, d = xn.shape
    n_exp = u_bf16.shape[0]
    nb = n_exp // eb
    n_items = (t // tm) * nb
    item = lambda s, lag: jnp.clip(s - lag, 0, n_items - 1)
    stat_spec = pl.BlockSpec((PEER_HEADS, PEER_NKEYS, tm), lambda s: (0, 0, item(s, 1) // nb),
                             pipeline_mode=pl.Buffered(1))
    return pl.pallas_call(
        functools.partial(_peer_dense_kernel, tm=tm, eb=eb, nb=nb),
        grid=(n_items + 1,),
        in_specs=[pl.BlockSpec((tm, d), lambda s: (item(s, 0) // nb, 0)),
                  pl.BlockSpec((eb, d), lambda s: (item(s, 0) % nb, 0)),
                  pl.BlockSpec((None, d, eb), lambda s: (item(s, 1) % nb, 0, 0)),
                  stat_spec, stat_spec, stat_spec, stat_spec],
        out_specs=pl.BlockSpec((d, tm), lambda s: (0, item(s, 1) // nb)),
        out_shape=jax.ShapeDtypeStruct((d, t), F32),
        scratch_shapes=[pltpu.VMEM((2, eb, tm), F32), pltpu.VMEM((eb, tm), BF16)],
        compiler_params=_cparams("arbitrary", fuse=[False, True, True, False, False, False, False]),
        name="peer_dense",
    )(xn, u_bf16, vt_bf16, e1, n1, e2, r2)


def _final_kernel(x1_ref, yt_ref, g_ref, o_ref):
    y = x1_ref[...] + yt_ref[...].T
    o_ref[...] = y * lax.rsqrt(jnp.mean(y * y, axis=-1, keepdims=True) + EPS) * g_ref[...]


def _final_norm(x1, yt, g_final, tm):
    t, d = x1.shape
    return pl.pallas_call(
        _final_kernel,
        grid=(t // tm,),
        in_specs=[pl.BlockSpec((tm, d), lambda i: (i, 0)), pl.BlockSpec((d, tm), lambda i: (0, i)),
                  pl.BlockSpec((1, d), lambda i: (0, 0))],
        out_specs=pl.BlockSpec((tm, d), lambda i: (i, 0)),
        out_shape=jax.ShapeDtypeStruct((t, d), F32),
        compiler_params=_cparams("parallel"),
        name="peer_residual_final_norm",
    )(x1, yt, g_final.reshape(1, d))


def _gate_expand_matrix():
    r = jnp.arange(GATE_PAD)[:, None]
    c = jnp.arange(3 * NSA_WIDTH)[None, :]
    return ((r < GATE_COLS) & (r == (c // NSA_WIDTH) * NSA_HEADS + (c % NSA_WIDTH) // NSA_HEAD_DIM)).astype(F32)


class _Tiles:
    proj = 256
    attn = ATT_TILE
    key_prep = 512
    mix = 256
    stats = 512
    dense = 1024
    experts = 1024
    final = 512


def kernel(x, g_mix, w_in, pe_cmp_k, pe_cmp_v, w_cmp_k1, w_cmp_k2, w_cmp_v1, w_cmp_v2, w_nsa_out, w_dw, b_dw,
           g_conv_ln, b_conv_ln, w_conv_out, w_o, g_ffn, w_peer_q, peer_sub_keys, peer_u, peer_v, g_final):
    b, s, d = x.shape
    t = b * s
    depth = w_in.shape[0]
    assert depth == 1, "the fused final norm assumes a single layer"
    assert s % 512 == 0 and d % LANE == 0 and t % _Tiles.dense == 0
    x2 = x.reshape(t, d)
    wexp = _gate_expand_matrix().astype(BF16)
    l = 0
    gate_end = QKV_COLS + GATE_COLS
    w_pad = jnp.concatenate([w_in[l][:, :gate_end], jnp.zeros((d, GATE_PAD - GATE_COLS), F32),
                             w_in[l][:, gate_end:]], axis=1).astype(BF16)
    splits = (QKV_COLS, GATE_PAD, 2 * CONV_CH, 2 * d)
    qkv, gates, glu, merge = _norm_proj(x2, g_mix[l], w_pad, splits, tm=_Tiles.proj)

    kcmp, vcmp = _compress(qkv, b, s, pe_cmp_k[l], pe_cmp_v[l], w_cmp_k1[l], w_cmp_k2[l],
                           w_cmp_v1[l], w_cmp_v2[l])
    o_cmp, sel = _cmp_attn(qkv, kcmp, vcmp, b, s, tq=_Tiles.attn)
    kaug_slc, kaug_win = _attn_prep(qkv, b, s, ts=_Tiles.key_prep)
    o_slc = _flash(qkv, kaug_slc, sel, b, s, "slc")
    o_win = _flash(qkv, kaug_win, None, b, s, "win")

    x1 = _mix(o_cmp, o_slc, o_win, gates, glu, merge, x2, wexp,
              w_nsa_out[l].astype(BF16), w_dw[l].reshape(CONV_WIDTH, CONV_CH), b_dw[l].reshape(1, -1),
              g_conv_ln[l].reshape(1, -1), b_conv_ln[l].reshape(1, -1),
              w_conv_out[l].astype(BF16), w_o[l].astype(BF16), s, tm=_Tiles.mix)

    wq_h = w_peer_q[l].reshape(d, PEER_HEADS, PEER_QDIM).transpose(1, 0, 2).astype(BF16)
    xn, e1, n1, e2, r2 = _peer_stats(x1, g_ffn[l], wq_h, peer_sub_keys[l].astype(BF16), tm=_Tiles.stats)
    eb = _Tiles.experts
    n_exp = peer_v.shape[1]
    vt_blocks = peer_v[l].astype(BF16).reshape(n_exp // eb, eb, d).transpose(0, 2, 1)
    yt = _peer_dense(xn, peer_u[l].astype(BF16), vt_blocks, e1, n1, e2, r2, tm=_Tiles.dense, eb=eb)
    out = _final_norm(x1, yt, g_final, tm=_Tiles.final)
    return out.reshape(b, s, d)
```
